```python
import jax, jax.numpy as jnp
from jax import lax
import numpy as np

D_MODEL = 1024
BATCH = 8
SEQ = 4096
DEPTH = 2

GLA_HEADS = 4
GLA_KEY_DIM = D_MODEL // 2
GLA_VAL_DIM = D_MODEL
GLA_HEAD_K = GLA_KEY_DIM // GLA_HEADS
GLA_HEAD_V = GLA_VAL_DIM // GLA_HEADS
GLA_GATE_RANK = 16
GLA_GATE_NORMALIZER = 16.0
GLA_CHUNK = 64
GLA_NORM_EPS = 1e-5
CONV_CH = D_MODEL
CONV_WIDTH = 31
LN_EPS = 1e-5
POOL_WINDOWS = (2, 4, 8, 16)
POOL_GROUPS = 4
POOL_CH = D_MODEL
POOL_GROUP_CH = POOL_CH // POOL_GROUPS
N_BRANCH = 3
IN_SPLITS = (GLA_KEY_DIM, GLA_KEY_DIM, GLA_VAL_DIM, GLA_VAL_DIM, GLA_GATE_RANK, 2 * CONV_CH, POOL_CH, N_BRANCH * D_MODEL)
IN_DIM = int(sum(IN_SPLITS))
IN_OFFSETS = [int(o) for o in np.cumsum(IN_SPLITS)[:-1]]
FFN_DIM = ((8 * D_MODEL // 3 + 255) // 256) * 256
N_EXPERTS = 8
TOP_K = 2
EXPERT_DIM = 7 * D_MODEL // 2
MOE_BLOCK = 256
PLE_DIM = 256
RMS_EPS = 1e-6

kernel_name = 'hybrid_gla_conformer_pool_moe_trunk'


def rms_norm(x, g):
    x32 = x.astype(jnp.float32)
    y = x32 * lax.rsqrt(jnp.mean(x32 * x32, axis=-1, keepdims=True) + RMS_EPS)
    return (y * g.astype(jnp.float32)).astype(x.dtype)


def swiglu(h, w_gate, w_up, w_down):
    return (jax.nn.silu(h @ w_gate) * (h @ w_up)) @ w_down


def gla_chunked(q, k, v, gk):
    B_, S_, H, DK = q.shape
    DV = v.shape[-1]
    C = GLA_CHUNK
    N = S_ // C

    def chunks(t):
        return t.reshape(B_, N, C, H, t.shape[-1]).transpose(1, 0, 3, 2, 4)

    q = chunks(q * (DK ** -0.5))
    k = chunks(k)
    v = chunks(v)
    b = jnp.cumsum(chunks(gk), axis=3)
    b_last = b[:, :, :, C - 1:C, :]
    b_mid = b[:, :, :, C // 2 - 1:C // 2, :]
    causal = jnp.tril(jnp.ones((C, C), dtype=bool))
    attn = jnp.einsum('nbhid,nbhjd->nbhij', q * jnp.exp(b - b_mid), k * jnp.exp(b_mid - b))
    attn = jnp.where(causal, attn, 0.0)
    o_intra = jnp.einsum('nbhij,nbhjv->nbhiv', attn, v)
    q_dec = q * jnp.exp(b)
    k_dec = k * jnp.exp(b_last - b)
    chunk_decay = jnp.exp(b_last[:, :, :, 0, :])

    def step(state, xs):
        qd, kd, vc, dec = xs
        o = jnp.einsum('bhid,bhdv->bhiv', qd, state)
        state = state * dec[..., None] + jnp.einsum('bhjd,bhjv->bhdv', kd, vc)
        return state, o

    state0 = jnp.zeros((B_, H, DK, DV), jnp.float32)
    _, o_inter = lax.scan(step, state0, (q_dec, k_dec, v, chunk_decay))
    o = o_intra + o_inter
    return o.transpose(1, 0, 3, 2, 4).reshape(B_, S_, H, DV)


def conformer_conv(u2, conv_w, conv_b, ln_g, ln_b, w_proj):
    a, g = jnp.split(u2, 2, axis=-1)
    u = a * jax.nn.sigmoid(g)
    y = lax.conv_general_dilated(u, conv_w[:, None, :], window_strides=(1,), padding=[(CONV_WIDTH - 1, 0)], dimension_numbers=('NWC', 'WIO', 'NWC'), feature_group_count=CONV_CH) + conv_b
    y32 = y.astype(jnp.float32)
    mu = jnp.mean(y32, axis=-1, keepdims=True)
    var = jnp.mean(jnp.square(y32 - mu), axis=-1, keepdims=True)
    yn = (y32 - mu) * lax.rsqrt(var + LN_EPS) * ln_g.astype(jnp.float32) + ln_b.astype(jnp.float32)
    return jax.nn.silu(yn).astype(u2.dtype) @ w_proj


def multiscale_pool(u, w_pool, scale):
    B_, S_, _ = u.shape
    ug = u.astype(jnp.float32).reshape(B_, S_, POOL_GROUPS, POOL_GROUP_CH)
    cs = jnp.concatenate([jnp.zeros((B_, 1, POOL_GROUPS, POOL_GROUP_CH), jnp.float32), jnp.cumsum(ug, axis=1)], axis=1)
    t = jnp.arange(S_)
    means = []
    for gi, w in enumerate(POOL_WINDOWS):
        lo = jnp.maximum(t + 1 - w, 0)
        wsum = cs[:, 1:, gi] - jnp.take(cs[:, :, gi], lo, axis=1)
        cnt = jnp.minimum(t + 1, w).astype(jnp.float32)
        means.append(wsum / cnt[None, :, None])
    d = (jnp.stack(means, axis=2) - ug).astype(u.dtype)
    y = jnp.einsum('bsgc,gcd->bsgd', d, w_pool).reshape(B_, S_, POOL_CH)
    return y * scale


def moe_swiglu(h, w_router, b_router, w_gate, w_up, w_down):
    B_, S_, D_ = h.shape
    T = B_ * S_
    xt = h.reshape(T, D_)
    logits = (xt @ w_router).astype(jnp.float32) + b_router.astype(jnp.float32)
    top_logit, top_idx = lax.top_k(logits, TOP_K)
    top_w = jax.nn.softmax(top_logit, axis=-1)
    A = T * TOP_K
    flat_e = top_idx.reshape(A).astype(jnp.int32)
    flat_tok = jnp.repeat(jnp.arange(T, dtype=jnp.int32), TOP_K)
    flat_w = top_w.reshape(A)
    order = jnp.argsort(flat_e, stable=True)
    e_sorted = flat_e[order]
    tok_sorted = flat_tok[order]
    w_sorted = flat_w[order]
    counts = jnp.zeros((N_EXPERTS,), jnp.int32).at[flat_e].add(1)
    padded = (counts + MOE_BLOCK - 1) // MOE_BLOCK * MOE_BLOCK
    pad_end = jnp.cumsum(padded)
    pad_start = pad_end - padded
    start = jnp.cumsum(counts) - counts
    dest = pad_start[e_sorted] + jnp.arange(A, dtype=jnp.int32) - start[e_sorted]
    n_blocks = -(-A // MOE_BLOCK) + N_EXPERTS
    rows = jnp.full((n_blocks * MOE_BLOCK,), T, jnp.int32).at[dest].set(tok_sorted)
    block_start = jnp.arange(n_blocks, dtype=jnp.int32) * MOE_BLOCK
    block_e = jnp.minimum(jnp.searchsorted(pad_end, block_start, side='right'), N_EXPERTS - 1)
    x_pad = jnp.concatenate([xt, jnp.zeros((1, D_), xt.dtype)], axis=0)

    def expert_block(args):
        r, e = args
        return swiglu(x_pad[r], w_gate[e], w_up[e], w_down[e])

    yb = lax.map(expert_block, (rows.reshape(n_blocks, MOE_BLOCK), block_e))
    y_sorted = yb.reshape(n_blocks * MOE_BLOCK, D_)[dest]
    y = jax.ops.segment_sum(y_sorted * w_sorted[:, None].astype(y_sorted.dtype), tok_sorted, num_segments=T)
    return y.reshape(B_, S_, D_)


def setup_inputs(seed: int = 0) -> dict:
    key = jax.random.key(seed)
    keys = jax.random.split(key, 40)
    ctr = [0]

    def nk():
        ctr[0] += 1
        return keys[ctr[0] - 1]

    def nrm(shape, scale):
        return jax.random.normal(nk(), shape, jnp.float32) * scale

    def gain(shape):
        return 1.0 + 0.05 * jax.random.normal(nk(), shape, jnp.float32)

    nd = (DEPTH + 1) // 2
    nm = DEPTH // 2
    return {
        'x': nrm((BATCH, SEQ, D_MODEL), 1.0),
        'p': nrm((DEPTH, BATCH, SEQ, PLE_DIM), 1.0),
        'g_mix': gain((DEPTH, D_MODEL)),
        'w_in': nrm((DEPTH, D_MODEL, IN_DIM), D_MODEL ** -0.5),
        'w_gk_up': nrm((DEPTH, GLA_GATE_RANK, GLA_KEY_DIM), GLA_GATE_RANK ** -0.5),
        'b_gk': nrm((DEPTH, GLA_KEY_DIM), 0.1),
        'g_gla_norm': gain((DEPTH, GLA_HEAD_V)),
        'conv_w': nrm((DEPTH, CONV_WIDTH, CONV_CH), CONV_WIDTH ** -0.5),
        'conv_b': nrm((DEPTH, CONV_CH), 0.02),
        'ln_conv_g': gain((DEPTH, CONV_CH)),
        'ln_conv_b': nrm((DEPTH, CONV_CH), 0.02),
        'w_conv_out': nrm((DEPTH, CONV_CH, D_MODEL), CONV_CH ** -0.5),
        'w_pool': nrm((DEPTH, POOL_GROUPS, POOL_GROUP_CH, POOL_GROUP_CH), POOL_GROUP_CH ** -0.5),
        'pool_scale': gain((DEPTH, POOL_CH)),
        'w_out': nrm((DEPTH, D_MODEL, D_MODEL), D_MODEL ** -0.5),
        'g_ffn': gain((DEPTH, D_MODEL)),
        'w_ffn_gate': nrm((nd, D_MODEL, FFN_DIM), D_MODEL ** -0.5),
        'w_ffn_up': nrm((nd, D_MODEL, FFN_DIM), D_MODEL ** -0.5),
        'w_ffn_down': nrm((nd, FFN_DIM, D_MODEL), FFN_DIM ** -0.5),
        'w_router': nrm((nm, D_MODEL, N_EXPERTS), D_MODEL ** -0.5),
        'b_router': nrm((nm, N_EXPERTS), 0.01),
        'w_moe_gate': nrm((nm, N_EXPERTS, D_MODEL, EXPERT_DIM), D_MODEL ** -0.5),
        'w_moe_up': nrm((nm, N_EXPERTS, D_MODEL, EXPERT_DIM), D_MODEL ** -0.5),
        'w_moe_down': nrm((nm, N_EXPERTS, EXPERT_DIM, D_MODEL), EXPERT_DIM ** -0.5),
        'g_ple': gain((DEPTH, D_MODEL)),
        'w_ple_gate': nrm((DEPTH, D_MODEL, D_MODEL), D_MODEL ** -0.5),
        'w_ple_proj': nrm((DEPTH, PLE_DIM, D_MODEL), PLE_DIM ** -0.5),
        'g_final': gain((D_MODEL,)),
    }


def reference(x, p, g_mix, w_in, w_gk_up, b_gk, g_gla_norm, conv_w, conv_b, ln_conv_g, ln_conv_b, w_conv_out, w_pool, pool_scale, w_out, g_ffn, w_ffn_gate, w_ffn_up, w_ffn_down, w_router, b_router, w_moe_gate, w_moe_up, w_moe_down, g_ple, w_ple_gate, w_ple_proj, g_final):
    B_, S_, _ = x.shape
    f32 = jnp.float32
    for i in range(DEPTH):
        h = rms_norm(x, g_mix[i])
        z = h @ w_in[i]
        q_in, k_in, v_in, g_in, gk_lr, conv_in, pool_in, gate_in = jnp.split(z, IN_OFFSETS, axis=-1)

        gk = jax.nn.log_sigmoid((gk_lr @ w_gk_up[i]).astype(f32) + b_gk[i].astype(f32)) / GLA_GATE_NORMALIZER
        o = gla_chunked(q_in.astype(f32).reshape(B_, S_, GLA_HEADS, GLA_HEAD_K), k_in.astype(f32).reshape(B_, S_, GLA_HEADS, GLA_HEAD_K), v_in.astype(f32).reshape(B_, S_, GLA_HEADS, GLA_HEAD_V), gk.reshape(B_, S_, GLA_HEADS, GLA_HEAD_K))
        o = o * lax.rsqrt(jnp.mean(o * o, axis=-1, keepdims=True) + GLA_NORM_EPS) * g_gla_norm[i].astype(f32)
        o = o * jax.nn.silu(g_in.astype(f32).reshape(B_, S_, GLA_HEADS, GLA_HEAD_V))
        y_gla = o.reshape(B_, S_, GLA_VAL_DIM)

        y_conv = conformer_conv(conv_in, conv_w[i], conv_b[i], ln_conv_g[i], ln_conv_b[i], w_conv_out[i]).astype(f32)

        y_pool = multiscale_pool(pool_in, w_pool[i], pool_scale[i]).astype(f32)

        gates = jax.nn.sigmoid(gate_in.astype(f32)).reshape(B_, S_, N_BRANCH, D_MODEL)
        mix = gates[:, :, 0] * y_gla + gates[:, :, 1] * y_conv + gates[:, :, 2] * y_pool
        x = x + mix.astype(x.dtype) @ w_out[i]

        h = rms_norm(x, g_ffn[i])
        if i % 2 == 0:
            y = swiglu(h, w_ffn_gate[i // 2], w_ffn_up[i // 2], w_ffn_down[i // 2])
        else:
            y = moe_swiglu(h, w_router[i // 2], b_router[i // 2], w_moe_gate[i // 2], w_moe_up[i // 2], w_moe_down[i // 2])
        x = x + y

        hp = rms_norm(x, g_ple[i])
        x = x + jax.nn.sigmoid(hp @ w_ple_gate[i]) * (p[i] @ w_ple_proj[i])
    return rms_norm(x, g_final)
```

```python
import functools

import jax
import jax.numpy as jnp
from jax import lax
from jax.experimental import pallas as pl
from jax.experimental.pallas import tpu as pltpu

F32 = jnp.float32
BF16 = jnp.bfloat16

D_MODEL = 1024
GLA_HEADS = 4
GLA_KEY_DIM = 512
GLA_HEAD_K = 128
GLA_HEAD_V = 256
GLA_GATE_RANK = 16
GLA_GATE_NORMALIZER = 16.0
GLA_CHUNK = 64
GLA_NORM_EPS = 1e-5
CONV_WIDTH = 31
LN_EPS = 1e-5
POOL_WINDOWS = (2, 4, 8, 16)
POOL_GROUP_CH = 256
N_EXPERTS = 8
PLE_DIM = 256
RMS_EPS = 1e-6

Z_Q, Z_K, Z_V, Z_G, Z_CA, Z_CG, Z_POOL, Z_GATE = 0, 512, 1024, 2048, 3072, 4096, 5120, 6144
Z_DIM = 9216
GK_PAD = 128
LANES = 128

CONV_HALO = 32
POOL_HALO = 16
CONV_RC = 32
VMEM_LIMIT = 56 * 1024 * 1024

NEG_BIG = -1e30


def _dot(a, b):
    return jnp.dot(a, b, preferred_element_type=F32)


def _dot_nt(a, b):
    return lax.dot_general(a, b, (((1,), (1,)), ((), ())), preferred_element_type=F32)


def _dot_tn(a, b):
    return lax.dot_general(a, b, (((0,), (0,)), ((), ())), preferred_element_type=F32)


def _sigmoid(x):
    return 1.0 / (1.0 + jnp.exp(-x))


def _rms(x, g, eps):
    return x * lax.rsqrt(jnp.mean(x * x, axis=-1, keepdims=True) + eps) * g


def _split3(x):
    hi = x.astype(BF16)
    r1 = x - hi.astype(F32)
    mid = r1.astype(BF16)
    lo = (r1 - mid.astype(F32)).astype(BF16)
    return hi, mid, lo


def _full(shape):
    n = len(shape)
    return pl.BlockSpec(shape, lambda *_: (0,) * n)


def _in_proj_kernel(x_ref, g_ref, w_ref, wgk_ref, z_ref, zgk_ref, *, nc):
    h = _rms(x_ref[...], g_ref[...], RMS_EPS).astype(BF16)
    for c in range(0, Z_DIM, nc):
        z_ref[:, c:c + nc] = _dot(h, w_ref[:, c:c + nc]).astype(BF16)
    zgk_ref[...] = _dot(h, wgk_ref[...])


def _in_proj(x2, g, w_main, w_gk, *, tm=256, nc=1024):
    T = x2.shape[0]
    return pl.pallas_call(
        functools.partial(_in_proj_kernel, nc=nc),
        out_shape=(jax.ShapeDtypeStruct((T, Z_DIM), BF16), jax.ShapeDtypeStruct((T, GK_PAD), F32)),
        grid=(T // tm,),
        in_specs=[
            pl.BlockSpec((tm, D_MODEL), lambda i: (i, 0)),
            _full((1, D_MODEL)),
            _full((D_MODEL, Z_DIM)),
            _full((D_MODEL, GK_PAD)),
        ],
        out_specs=(pl.BlockSpec((tm, Z_DIM), lambda i: (i, 0)), pl.BlockSpec((tm, GK_PAD), lambda i: (i, 0))),
        compiler_params=pltpu.CompilerParams(dimension_semantics=("arbitrary",), vmem_limit_bytes=VMEM_LIMIT),
        name="in_proj",
    )(x2, g, w_main, w_gk)


def _mixer_kernel(z_ref, zgk_ref, x_ref, wgk_ref, bgk_ref, gnorm_ref, convw_ref, convb_ref, lng_ref, lnb_ref,
                  wco_ref, wpool_ref, pscale_ref, wout_ref, o_ref,
                  state_ref, ubuf_ref, pbuf_ref, ygla_ref, yconv_ref, *, ts):
    s = pl.program_id(1)
    C = GLA_CHUNK

    @pl.when(s == 0)
    def _():
        state_ref[...] = jnp.zeros_like(state_ref)
        ubuf_ref[0:CONV_HALO, :] = jnp.zeros((CONV_HALO, D_MODEL), F32)
        pbuf_ref[0:POOL_HALO, :] = jnp.zeros((POOL_HALO, D_MODEL), F32)

    gk_pre = _dot(zgk_ref[...].astype(BF16), wgk_ref[...]) + bgk_ref[...]
    gk = (jnp.minimum(gk_pre, 0.0) - jnp.log(1.0 + jnp.exp(-jnp.abs(gk_pre)))) * (1.0 / GLA_GATE_NORMALIZER)
    row = lax.broadcasted_iota(jnp.int32, (ts, ts), 0)
    col = lax.broadcasted_iota(jnp.int32, (ts, ts), 1)
    tri = ((jnp.right_shift(row, 6) == jnp.right_shift(col, 6)) & (col <= row)).astype(BF16)
    g_hi, g_mid, g_lo = _split3(gk)
    b_all = _dot(tri, g_hi) + _dot(tri, g_mid) + _dot(tri, g_lo)
    causal = (lax.broadcasted_iota(jnp.int32, (C, C), 1) <= lax.broadcasted_iota(jnp.int32, (C, C), 0))
    scale = GLA_HEAD_K ** -0.5
    gnorm = gnorm_ref[...]
    for c in range(ts // C):
        r0 = c * C
        bc = b_all[r0:r0 + C, :]
        b_mid = bc[C // 2 - 1:C // 2, :]
        b_last = bc[C - 1:C, :]
        e_qa = jnp.exp(bc - b_mid)
        e_ka = jnp.exp(b_mid - bc)
        e_qd = jnp.exp(bc)
        e_kd = jnp.exp(b_last - bc)
        e_dec = jnp.exp(b_last)
        for h in range(GLA_HEADS):
            ks = slice(h * GLA_HEAD_K, (h + 1) * GLA_HEAD_K)
            q = z_ref[r0:r0 + C, Z_Q + h * GLA_HEAD_K:Z_Q + (h + 1) * GLA_HEAD_K].astype(F32) * scale
            k = z_ref[r0:r0 + C, Z_K + h * GLA_HEAD_K:Z_K + (h + 1) * GLA_HEAD_K].astype(F32)
            v = z_ref[r0:r0 + C, Z_V + h * GLA_HEAD_V:Z_V + (h + 1) * GLA_HEAD_V]
            attn = _dot_nt((q * e_qa[:, ks]).astype(BF16), (k * e_ka[:, ks]).astype(BF16))
            attn = jnp.where(causal, attn, 0.0).astype(BF16)
            o_intra = _dot(attn, v)
            st = state_ref[h]
            o_inter = _dot_nt((q * e_qd[:, ks]).astype(BF16), st.astype(BF16))
            state_ref[h] = st * e_dec[:, ks] + _dot_tn(v, (k * e_kd[:, ks]).astype(BF16))
            o = o_intra + o_inter
            o = o * lax.rsqrt(jnp.mean(o * o, axis=-1, keepdims=True) + GLA_NORM_EPS) * gnorm
            gi = z_ref[r0:r0 + C, Z_G + h * GLA_HEAD_V:Z_G + (h + 1) * GLA_HEAD_V].astype(F32)
            ygla_ref[r0:r0 + C, h * GLA_HEAD_V:(h + 1) * GLA_HEAD_V] = o * (gi * _sigmoid(gi))

    ca = z_ref[:, Z_CA:Z_CA + D_MODEL].astype(F32)
    cg = z_ref[:, Z_CG:Z_CG + D_MODEL].astype(F32)
    ubuf_ref[CONV_HALO:CONV_HALO + ts, :] = ca * _sigmoid(cg)
    RC = CONV_RC

    def conv_rows(i, carry):
        base = pl.multiple_of(i * RC, RC)
        for cc in range(D_MODEL // LANES):
            cs = slice(cc * LANES, (cc + 1) * LANES)
            acc = None
            for r in range(8):
                n_rows = RC if r == 0 else RC + 8
                part = None
                for a in range(5):
                    o = 8 * a + r
                    if o < 2 or o > CONV_HALO:
                        continue
                    kk = o - 2
                    term = convw_ref[kk:kk + 1, cs] * ubuf_ref[pl.ds(base + 8 * a, n_rows), cs]
                    part = term if part is None else part + term
                part = part[r:r + RC, :]
                acc = part if acc is None else acc + part
            yconv_ref[pl.ds(base, RC), cs] = acc + convb_ref[:, cs]
        return carry

    lax.fori_loop(0, ts // RC, conv_rows, 0)
    ubuf_ref[0:CONV_HALO, :] = ubuf_ref[ts:ts + CONV_HALO, :]
    yc = yconv_ref[...]
    mu = jnp.mean(yc, axis=-1, keepdims=True)
    ycc = yc - mu
    var = jnp.mean(ycc * ycc, axis=-1, keepdims=True)
    yn = ycc * lax.rsqrt(var + LN_EPS) * lng_ref[...] + lnb_ref[...]
    y_conv = _dot((yn * _sigmoid(yn)).astype(BF16), wco_ref[...])

    pin = z_ref[:, Z_POOL:Z_POOL + D_MODEL].astype(F32)
    pbuf_ref[POOL_HALO:POOL_HALO + ts, :] = pin
    tpos = s * ts + lax.broadcasted_iota(jnp.int32, (ts, 1), 0)
    y_pool = []
    for gi_, w in enumerate(POOL_WINDOWS):
        cs = slice(gi_ * POOL_GROUP_CH, (gi_ + 1) * POOL_GROUP_CH)
        wsum = pin[:, cs]
        for j in range(1, w):
            wsum = wsum + pbuf_ref[POOL_HALO - j:POOL_HALO - j + ts, cs]
        cnt = jnp.minimum(tpos + 1, w).astype(F32)
        d = wsum / cnt - pin[:, cs]
        y_pool.append(_dot(d.astype(BF16), wpool_ref[gi_]) * pscale_ref[:, cs])
    pbuf_ref[0:POOL_HALO, :] = pbuf_ref[ts:ts + POOL_HALO, :]

    mix_parts = []
    for gi_ in range(4):
        cs = slice(gi_ * POOL_GROUP_CH, (gi_ + 1) * POOL_GROUP_CH)
        g0 = _sigmoid(z_ref[:, Z_GATE + gi_ * 256:Z_GATE + (gi_ + 1) * 256].astype(F32))
        g1 = _sigmoid(z_ref[:, Z_GATE + D_MODEL + gi_ * 256:Z_GATE + D_MODEL + (gi_ + 1) * 256].astype(F32))
        g2 = _sigmoid(z_ref[:, Z_GATE + 2 * D_MODEL + gi_ * 256:Z_GATE + 2 * D_MODEL + (gi_ + 1) * 256].astype(F32))
        m = g0 * ygla_ref[:, cs] + g1 * y_conv[:, cs] + g2 * y_pool[gi_]
        mix_parts.append(m.astype(BF16))
    acc = x_ref[...]
    for gi_ in range(4):
        acc = acc + _dot(mix_parts[gi_], wout_ref[gi_ * 256:(gi_ + 1) * 256, :])
    o_ref[...] = acc


def _mixer(z, zgk, x2, wgk, bgk, gnorm, convw, convb, lng, lnb, wco, wpool, pscale, wout, *, B, S, ts=256):
    T = B * S
    nt = S // ts
    rowmap = lambda b, s: (b * nt + s, 0)
    return pl.pallas_call(
        functools.partial(_mixer_kernel, ts=ts),
        out_shape=jax.ShapeDtypeStruct((T, D_MODEL), F32),
        grid=(B, nt),
        in_specs=[
            pl.BlockSpec((ts, Z_DIM), rowmap),
            pl.BlockSpec((ts, GK_PAD), rowmap),
            pl.BlockSpec((ts, D_MODEL), rowmap),
            _full((GK_PAD, GLA_KEY_DIM)),
            _full((1, GLA_KEY_DIM)),
            _full((1, GLA_HEAD_V)),
            _full((CONV_WIDTH, D_MODEL)),
            _full((1, D_MODEL)),
            _full((1, D_MODEL)),
            _full((1, D_MODEL)),
            _full((D_MODEL, D_MODEL)),
            _full((4, POOL_GROUP_CH, POOL_GROUP_CH)),
            _full((1, D_MODEL)),
            _full((D_MODEL, D_MODEL)),
        ],
        out_specs=pl.BlockSpec((ts, D_MODEL), rowmap),
        scratch_shapes=[
            pltpu.VMEM((GLA_HEADS, GLA_HEAD_V, GLA_HEAD_K), F32),
            pltpu.VMEM((CONV_HALO + ts, D_MODEL), F32),
            pltpu.VMEM((POOL_HALO + ts, D_MODEL), F32),
            pltpu.VMEM((ts, D_MODEL), F32),
            pltpu.VMEM((ts, D_MODEL), F32),
        ],
        compiler_params=pltpu.CompilerParams(dimension_semantics=("arbitrary", "arbitrary"), vmem_limit_bytes=VMEM_LIMIT),
        name="token_mixer",
    )(z, zgk, x2, wgk, bgk, gnorm, convw, convb, lng, lnb, wco, wpool, pscale, wout)


def _ple(x1, p, gple, wpg, wpp):
    hp = _rms(x1, gple, RMS_EPS).astype(BF16)
    return x1 + _sigmoid(_dot(hp, wpg)) * _dot(p.astype(BF16), wpp)


def _ffn_ple_kernel(x_ref, p_ref, gffn_ref, wg_ref, wu_ref, wd_ref, gple_ref, wpg_ref, wpp_ref, o_ref, *, fc):
    x = x_ref[...]
    h = _rms(x, gffn_ref[...], RMS_EPS).astype(BF16)
    F = wg_ref.shape[1]
    y = x
    for c in range(0, F, fc):
        g = _dot(h, wg_ref[:, c:c + fc])
        u = _dot(h, wu_ref[:, c:c + fc])
        y = y + _dot((g * _sigmoid(g) * u).astype(BF16), wd_ref[c:c + fc, :])
    o_ref[...] = _ple(y, p_ref[...], gple_ref[...], wpg_ref[...], wpp_ref[...])


def _ffn_ple(x2, p2, gffn, wg, wu, wd, gple, wpg, wpp, *, tm=256):
    T = x2.shape[0]
    F = wg.shape[1]
    return pl.pallas_call(
        functools.partial(_ffn_ple_kernel, fc=F // 2),
        out_shape=jax.ShapeDtypeStruct((T, D_MODEL), F32),
        grid=(T // tm,),
        in_specs=[
            pl.BlockSpec((tm, D_MODEL), lambda i: (i, 0)),
            pl.BlockSpec((tm, PLE_DIM), lambda i: (i, 0)),
            _full((1, D_MODEL)),
            _full((D_MODEL, F)),
            _full((D_MODEL, F)),
            _full((F, D_MODEL)),
            _full((1, D_MODEL)),
            _full((D_MODEL, D_MODEL)),
            _full((PLE_DIM, D_MODEL)),
        ],
        out_specs=pl.BlockSpec((tm, D_MODEL), lambda i: (i, 0)),
        compiler_params=pltpu.CompilerParams(dimension_semantics=("arbitrary",), vmem_limit_bytes=VMEM_LIMIT),
        name="ffn_ple",
    )(x2, p2, gffn, wg, wu, wd, gple, wpg, wpp)


def _router_kernel(x_ref, g_ref, whi_ref, wlo_ref, br_ref, h_ref, ri_ref, rw_ref, cnt_ref, run_ref, *, tm):
    i = pl.program_id(0)

    @pl.when(i == 0)
    def _():
        run_ref[...] = jnp.zeros_like(run_ref)

    h = _rms(x_ref[...], g_ref[...], RMS_EPS)
    h_ref[...] = h
    hh = h.astype(BF16)
    hl = (h - hh.astype(F32)).astype(BF16)
    logits = _dot(hh, whi_ref[...]) + _dot(hh, wlo_ref[...]) + _dot(hl, whi_ref[...]) + br_ref[...]
    lane = lax.broadcasted_iota(jnp.int32, (tm, LANES), 1)
    m1 = jnp.max(logits, axis=-1, keepdims=True)
    i1 = jnp.min(jnp.where(logits == m1, lane, LANES), axis=-1, keepdims=True)
    l2 = jnp.where(lane == i1, -jnp.inf, logits)
    m2 = jnp.max(l2, axis=-1, keepdims=True)
    i2 = jnp.min(jnp.where(l2 == m2, lane, LANES), axis=-1, keepdims=True)
    e = jnp.exp(m2 - m1)
    w1 = 1.0 / (1.0 + e)
    w2 = e / (1.0 + e)
    sel1 = lane == i1
    sel2 = lane == i2
    onehot = (sel1 | sel2).astype(BF16)
    row = lax.broadcasted_iota(jnp.int32, (tm, tm), 0)
    col = lax.broadcasted_iota(jnp.int32, (tm, tm), 1)
    before = _dot((col < row).astype(BF16), onehot) + run_ref[...]
    pos1 = jnp.sum(jnp.where(sel1, before, 0.0), axis=-1, keepdims=True).astype(jnp.int32)
    pos2 = jnp.sum(jnp.where(sel2, before, 0.0), axis=-1, keepdims=True).astype(jnp.int32)
    run = run_ref[...] + jnp.sum(onehot.astype(F32), axis=0, keepdims=True)
    run_ref[...] = run
    cnt_ref[...] = run
    ri_ref[...] = jnp.where(lane == 0, i1, jnp.where(lane == 1, i2, jnp.where(lane == 2, pos1, jnp.where(lane == 3, pos2, 0))))
    rw_ref[...] = jnp.where(lane == 0, w1, jnp.where(lane == 1, w2, 0.0))


def _router(x2, g, whi, wlo, br, *, tm=256):
    T = x2.shape[0]
    return pl.pallas_call(
        functools.partial(_router_kernel, tm=tm),
        out_shape=(
            jax.ShapeDtypeStruct((T, D_MODEL), F32),
            jax.ShapeDtypeStruct((T, LANES), jnp.int32),
            jax.ShapeDtypeStruct((T, LANES), F32),
            jax.ShapeDtypeStruct((1, LANES), F32),
        ),
        grid=(T // tm,),
        in_specs=[
            pl.BlockSpec((tm, D_MODEL), lambda i: (i, 0)),
            _full((1, D_MODEL)),
            _full((D_MODEL, LANES)),
            _full((D_MODEL, LANES)),
            _full((1, LANES)),
        ],
        out_specs=(
            pl.BlockSpec((tm, D_MODEL), lambda i: (i, 0)),
            pl.BlockSpec((tm, LANES), lambda i: (i, 0)),
            pl.BlockSpec((tm, LANES), lambda i: (i, 0)),
            _full((1, LANES)),
        ),
        scratch_shapes=[pltpu.VMEM((1, LANES), F32)],
        compiler_params=pltpu.CompilerParams(dimension_semantics=("arbitrary",), vmem_limit_bytes=VMEM_LIMIT),
        name="router",
    )(x2, g, whi, wlo, br)


def _row_copy(src_ref, src_row, dst_ref, dst_row, sem):
    return pltpu.make_async_copy(src_ref.at[pl.ds(src_row, 1)], dst_ref.at[pl.ds(dst_row, 1)], sem)


def _dispatch_kernel(dest_ref, h_ref, xs_in_ref, xs_ref, sem, *, td):
    del xs_in_ref

    def start(r, carry):
        _row_copy(h_ref, r, xs_ref, dest_ref[2 * r], sem).start()
        _row_copy(h_ref, r, xs_ref, dest_ref[2 * r + 1], sem).start()
        return carry

    lax.fori_loop(0, td, start, 0)

    def wait(r, carry):
        _row_copy(h_ref, 0, xs_ref, 0, sem).wait()
        _row_copy(h_ref, 0, xs_ref, 0, sem).wait()
        return carry

    lax.fori_loop(0, td, wait, 0)


def _dispatch(dest_flat, h, xs_zero, *, td=512):
    T = h.shape[0]
    return pl.pallas_call(
        functools.partial(_dispatch_kernel, td=td),
        out_shape=jax.ShapeDtypeStruct(xs_zero.shape, xs_zero.dtype),
        grid=(T // td,),
        in_specs=[
            pl.BlockSpec((2 * td,), lambda i: (i,), memory_space=pltpu.SMEM),
            pl.BlockSpec((td, D_MODEL), lambda i: (i, 0)),
            pl.BlockSpec(memory_space=pl.ANY),
        ],
        out_specs=pl.BlockSpec(memory_space=pl.ANY),
        scratch_shapes=[pltpu.SemaphoreType.DMA],
        input_output_aliases={2: 0},
        compiler_params=pltpu.CompilerParams(dimension_semantics=("arbitrary",), vmem_limit_bytes=VMEM_LIMIT),
        name="moe_dispatch",
    )(dest_flat, h, xs_zero)


def _gmm_kernel(be_ref, nu_ref, xs_ref, wg_ref, wu_ref, wd_ref, o_ref):
    i = pl.program_id(0)
    f = pl.program_id(1)

    @pl.when(i < nu_ref[0])
    def _():
        x = xs_ref[...].astype(BF16)
        g = _dot(x, wg_ref[0])
        u = _dot(x, wu_ref[0])
        y = _dot((g * _sigmoid(g) * u).astype(BF16), wd_ref[0])

        @pl.when(f == 0)
        def _():
            o_ref[...] = y

        @pl.when(f > 0)
        def _():
            o_ref[...] += y

    @pl.when((i >= nu_ref[0]) & (f == 0))
    def _():
        o_ref[...] = jnp.zeros_like(o_ref)


def _gmm(block_e, n_used, xs, wg, wu, wd, *, tmm, tf):
    n_rows = xs.shape[0]
    n_blocks = n_rows // tmm
    F = wg.shape[2]
    nf = F // tf

    def row_map(i, f, be, nu):
        return (jnp.minimum(i, nu[0] - 1), 0)

    def f_idx(i, f, nu):
        return jnp.where(i < nu[0], f, nf - 1)

    return pl.pallas_call(
        _gmm_kernel,
        out_shape=jax.ShapeDtypeStruct((n_rows, D_MODEL), F32),
        grid_spec=pltpu.PrefetchScalarGridSpec(
            num_scalar_prefetch=2,
            grid=(n_blocks, nf),
            in_specs=[
                pl.BlockSpec((tmm, D_MODEL), row_map),
                pl.BlockSpec((1, D_MODEL, tf), lambda i, f, be, nu: (be[i], 0, f_idx(i, f, nu))),
                pl.BlockSpec((1, D_MODEL, tf), lambda i, f, be, nu: (be[i], 0, f_idx(i, f, nu))),
                pl.BlockSpec((1, tf, D_MODEL), lambda i, f, be, nu: (be[i], f_idx(i, f, nu), 0)),
            ],
            out_specs=pl.BlockSpec((tmm, D_MODEL), lambda i, f, be, nu: (i, 0)),
        ),
        compiler_params=pltpu.CompilerParams(dimension_semantics=("arbitrary", "arbitrary"), vmem_limit_bytes=VMEM_LIMIT),
        name="moe_grouped_swiglu",
    )(block_e, n_used, xs, wg, wu, wd)


def _combine_ple_kernel(dest_ref, x_ref, rw_ref, p_ref, gple_ref, wpg_ref, wpp_ref, gfin_ref, yb_ref, o_ref,
                        buf_ref, sem, *, tp):
    def start(r, carry):
        _row_copy(yb_ref, dest_ref[2 * r], buf_ref.at[0], r, sem).start()
        _row_copy(yb_ref, dest_ref[2 * r + 1], buf_ref.at[1], r, sem).start()
        return carry

    lax.fori_loop(0, tp, start, 0)

    def wait(r, carry):
        _row_copy(yb_ref, 0, buf_ref.at[0], 0, sem).wait()
        _row_copy(yb_ref, 0, buf_ref.at[1], 0, sem).wait()
        return carry

    lax.fori_loop(0, tp, wait, 0)
    rw = rw_ref[...]
    x1 = x_ref[...] + rw[:, 0:1] * buf_ref[0] + rw[:, 1:2] * buf_ref[1]
    x2 = _ple(x1, p_ref[...], gple_ref[...], wpg_ref[...], wpp_ref[...])
    o_ref[...] = _rms(x2, gfin_ref[...], RMS_EPS)


def _combine_ple(dest_flat, x2, rw, p2, gple, wpg, wpp, gfin, yb, *, tp=256):
    T = x2.shape[0]
    return pl.pallas_call(
        functools.partial(_combine_ple_kernel, tp=tp),
        out_shape=jax.ShapeDtypeStruct((T, D_MODEL), F32),
        grid=(T // tp,),
        in_specs=[
            pl.BlockSpec((2 * tp,), lambda i: (i,), memory_space=pltpu.SMEM),
            pl.BlockSpec((tp, D_MODEL), lambda i: (i, 0)),
            pl.BlockSpec((tp, LANES), lambda i: (i, 0)),
            pl.BlockSpec((tp, PLE_DIM), lambda i: (i, 0)),
            _full((1, D_MODEL)),
            _full((D_MODEL, D_MODEL)),
            _full((PLE_DIM, D_MODEL)),
            _full((1, D_MODEL)),
            pl.BlockSpec(memory_space=pl.ANY),
        ],
        out_specs=pl.BlockSpec((tp, D_MODEL), lambda i: (i, 0)),
        scratch_shapes=[pltpu.VMEM((2, tp, D_MODEL), F32), pltpu.SemaphoreType.DMA],
        compiler_params=pltpu.CompilerParams(dimension_semantics=("arbitrary",), vmem_limit_bytes=VMEM_LIMIT),
        name="moe_combine_ple",
    )(dest_flat, x2, rw, p2, gple, wpg, wpp, gfin, yb)


def _moe_ple_final(x2, p2, gffn, w_router, b_router, wg, wu, wd, gple, wpg, wpp, gfin, *, tmm=512, tf=1792):
    T = x2.shape[0]
    wr = jnp.zeros((D_MODEL, LANES), F32).at[:, :N_EXPERTS].set(w_router)
    whi = wr.astype(BF16)
    wlo = (wr - whi.astype(F32)).astype(BF16)
    br = jnp.full((1, LANES), NEG_BIG, F32).at[0, :N_EXPERTS].set(b_router)
    h, ri, rw, cnt = _router(x2, gffn, whi, wlo, br)
    counts = cnt[0, :N_EXPERTS].astype(jnp.int32)
    padded = (counts + tmm - 1) // tmm * tmm
    pad_end = jnp.cumsum(padded)
    pad_start = pad_end - padded
    dest = pad_start[ri[:, 0:2]] + ri[:, 2:4]
    dest_flat = dest.reshape(2 * T).astype(jnp.int32)
    n_blocks = (2 * T) // tmm + N_EXPERTS
    block_start = jnp.arange(n_blocks, dtype=jnp.int32) * tmm
    block_e = jnp.minimum(jnp.searchsorted(pad_end, block_start, side='right'), N_EXPERTS - 1).astype(jnp.int32)
    n_used = (pad_end[-1] // tmm).astype(jnp.int32).reshape(1)
    xs = _dispatch(dest_flat, h, jnp.zeros((n_blocks * tmm, D_MODEL), F32))
    yb = _gmm(block_e, n_used, xs, wg, wu, wd, tmm=tmm, tf=tf)
    return _combine_ple(dest_flat, x2, rw, p2, gple, wpg, wpp, gfin, yb)


def kernel(x, p, g_mix, w_in, w_gk_up, b_gk, g_gla_norm, conv_w, conv_b, ln_conv_g, ln_conv_b, w_conv_out, w_pool, pool_scale, w_out, g_ffn, w_ffn_gate, w_ffn_up, w_ffn_down, w_router, b_router, w_moe_gate, w_moe_up, w_moe_down, g_ple, w_ple_gate, w_ple_proj, g_final):
    B, S, _ = x.shape
    T = B * S
    depth = w_in.shape[0]
    assert depth == 2 and x.shape[2] == D_MODEL, "dense layer 0, routed layer 1 with the final norm fused"
    x2 = x.reshape(T, D_MODEL)
    row = lambda a: a.reshape(1, -1)
    for i in range(depth):
        wi = w_in[i]
        w_main = jnp.concatenate([wi[:, :3072], wi[:, 3088:]], axis=1).astype(BF16)
        w_gk = jnp.zeros((D_MODEL, GK_PAD), F32).at[:, :GLA_GATE_RANK].set(wi[:, 3072:3088]).astype(BF16)
        w_gk_up_p = jnp.zeros((GK_PAD, GLA_KEY_DIM), F32).at[:GLA_GATE_RANK].set(w_gk_up[i]).astype(BF16)
        z, zgk = _in_proj(x2, row(g_mix[i]), w_main, w_gk)
        x2 = _mixer(z, zgk, x2, w_gk_up_p, row(b_gk[i]), row(g_gla_norm[i]), conv_w[i], row(conv_b[i]),
                    row(ln_conv_g[i]), row(ln_conv_b[i]), w_conv_out[i].astype(BF16), w_pool[i].astype(BF16),
                    row(pool_scale[i]), w_out[i].astype(BF16), B=B, S=S)
        p2 = p[i].reshape(T, PLE_DIM)
        wpg = w_ple_gate[i].astype(BF16)
        wpp = w_ple_proj[i].astype(BF16)
        if i % 2 == 0:
            j = i // 2
            x2 = _ffn_ple(x2, p2, row(g_ffn[i]), w_ffn_gate[j].astype(BF16), w_ffn_up[j].astype(BF16),
                          w_ffn_down[j].astype(BF16), row(g_ple[i]), wpg, wpp)
        else:
            j = i // 2
            x2 = _moe_ple_final(x2, p2, row(g_ffn[i]), w_router[j], b_router[j], w_moe_gate[j].astype(BF16),
                                w_moe_up[j].astype(BF16), w_moe_down[j].astype(BF16), row(g_ple[i]), wpg, wpp,
                                row(g_final))
    return x2.reshape(B, S, D_MODEL)
```

```python
import functools

import jax
import jax.numpy as jnp
from jax import lax
from jax.experimental import pallas as pl
from jax.experimental.pallas import tpu as pltpu

F32 = jnp.float32
BF16 = jnp.bfloat16

D_MODEL = 1024
GLA_HEADS = 4
GLA_KEY_DIM = 512
GLA_HEAD_K = 128
GLA_HEAD_V = 256
GLA_GATE_RANK = 16
GLA_GATE_NORMALIZER = 16.0
GLA_CHUNK = 64
GLA_NORM_EPS = 1e-5
CONV_WIDTH = 31
LN_EPS = 1e-5
POOL_WINDOWS = (2, 4, 8, 16)
POOL_GROUP_CH = 256
N_EXPERTS = 8
PLE_DIM = 256
RMS_EPS = 1e-6

Z_Q, Z_K, Z_V, Z_G, Z_CA, Z_CG, Z_POOL, Z_GATE = 0, 512, 1024, 2048, 3072, 4096, 5120, 6144
Z_DIM = 9216
GK_PAD = 128
LANES = 128

CONV_HALO = 32
POOL_HALO = 128
CONV_RC = 32
VMEM_LIMIT = 56 * 1024 * 1024

NEG_BIG = -1e30
DMA_UNROLL = 8


def _dot(a, b):
    return jnp.dot(a, b, preferred_element_type=F32)


def _dot_nt(a, b):
    return lax.dot_general(a, b, (((1,), (1,)), ((), ())), preferred_element_type=F32)


def _dot_tn(a, b):
    return lax.dot_general(a, b, (((0,), (0,)), ((), ())), preferred_element_type=F32)


def _sigmoid(x):
    return 1.0 / (1.0 + jnp.exp(-x))


def _rms(x, g, eps):
    return x * lax.rsqrt(jnp.mean(x * x, axis=-1, keepdims=True) + eps) * g


def _split3(x):
    hi = x.astype(BF16)
    r1 = x - hi.astype(F32)
    mid = r1.astype(BF16)
    lo = (r1 - mid.astype(F32)).astype(BF16)
    return hi, mid, lo


def _full(shape):
    n = len(shape)
    return pl.BlockSpec(shape, lambda *_: (0,) * n)


def _in_proj_kernel(x_ref, g_ref, w_ref, wgk_ref, z_ref, zgk_ref, *, nc):
    h = _rms(x_ref[...], g_ref[...], RMS_EPS).astype(BF16)
    for c in range(0, Z_DIM, nc):
        z_ref[:, c:c + nc] = _dot(h, w_ref[:, c:c + nc]).astype(BF16)
    zgk_ref[...] = _dot(h, wgk_ref[...])


def _in_proj(x2, g, w_main, w_gk, *, tm=256, nc=1024):
    T = x2.shape[0]
    return pl.pallas_call(
        functools.partial(_in_proj_kernel, nc=nc),
        out_shape=(jax.ShapeDtypeStruct((T, Z_DIM), BF16), jax.ShapeDtypeStruct((T, GK_PAD), F32)),
        grid=(T // tm,),
        in_specs=[
            pl.BlockSpec((tm, D_MODEL), lambda i: (i, 0)),
            _full((1, D_MODEL)),
            _full((D_MODEL, Z_DIM)),
            _full((D_MODEL, GK_PAD)),
        ],
        out_specs=(pl.BlockSpec((tm, Z_DIM), lambda i: (i, 0)), pl.BlockSpec((tm, GK_PAD), lambda i: (i, 0))),
        compiler_params=pltpu.CompilerParams(dimension_semantics=("arbitrary",), vmem_limit_bytes=VMEM_LIMIT),
        name="in_proj",
    )(x2, g, w_main, w_gk)


def _mixer_kernel(z_ref, zgk_ref, x_ref, wgk_ref, bgk_ref, gnorm_ref, convw_ref, convb_ref, lng_ref, lnb_ref,
                  wco_ref, wpool_ref, pscale_ref, wout_ref, o_ref,
                  state_ref, ubuf_ref, pbuf_ref, ygla_ref, yconv_ref, *, ts):
    s = pl.program_id(1)
    C = GLA_CHUNK

    @pl.when(s == 0)
    def _():
        state_ref[...] = jnp.zeros_like(state_ref)
        ubuf_ref[0:CONV_HALO, :] = jnp.zeros((CONV_HALO, D_MODEL), F32)
        pbuf_ref[0:POOL_HALO, :] = jnp.zeros((POOL_HALO, D_MODEL), BF16)

    gk_pre = _dot(zgk_ref[...].astype(BF16), wgk_ref[...]) + bgk_ref[...]
    gk = (jnp.minimum(gk_pre, 0.0) - jnp.log(1.0 + jnp.exp(-jnp.abs(gk_pre)))) * (1.0 / GLA_GATE_NORMALIZER)
    row = lax.broadcasted_iota(jnp.int32, (ts, ts), 0)
    col = lax.broadcasted_iota(jnp.int32, (ts, ts), 1)
    same_chunk_causal = (jnp.right_shift(row, 6) == jnp.right_shift(col, 6)) & (col <= row)
    tri = same_chunk_causal.astype(BF16)
    g_hi, g_mid, g_lo = _split3(gk)
    b_all = _dot(tri, g_hi) + _dot(tri, g_mid) + _dot(tri, g_lo)
    nchunk = ts // C

    def per_chunk_row(r):
        return jnp.concatenate(
            [jnp.broadcast_to(b_all[c * C + r:c * C + r + 1, :], (C, GLA_KEY_DIM)) for c in range(nchunk)], axis=0)

    b_mid = per_chunk_row(C // 2 - 1)
    b_last = per_chunk_row(C - 1)
    q = z_ref[:, Z_Q:Z_Q + GLA_KEY_DIM].astype(F32) * (GLA_HEAD_K ** -0.5)
    k = z_ref[:, Z_K:Z_K + GLA_KEY_DIM].astype(F32)
    qa = (q * jnp.exp(b_all - b_mid)).astype(BF16)
    ka = (k * jnp.exp(b_mid - b_all)).astype(BF16)
    qd = (q * jnp.exp(b_all)).astype(BF16)
    kd = (k * jnp.exp(b_last - b_all)).astype(BF16)
    e_dec = [jnp.exp(b_all[c * C + C - 1:c * C + C, :]) for c in range(nchunk)]
    gnorm = gnorm_ref[...]
    for h in range(GLA_HEADS):
        ks = slice(h * GLA_HEAD_K, (h + 1) * GLA_HEAD_K)
        v = z_ref[:, Z_V + h * GLA_HEAD_V:Z_V + (h + 1) * GLA_HEAD_V]
        attn = jnp.where(same_chunk_causal, _dot_nt(qa[:, ks], ka[:, ks]), 0.0).astype(BF16)
        o = _dot(attn, v)
        st = state_ref[h]
        o_inter = []
        for c in range(nchunk):
            rs = slice(c * C, (c + 1) * C)
            o_inter.append(_dot_nt(qd[rs, ks], st.astype(BF16)))
            st = st * e_dec[c][:, ks] + _dot_tn(v[rs, :], kd[rs, ks])
        state_ref[h] = st
        o = o + jnp.concatenate(o_inter, axis=0)
        o = o * lax.rsqrt(jnp.mean(o * o, axis=-1, keepdims=True) + GLA_NORM_EPS) * gnorm
        gi = z_ref[:, Z_G + h * GLA_HEAD_V:Z_G + (h + 1) * GLA_HEAD_V].astype(F32)
        ygla_ref[:, h * GLA_HEAD_V:(h + 1) * GLA_HEAD_V] = o * (gi * _sigmoid(gi))

    ca = z_ref[:, Z_CA:Z_CA + D_MODEL].astype(F32)
    cg = z_ref[:, Z_CG:Z_CG + D_MODEL].astype(F32)
    ubuf_ref[CONV_HALO:CONV_HALO + ts, :] = ca * _sigmoid(cg)
    RC = CONV_RC

    def conv_rows(i, carry):
        base = pl.multiple_of(i * RC, RC)
        for cc in range(D_MODEL // LANES):
            cs = slice(cc * LANES, (cc + 1) * LANES)
            acc = None
            for r in range(8):
                n_rows = RC if r == 0 else RC + 8
                part = None
                for a in range(5):
                    o = 8 * a + r
                    if o < 2 or o > CONV_HALO:
                        continue
                    kk = o - 2
                    term = convw_ref[kk:kk + 1, cs] * ubuf_ref[pl.ds(base + 8 * a, n_rows), cs]
                    part = term if part is None else part + term
                part = part[r:r + RC, :]
                acc = part if acc is None else acc + part
            yconv_ref[pl.ds(base, RC), cs] = acc + convb_ref[:, cs]
        return carry

    lax.fori_loop(0, ts // RC, conv_rows, 0)
    ubuf_ref[0:CONV_HALO, :] = ubuf_ref[ts:ts + CONV_HALO, :]
    yc = yconv_ref[...]
    mu = jnp.mean(yc, axis=-1, keepdims=True)
    ycc = yc - mu
    var = jnp.mean(ycc * ycc, axis=-1, keepdims=True)
    yn = ycc * lax.rsqrt(var + LN_EPS) * lng_ref[...] + lnb_ref[...]
    y_conv = _dot((yn * _sigmoid(yn)).astype(BF16), wco_ref[...])

    pbuf_ref[POOL_HALO:POOL_HALO + ts, :] = z_ref[:, Z_POOL:Z_POOL + D_MODEL]
    tpos = s * ts + lax.broadcasted_iota(jnp.int32, (ts, 1), 0)
    prow = lax.broadcasted_iota(jnp.int32, (ts, POOL_HALO + ts), 0) + POOL_HALO
    pcol = lax.broadcasted_iota(jnp.int32, (ts, POOL_HALO + ts), 1)
    y_pool = []
    for gi_, w in enumerate(POOL_WINDOWS):
        cs = slice(gi_ * POOL_GROUP_CH, (gi_ + 1) * POOL_GROUP_CH)
        band = ((pcol <= prow) & (pcol > prow - w)).astype(BF16)
        wsum = _dot(band, pbuf_ref[:, cs])
        cnt = jnp.minimum(tpos + 1, w).astype(F32)
        d = wsum / cnt - z_ref[:, Z_POOL + gi_ * POOL_GROUP_CH:Z_POOL + (gi_ + 1) * POOL_GROUP_CH].astype(F32)
        y_pool.append(_dot(d.astype(BF16), wpool_ref[gi_]) * pscale_ref[:, cs])
    pbuf_ref[0:POOL_HALO, :] = pbuf_ref[ts:ts + POOL_HALO, :]

    mix_parts = []
    for gi_ in range(4):
        cs = slice(gi_ * POOL_GROUP_CH, (gi_ + 1) * POOL_GROUP_CH)
        g0 = _sigmoid(z_ref[:, Z_GATE + gi_ * 256:Z_GATE + (gi_ + 1) * 256])
        g1 = _sigmoid(z_ref[:, Z_GATE + D_MODEL + gi_ * 256:Z_GATE + D_MODEL + (gi_ + 1) * 256])
        g2 = _sigmoid(z_ref[:, Z_GATE + 2 * D_MODEL + gi_ * 256:Z_GATE + 2 * D_MODEL + (gi_ + 1) * 256])
        m = g0 * ygla_ref[:, cs].astype(BF16) + g1 * y_conv[:, cs].astype(BF16) + g2 * y_pool[gi_].astype(BF16)
        mix_parts.append(m)
    acc = x_ref[...]
    for gi_ in range(4):
        acc = acc + _dot(mix_parts[gi_], wout_ref[gi_ * 256:(gi_ + 1) * 256, :])
    o_ref[...] = acc


def _mixer(z, zgk, x2, wgk, bgk, gnorm, convw, convb, lng, lnb, wco, wpool, pscale, wout, *, B, S, ts=256):
    T = B * S
    nt = S // ts
    rowmap = lambda b, s: (b * nt + s, 0)
    return pl.pallas_call(
        functools.partial(_mixer_kernel, ts=ts),
        out_shape=jax.ShapeDtypeStruct((T, D_MODEL), F32),
        grid=(B, nt),
        in_specs=[
            pl.BlockSpec((ts, Z_DIM), rowmap),
            pl.BlockSpec((ts, GK_PAD), rowmap),
            pl.BlockSpec((ts, D_MODEL), rowmap),
            _full((GK_PAD, GLA_KEY_DIM)),
            _full((1, GLA_KEY_DIM)),
            _full((1, GLA_HEAD_V)),
            _full((CONV_WIDTH, D_MODEL)),
            _full((1, D_MODEL)),
            _full((1, D_MODEL)),
            _full((1, D_MODEL)),
            _full((D_MODEL, D_MODEL)),
            _full((4, POOL_GROUP_CH, POOL_GROUP_CH)),
            _full((1, D_MODEL)),
            _full((D_MODEL, D_MODEL)),
        ],
        out_specs=pl.BlockSpec((ts, D_MODEL), rowmap),
        scratch_shapes=[
            pltpu.VMEM((GLA_HEADS, GLA_HEAD_V, GLA_HEAD_K), F32),
            pltpu.VMEM((CONV_HALO + ts, D_MODEL), F32),
            pltpu.VMEM((POOL_HALO + ts, D_MODEL), BF16),
            pltpu.VMEM((ts, D_MODEL), F32),
            pltpu.VMEM((ts, D_MODEL), F32),
        ],
        compiler_params=pltpu.CompilerParams(dimension_semantics=("arbitrary", "arbitrary"), vmem_limit_bytes=VMEM_LIMIT),
        name="token_mixer",
    )(z, zgk, x2, wgk, bgk, gnorm, convw, convb, lng, lnb, wco, wpool, pscale, wout)


def _ple(x1, p, gple, wpg, wpp):
    hp = _rms(x1, gple, RMS_EPS).astype(BF16)
    return x1 + _sigmoid(_dot(hp, wpg)) * _dot(p.astype(BF16), wpp)


def _ffn_ple_kernel(x_ref, p_ref, gffn_ref, wg_ref, wu_ref, wd_ref, gple_ref, wpg_ref, wpp_ref, o_ref, *, fc):
    x = x_ref[...]
    h = _rms(x, gffn_ref[...], RMS_EPS).astype(BF16)
    F = wg_ref.shape[1]
    y = x
    for c in range(0, F, fc):
        g = _dot(h, wg_ref[:, c:c + fc])
        u = _dot(h, wu_ref[:, c:c + fc])
        y = y + _dot((g * _sigmoid(g) * u).astype(BF16), wd_ref[c:c + fc, :])
    o_ref[...] = _ple(y, p_ref[...], gple_ref[...], wpg_ref[...], wpp_ref[...])


def _ffn_ple(x2, p2, gffn, wg, wu, wd, gple, wpg, wpp, *, tm=256):
    T = x2.shape[0]
    F = wg.shape[1]
    return pl.pallas_call(
        functools.partial(_ffn_ple_kernel, fc=F // 2),
        out_shape=jax.ShapeDtypeStruct((T, D_MODEL), F32),
        grid=(T // tm,),
        in_specs=[
            pl.BlockSpec((tm, D_MODEL), lambda i: (i, 0)),
            pl.BlockSpec((tm, PLE_DIM), lambda i: (i, 0)),
            _full((1, D_MODEL)),
            _full((D_MODEL, F)),
            _full((D_MODEL, F)),
            _full((F, D_MODEL)),
            _full((1, D_MODEL)),
            _full((D_MODEL, D_MODEL)),
            _full((PLE_DIM, D_MODEL)),
        ],
        out_specs=pl.BlockSpec((tm, D_MODEL), lambda i: (i, 0)),
        compiler_params=pltpu.CompilerParams(dimension_semantics=("arbitrary",), vmem_limit_bytes=VMEM_LIMIT),
        name="ffn_ple",
    )(x2, p2, gffn, wg, wu, wd, gple, wpg, wpp)


def _router_kernel(x_ref, g_ref, whi_ref, wlo_ref, br_ref, h_ref, ri_ref, rw_ref, cnt_ref, run_ref, *, tm):
    i = pl.program_id(0)

    @pl.when(i == 0)
    def _():
        run_ref[...] = jnp.zeros_like(run_ref)

    h = _rms(x_ref[...], g_ref[...], RMS_EPS)
    h_ref[...] = h
    hh = h.astype(BF16)
    hl = (h - hh.astype(F32)).astype(BF16)
    logits = _dot(hh, whi_ref[...]) + _dot(hh, wlo_ref[...]) + _dot(hl, whi_ref[...]) + br_ref[...]
    lane = lax.broadcasted_iota(jnp.int32, (tm, LANES), 1)
    m1 = jnp.max(logits, axis=-1, keepdims=True)
    i1 = jnp.min(jnp.where(logits == m1, lane, LANES), axis=-1, keepdims=True)
    l2 = jnp.where(lane == i1, -jnp.inf, logits)
    m2 = jnp.max(l2, axis=-1, keepdims=True)
    i2 = jnp.min(jnp.where(l2 == m2, lane, LANES), axis=-1, keepdims=True)
    e = jnp.exp(m2 - m1)
    w1 = 1.0 / (1.0 + e)
    w2 = e / (1.0 + e)
    sel1 = lane == i1
    sel2 = lane == i2
    onehot = (sel1 | sel2).astype(BF16)
    row = lax.broadcasted_iota(jnp.int32, (tm, tm), 0)
    col = lax.broadcasted_iota(jnp.int32, (tm, tm), 1)
    before = _dot((col < row).astype(BF16), onehot) + run_ref[...]
    pos1 = jnp.sum(jnp.where(sel1, before, 0.0), axis=-1, keepdims=True).astype(jnp.int32)
    pos2 = jnp.sum(jnp.where(sel2, before, 0.0), axis=-1, keepdims=True).astype(jnp.int32)
    run = run_ref[...] + jnp.sum(onehot.astype(F32), axis=0, keepdims=True)
    run_ref[...] = run
    cnt_ref[...] = run
    ri_ref[...] = jnp.where(lane == 0, i1, jnp.where(lane == 1, i2, jnp.where(lane == 2, pos1, jnp.where(lane == 3, pos2, 0))))
    rw_ref[...] = jnp.where(lane == 0, w1, jnp.where(lane == 1, w2, 0.0))


def _router(x2, g, whi, wlo, br, *, tm=256):
    T = x2.shape[0]
    return pl.pallas_call(
        functools.partial(_router_kernel, tm=tm),
        out_shape=(
            jax.ShapeDtypeStruct((T, D_MODEL), F32),
            jax.ShapeDtypeStruct((T, LANES), jnp.int32),
            jax.ShapeDtypeStruct((T, LANES), F32),
            jax.ShapeDtypeStruct((1, LANES), F32),
        ),
        grid=(T // tm,),
        in_specs=[
            pl.BlockSpec((tm, D_MODEL), lambda i: (i, 0)),
            _full((1, D_MODEL)),
            _full((D_MODEL, LANES)),
            _full((D_MODEL, LANES)),
            _full((1, LANES)),
        ],
        out_specs=(
            pl.BlockSpec((tm, D_MODEL), lambda i: (i, 0)),
            pl.BlockSpec((tm, LANES), lambda i: (i, 0)),
            pl.BlockSpec((tm, LANES), lambda i: (i, 0)),
            _full((1, LANES)),
        ),
        scratch_shapes=[pltpu.VMEM((1, LANES), F32)],
        compiler_params=pltpu.CompilerParams(dimension_semantics=("arbitrary",), vmem_limit_bytes=VMEM_LIMIT),
        name="router",
    )(x2, g, whi, wlo, br)


def _row_copy(src_ref, src_row, dst_ref, dst_row, sem):
    return pltpu.make_async_copy(src_ref.at[pl.ds(src_row, 1)], dst_ref.at[pl.ds(dst_row, 1)], sem)


def _dispatch_kernel(dest_ref, h_ref, xs_in_ref, xs_ref, sem, *, td):
    del xs_in_ref

    def start(i, carry):
        for j in range(DMA_UNROLL):
            r = i * DMA_UNROLL + j
            _row_copy(h_ref, r, xs_ref, dest_ref[2 * r], sem).start()
            _row_copy(h_ref, r, xs_ref, dest_ref[2 * r + 1], sem).start()
        return carry

    lax.fori_loop(0, td // DMA_UNROLL, start, 0)
    for _ in range(2):
        pltpu.make_async_copy(h_ref, xs_ref.at[pl.ds(0, td)], sem).wait()


def _dispatch(dest_flat, h, xs_zero, *, td=512):
    T = h.shape[0]
    return pl.pallas_call(
        functools.partial(_dispatch_kernel, td=td),
        out_shape=jax.ShapeDtypeStruct(xs_zero.shape, xs_zero.dtype),
        grid=(T // td,),
        in_specs=[
            pl.BlockSpec((2 * td,), lambda i: (i,), memory_space=pltpu.SMEM),
            pl.BlockSpec((td, D_MODEL), lambda i: (i, 0)),
            pl.BlockSpec(memory_space=pl.ANY),
        ],
        out_specs=pl.BlockSpec(memory_space=pl.ANY),
        scratch_shapes=[pltpu.SemaphoreType.DMA],
        input_output_aliases={2: 0},
        compiler_params=pltpu.CompilerParams(dimension_semantics=("arbitrary",), vmem_limit_bytes=VMEM_LIMIT),
        name="moe_dispatch",
    )(dest_flat, h, xs_zero)


def _gmm_kernel(be_ref, nu_ref, xs_ref, wg_ref, wu_ref, wd_ref, o_ref):
    i = pl.program_id(0)
    f = pl.program_id(1)

    @pl.when(i < nu_ref[0])
    def _():
        x = xs_ref[...].astype(BF16)
        g = _dot(x, wg_ref[0])
        u = _dot(x, wu_ref[0])
        y = _dot((g * _sigmoid(g) * u).astype(BF16), wd_ref[0])

        @pl.when(f == 0)
        def _():
            o_ref[...] = y

        @pl.when(f > 0)
        def _():
            o_ref[...] += y

    @pl.when((i >= nu_ref[0]) & (f == 0))
    def _():
        o_ref[...] = jnp.zeros_like(o_ref)


def _gmm(block_e, n_used, xs, wg, wu, wd, *, tmm, tf):
    n_rows = xs.shape[0]
    n_blocks = n_rows // tmm
    F = wg.shape[2]
    nf = F // tf

    def row_map(i, f, be, nu):
        return (jnp.minimum(i, nu[0] - 1), 0)

    def f_idx(i, f, nu):
        return jnp.where(i < nu[0], f, nf - 1)

    return pl.pallas_call(
        _gmm_kernel,
        out_shape=jax.ShapeDtypeStruct((n_rows, D_MODEL), F32),
        grid_spec=pltpu.PrefetchScalarGridSpec(
            num_scalar_prefetch=2,
            grid=(n_blocks, nf),
            in_specs=[
                pl.BlockSpec((tmm, D_MODEL), row_map),
                pl.BlockSpec((1, D_MODEL, tf), lambda i, f, be, nu: (be[i], 0, f_idx(i, f, nu))),
                pl.BlockSpec((1, D_MODEL, tf), lambda i, f, be, nu: (be[i], 0, f_idx(i, f, nu))),
                pl.BlockSpec((1, tf, D_MODEL), lambda i, f, be, nu: (be[i], f_idx(i, f, nu), 0)),
            ],
            out_specs=pl.BlockSpec((tmm, D_MODEL), lambda i, f, be, nu: (i, 0)),
        ),
        compiler_params=pltpu.CompilerParams(dimension_semantics=("arbitrary", "arbitrary"), vmem_limit_bytes=VMEM_LIMIT),
        name="moe_grouped_swiglu",
    )(block_e, n_used, xs, wg, wu, wd)


def _combine_ple_kernel(dest_ref, dest_next_ref, x_ref, rw_ref, p_ref, gple_ref, wpg_ref, wpp_ref, gfin_ref,
                        yb_ref, o_ref, buf_ref, sem, *, tp):
    i = pl.program_id(0)
    slot = lax.rem(i, 2)

    def gather(d_ref, sl):
        def body(it, carry):
            for j in range(DMA_UNROLL):
                r = it * DMA_UNROLL + j
                _row_copy(yb_ref, d_ref[2 * r], buf_ref.at[sl, 0], r, sem.at[sl]).start()
                _row_copy(yb_ref, d_ref[2 * r + 1], buf_ref.at[sl, 1], r, sem.at[sl]).start()
            return carry

        lax.fori_loop(0, tp // DMA_UNROLL, body, 0)

    @pl.when(i == 0)
    def _():
        gather(dest_ref, 0)

    @pl.when(i + 1 < pl.num_programs(0))
    def _():
        gather(dest_next_ref, 1 - slot)

    for k in range(2):
        pltpu.make_async_copy(yb_ref.at[pl.ds(0, tp)], buf_ref.at[slot, k], sem.at[slot]).wait()
    rw = rw_ref[...]
    x1 = x_ref[...] + rw[:, 0:1] * buf_ref[slot, 0] + rw[:, 1:2] * buf_ref[slot, 1]
    x2 = _ple(x1, p_ref[...], gple_ref[...], wpg_ref[...], wpp_ref[...])
    o_ref[...] = _rms(x2, gfin_ref[...], RMS_EPS)


def _combine_ple(dest_flat, x2, rw, p2, gple, wpg, wpp, gfin, yb, *, tp=256):
    T = x2.shape[0]
    n = T // tp
    return pl.pallas_call(
        functools.partial(_combine_ple_kernel, tp=tp),
        out_shape=jax.ShapeDtypeStruct((T, D_MODEL), F32),
        grid=(n,),
        in_specs=[
            pl.BlockSpec((2 * tp,), lambda i: (i,), memory_space=pltpu.SMEM),
            pl.BlockSpec((2 * tp,), lambda i: (jnp.minimum(i + 1, n - 1),), memory_space=pltpu.SMEM),
            pl.BlockSpec((tp, D_MODEL), lambda i: (i, 0)),
            pl.BlockSpec((tp, LANES), lambda i: (i, 0)),
            pl.BlockSpec((tp, PLE_DIM), lambda i: (i, 0)),
            _full((1, D_MODEL)),
            _full((D_MODEL, D_MODEL)),
            _full((PLE_DIM, D_MODEL)),
            _full((1, D_MODEL)),
            pl.BlockSpec(memory_space=pl.ANY),
        ],
        out_specs=pl.BlockSpec((tp, D_MODEL), lambda i: (i, 0)),
        scratch_shapes=[pltpu.VMEM((2, 2, tp, D_MODEL), F32), pltpu.SemaphoreType.DMA((2,))],
        compiler_params=pltpu.CompilerParams(dimension_semantics=("arbitrary",), vmem_limit_bytes=VMEM_LIMIT),
        name="moe_combine_ple",
    )(dest_flat, dest_flat, x2, rw, p2, gple, wpg, wpp, gfin, yb)


def _moe_ple_final(x2, p2, gffn, w_router, b_router, wg, wu, wd, gple, wpg, wpp, gfin, *, tmm=512, tf=1792):
    T = x2.shape[0]
    wr = jnp.zeros((D_MODEL, LANES), F32).at[:, :N_EXPERTS].set(w_router)
    whi = wr.astype(BF16)
    wlo = (wr - whi.astype(F32)).astype(BF16)
    br = jnp.full((1, LANES), NEG_BIG, F32).at[0, :N_EXPERTS].set(b_router)
    h, ri, rw, cnt = _router(x2, gffn, whi, wlo, br)
    counts = cnt[0, :N_EXPERTS].astype(jnp.int32)
    padded = (counts + tmm - 1) // tmm * tmm
    pad_end = jnp.cumsum(padded)
    pad_start = pad_end - padded
    dest = pad_start[ri[:, 0:2]] + ri[:, 2:4]
    dest_flat = dest.reshape(2 * T).astype(jnp.int32)
    n_blocks = (2 * T) // tmm + N_EXPERTS
    block_start = jnp.arange(n_blocks, dtype=jnp.int32) * tmm
    block_e = jnp.minimum(jnp.searchsorted(pad_end, block_start, side='right'), N_EXPERTS - 1).astype(jnp.int32)
    n_used = (pad_end[-1] // tmm).astype(jnp.int32).reshape(1)
    xs = _dispatch(dest_flat, h, jnp.zeros((n_blocks * tmm, D_MODEL), F32))
    yb = _gmm(block_e, n_used, xs, wg, wu, wd, tmm=tmm, tf=tf)
    return _combine_ple(dest_flat, x2, rw, p2, gple, wpg, wpp, gfin, yb)


def kernel(x, p, g_mix, w_in, w_gk_up, b_gk, g_gla_norm, conv_w, conv_b, ln_conv_g, ln_conv_b, w_conv_out, w_pool, pool_scale, w_out, g_ffn, w_ffn_gate, w_ffn_up, w_ffn_down, w_router, b_router, w_moe_gate, w_moe_up, w_moe_down, g_ple, w_ple_gate, w_ple_proj, g_final):
    B, S, _ = x.shape
    T = B * S
    depth = w_in.shape[0]
    assert depth == 2 and x.shape[2] == D_MODEL, "dense layer 0, routed layer 1 with the final norm fused"
    x2 = x.reshape(T, D_MODEL)
    row = lambda a: a.reshape(1, -1)
    for i in range(depth):
        wi = w_in[i]
        w_main = jnp.concatenate([wi[:, :3072], wi[:, 3088:]], axis=1).astype(BF16)
        w_gk = jnp.zeros((D_MODEL, GK_PAD), F32).at[:, :GLA_GATE_RANK].set(wi[:, 3072:3088]).astype(BF16)
        w_gk_up_p = jnp.zeros((GK_PAD, GLA_KEY_DIM), F32).at[:GLA_GATE_RANK].set(w_gk_up[i]).astype(BF16)
        z, zgk = _in_proj(x2, row(g_mix[i]), w_main, w_gk)
        x2 = _mixer(z, zgk, x2, w_gk_up_p, row(b_gk[i]), row(g_gla_norm[i]), conv_w[i], row(conv_b[i]),
                    row(ln_conv_g[i]), row(ln_conv_b[i]), w_conv_out[i].astype(BF16), w_pool[i].astype(BF16),
                    row(pool_scale[i]), w_out[i].astype(BF16), B=B, S=S)
        p2 = p[i].reshape(T, PLE_DIM)
        wpg = w_ple_gate[i].astype(BF16)
        wpp = w_ple_proj[i].astype(BF16)
        if i % 2 == 0:
            j = i // 2
            x2 = _ffn_ple(x2, p2, row(g_ffn[i]), w_ffn_gate[j].astype(BF16), w_ffn_up[j].astype(BF16),
                          w_ffn_down[j].astype(BF16), row(g_ple[i]), wpg, wpp)
        else:
            j = i // 2
            x2 = _moe_ple_final(x2, p2, row(g_ffn[i]), w_router[j], b_router[j], w_moe_gate[j].astype(BF16),
                                w_moe_up[j].astype(BF16), w_moe_down[j].astype(BF16), row(g_ple[i]), wpg, wpp,
                                row(g_final))
    return x2.reshape(B, S, D_MODEL)
```

```python
import functools

import jax
import jax.numpy as jnp
from jax import lax
from jax.experimental import pallas as pl
from jax.experimental.pallas import tpu as pltpu

F32 = jnp.float32
BF16 = jnp.bfloat16

D_MODEL = 1024
GLA_HEADS = 4
GLA_KEY_DIM = 512
GLA_HEAD_K = 128
GLA_HEAD_V = 256
GLA_GATE_RANK = 16
GLA_GATE_NORMALIZER = 16.0
GLA_CHUNK = 64
GLA_NORM_EPS = 1e-5
CONV_WIDTH = 31
LN_EPS = 1e-5
POOL_WINDOWS = (2, 4, 8, 16)
POOL_GROUP_CH = 256
N_EXPERTS = 8
PLE_DIM = 256
RMS_EPS = 1e-6

Z_Q, Z_K, Z_V, Z_G, Z_CA, Z_CG, Z_POOL, Z_GATE = 0, 512, 1024, 2048, 3072, 4096, 5120, 6144
Z_DIM = 9216
Z_CHUNK = 1024
PROJ_PIECE = 256
GK_PAD = 128
LANES = 128

CONV_HALO = 32
POOL_HALO = 128
CONV_RC = 32
VMEM_LIMIT = 56 * 1024 * 1024

NEG_BIG = -1e30
DMA_UNROLL = 8


def _dot(a, b):
    return jnp.dot(a, b, preferred_element_type=F32)


def _dot_nt(a, b):
    return lax.dot_general(a, b, (((1,), (1,)), ((), ())), preferred_element_type=F32)


def _dot_tn(a, b):
    return lax.dot_general(a, b, (((0,), (0,)), ((), ())), preferred_element_type=F32)


def _sigmoid(x):
    return 1.0 / (1.0 + jnp.exp(-x))


def _rms(x, g, eps):
    return x * lax.rsqrt(jnp.mean(x * x, axis=-1, keepdims=True) + eps) * g


def _split3(x):
    hi = x.astype(BF16)
    r1 = x - hi.astype(F32)
    mid = r1.astype(BF16)
    lo = (r1 - mid.astype(F32)).astype(BF16)
    return hi, mid, lo


def _full(shape):
    n = len(shape)
    return pl.BlockSpec(shape, lambda *_: (0,) * n)


def _once(shape):
    n = len(shape)
    return pl.BlockSpec(shape, lambda *_: (0,) * n, pipeline_mode=pl.Buffered(1))


def _mixer_kernel(x_ref, xn_ref, gmix_ref, win_ref, wgkin_ref, wgk_ref, bgk_ref, gnorm_ref, convw_ref, convb_ref,
                  lng_ref, lnb_ref, wco_ref, wpool_ref, pscale_ref, wout_ref, o_ref,
                  za_ref, zb_ref, zgka_ref, zgkb_ref, hn_ref, state_ref, ubuf_ref, pbuf_ref, ygla_ref, yconv_ref,
                  *, ts):
    sp = pl.program_id(1)
    refs = (wgk_ref, bgk_ref, gnorm_ref, convw_ref, convb_ref, lng_ref, lnb_ref, wco_ref, wpool_ref, pscale_ref,
            wout_ref, state_ref, ubuf_ref, pbuf_ref, ygla_ref, yconv_ref)
    assert Z_DIM // Z_CHUNK == ts // CONV_RC + 1, "one projection chunk up front, one per conv row chunk"

    def project_chunk(z_ref, c, piece=None):
        cols = slice(0, Z_CHUNK) if piece is None else slice(piece * PROJ_PIECE, (piece + 1) * PROJ_PIECE)
        z_ref[c, :, cols] = _dot(hn_ref[...], win_ref[c, :, cols]).astype(BF16)

    def start_projection(x, z_ref, zgk_ref):
        hn_ref[...] = _rms(x, gmix_ref[...], RMS_EPS).astype(BF16)
        zgk_ref[...] = _dot(hn_ref[...], wgkin_ref[...])
        project_chunk(z_ref, 0)

    @pl.when((pl.program_id(0) == 0) & (sp == 0))
    def _():
        start_projection(x_ref[0:ts, :], za_ref, zgka_ref)
        for c in range(1, Z_DIM // Z_CHUNK):
            project_chunk(za_ref, c)

    @pl.when(sp == 0)
    def _():
        state_ref[...] = jnp.zeros_like(state_ref)
        ubuf_ref[0:CONV_HALO, :] = jnp.zeros((CONV_HALO, D_MODEL), F32)
        pbuf_ref[0:POOL_HALO, :] = jnp.zeros((POOL_HALO, D_MODEL), BF16)

    start_projection(x_ref[ts:2 * ts, :], zb_ref, zgkb_ref)
    o_ref[0:ts, :] = _mix_tile(za_ref, zgka_ref, x_ref[0:ts, :], 2 * sp * ts, functools.partial(project_chunk, zb_ref),
                               *refs, ts=ts)
    start_projection(xn_ref[...], za_ref, zgka_ref)
    o_ref[ts:2 * ts, :] = _mix_tile(zb_ref, zgkb_ref, x_ref[ts:2 * ts, :], (2 * sp + 1) * ts,
                                    functools.partial(project_chunk, za_ref), *refs, ts=ts)


def _mix_tile(z_ref, zgk_ref, x, t0, project_next, wgk_ref, bgk_ref, gnorm_ref, convw_ref, convb_ref, lng_ref, lnb_ref,
              wco_ref, wpool_ref, pscale_ref, wout_ref, state_ref, ubuf_ref, pbuf_ref, ygla_ref, yconv_ref, *, ts):
    C = GLA_CHUNK

    def zcols(a, b):
        c, lo = divmod(a, Z_CHUNK)
        return z_ref[c, :, lo:lo + (b - a)]

    gk_pre = _dot(zgk_ref[...].astype(BF16), wgk_ref[...]) + bgk_ref[...]
    gk = (jnp.minimum(gk_pre, 0.0) - jnp.log(1.0 + jnp.exp(-jnp.abs(gk_pre)))) * (1.0 / GLA_GATE_NORMALIZER)
    row = lax.broadcasted_iota(jnp.int32, (ts, ts), 0)
    col = lax.broadcasted_iota(jnp.int32, (ts, ts), 1)
    same_chunk_causal = (jnp.right_shift(row, 6) == jnp.right_shift(col, 6)) & (col <= row)
    tri = same_chunk_causal.astype(BF16)
    g_hi, g_mid, g_lo = _split3(gk)
    b_all = _dot(tri, g_hi) + _dot(tri, g_mid) + _dot(tri, g_lo)
    nchunk = ts // C

    def per_chunk_row(r):
        return jnp.concatenate(
            [jnp.broadcast_to(b_all[c * C + r:c * C + r + 1, :], (C, GLA_KEY_DIM)) for c in range(nchunk)], axis=0)

    b_mid = per_chunk_row(C // 2 - 1)
    b_last = per_chunk_row(C - 1)
    q = zcols(Z_Q, Z_Q + GLA_KEY_DIM).astype(F32) * (GLA_HEAD_K ** -0.5)
    k = zcols(Z_K, Z_K + GLA_KEY_DIM).astype(F32)
    qa = (q * jnp.exp(b_all - b_mid)).astype(BF16)
    ka = (k * jnp.exp(b_mid - b_all)).astype(BF16)
    qd = (q * jnp.exp(b_all)).astype(BF16)
    kd = (k * jnp.exp(b_last - b_all)).astype(BF16)
    e_dec = [jnp.exp(b_all[c * C + C - 1:c * C + C, :]) for c in range(nchunk)]
    gnorm = gnorm_ref[...]
    for h in range(GLA_HEADS):
        ks = slice(h * GLA_HEAD_K, (h + 1) * GLA_HEAD_K)
        v = zcols(Z_V + h * GLA_HEAD_V, Z_V + (h + 1) * GLA_HEAD_V)
        attn = jnp.where(same_chunk_causal, _dot_nt(qa[:, ks], ka[:, ks]), 0.0).astype(BF16)
        o = _dot(attn, v)
        st = state_ref[h]
        o_inter = []
        for c in range(nchunk):
            rs = slice(c * C, (c + 1) * C)
            o_inter.append(_dot_nt(qd[rs, ks], st.astype(BF16)))
            st = st * e_dec[c][:, ks] + _dot_tn(v[rs, :], kd[rs, ks])
        state_ref[h] = st
        o = o + jnp.concatenate(o_inter, axis=0)
        o = o * lax.rsqrt(jnp.mean(o * o, axis=-1, keepdims=True) + GLA_NORM_EPS) * gnorm
        gi = zcols(Z_G + h * GLA_HEAD_V, Z_G + (h + 1) * GLA_HEAD_V).astype(F32)
        ygla_ref[:, h * GLA_HEAD_V:(h + 1) * GLA_HEAD_V] = o * (gi * _sigmoid(gi))

    ca = zcols(Z_CA, Z_CA + D_MODEL).astype(F32)
    cg = zcols(Z_CG, Z_CG + D_MODEL).astype(F32)
    ubuf_ref[CONV_HALO:CONV_HALO + ts, :] = ca * _sigmoid(cg)
    RC = CONV_RC

    def conv_rows(i, carry):
        base = pl.multiple_of(i * RC, RC)
        for cc in range(D_MODEL // LANES):
            if cc % (PROJ_PIECE // LANES) == 0:
                project_next(i + 1, cc // (PROJ_PIECE // LANES))
            cs = slice(cc * LANES, (cc + 1) * LANES)
            acc = None
            for r in range(8):
                n_rows = RC if r == 0 else RC + 8
                part = None
                for a in range(5):
                    o = 8 * a + r
                    if o < 2 or o > CONV_HALO:
                        continue
                    kk = o - 2
                    term = convw_ref[kk:kk + 1, cs] * ubuf_ref[pl.ds(base + 8 * a, n_rows), cs]
                    part = term if part is None else part + term
                part = part[r:r + RC, :]
                acc = part if acc is None else acc + part
            yconv_ref[pl.ds(base, RC), cs] = acc + convb_ref[:, cs]
        return carry

    lax.fori_loop(0, ts // RC, conv_rows, 0)
    ubuf_ref[0:CONV_HALO, :] = ubuf_ref[ts:ts + CONV_HALO, :]
    yc = yconv_ref[...]
    mu = jnp.mean(yc, axis=-1, keepdims=True)
    ycc = yc - mu
    var = jnp.mean(ycc * ycc, axis=-1, keepdims=True)
    yn = ycc * lax.rsqrt(var + LN_EPS) * lng_ref[...] + lnb_ref[...]
    y_conv = _dot((yn * _sigmoid(yn)).astype(BF16), wco_ref[...])

    pbuf_ref[POOL_HALO:POOL_HALO + ts, :] = zcols(Z_POOL, Z_POOL + D_MODEL)
    tpos = t0 + lax.broadcasted_iota(jnp.int32, (ts, 1), 0)
    prow = lax.broadcasted_iota(jnp.int32, (ts, POOL_HALO + ts), 0) + POOL_HALO
    pcol = lax.broadcasted_iota(jnp.int32, (ts, POOL_HALO + ts), 1)
    y_pool = []
    for gi_, w in enumerate(POOL_WINDOWS):
        cs = slice(gi_ * POOL_GROUP_CH, (gi_ + 1) * POOL_GROUP_CH)
        band = ((pcol <= prow) & (pcol > prow - w)).astype(BF16)
        wsum = _dot(band, pbuf_ref[:, cs])
        cnt = jnp.minimum(tpos + 1, w).astype(F32)
        d = wsum / cnt - zcols(Z_POOL + gi_ * POOL_GROUP_CH, Z_POOL + (gi_ + 1) * POOL_GROUP_CH).astype(F32)
        y_pool.append(_dot(d.astype(BF16), wpool_ref[gi_]) * pscale_ref[:, cs])
    pbuf_ref[0:POOL_HALO, :] = pbuf_ref[ts:ts + POOL_HALO, :]

    mix_parts = []
    for gi_ in range(4):
        cs = slice(gi_ * POOL_GROUP_CH, (gi_ + 1) * POOL_GROUP_CH)
        g0 = _sigmoid(zcols(Z_GATE + gi_ * 256, Z_GATE + (gi_ + 1) * 256))
        g1 = _sigmoid(zcols(Z_GATE + D_MODEL + gi_ * 256, Z_GATE + D_MODEL + (gi_ + 1) * 256))
        g2 = _sigmoid(zcols(Z_GATE + 2 * D_MODEL + gi_ * 256, Z_GATE + 2 * D_MODEL + (gi_ + 1) * 256))
        m = g0 * ygla_ref[:, cs].astype(BF16) + g1 * y_conv[:, cs].astype(BF16) + g2 * y_pool[gi_].astype(BF16)
        mix_parts.append(m)
    acc = x
    for gi_ in range(4):
        acc = acc + _dot(mix_parts[gi_], wout_ref[gi_ * 256:(gi_ + 1) * 256, :])
    return acc


def _mixer(x2, gmix, w_main, w_gk_in, wgk, bgk, gnorm, convw, convb, lng, lnb, wco, wpool, pscale, wout, *, B, S, ts=256):
    T = B * S
    nt = S // (2 * ts)
    n_tiles = T // ts
    nz = Z_DIM // Z_CHUNK
    return pl.pallas_call(
        functools.partial(_mixer_kernel, ts=ts),
        out_shape=jax.ShapeDtypeStruct((T, D_MODEL), F32),
        grid=(B, nt),
        in_specs=[
            pl.BlockSpec((2 * ts, D_MODEL), lambda b, s: (b * nt + s, 0)),
            pl.BlockSpec((ts, D_MODEL), lambda b, s: (jnp.minimum(2 * (b * nt + s) + 2, n_tiles - 1), 0)),
            _once((1, D_MODEL)),
            _once((nz, D_MODEL, Z_CHUNK)),
            _once((D_MODEL, GK_PAD)),
            _once((GK_PAD, GLA_KEY_DIM)),
            _once((1, GLA_KEY_DIM)),
            _once((1, GLA_HEAD_V)),
            _once((CONV_WIDTH, D_MODEL)),
            _once((1, D_MODEL)),
            _once((1, D_MODEL)),
            _once((1, D_MODEL)),
            _once((D_MODEL, D_MODEL)),
            _once((4, POOL_GROUP_CH, POOL_GROUP_CH)),
            _once((1, D_MODEL)),
            _once((D_MODEL, D_MODEL)),
        ],
        out_specs=pl.BlockSpec((2 * ts, D_MODEL), lambda b, s: (b * nt + s, 0)),
        scratch_shapes=[
            pltpu.VMEM((nz, ts, Z_CHUNK), BF16),
            pltpu.VMEM((nz, ts, Z_CHUNK), BF16),
            pltpu.VMEM((ts, GK_PAD), F32),
            pltpu.VMEM((ts, GK_PAD), F32),
            pltpu.VMEM((ts, D_MODEL), BF16),
            pltpu.VMEM((GLA_HEADS, GLA_HEAD_V, GLA_HEAD_K), F32),
            pltpu.VMEM((CONV_HALO + ts, D_MODEL), F32),
            pltpu.VMEM((POOL_HALO + ts, D_MODEL), BF16),
            pltpu.VMEM((ts, D_MODEL), F32),
            pltpu.VMEM((ts, D_MODEL), F32),
        ],
        compiler_params=pltpu.CompilerParams(dimension_semantics=("arbitrary", "arbitrary"), vmem_limit_bytes=VMEM_LIMIT),
        name="token_mixer",
    )(x2, x2, gmix, w_main, w_gk_in, wgk, bgk, gnorm, convw, convb, lng, lnb, wco, wpool, pscale, wout)


def _ple(x1, p, gple, wpg, wpp):
    hp = _rms(x1, gple, RMS_EPS).astype(BF16)
    return x1 + _sigmoid(_dot(hp, wpg)) * _dot(p.astype(BF16), wpp)


def _ffn_ple_kernel(x_ref, p_ref, gffn_ref, wg_ref, wu_ref, wd_ref, gple_ref, wpg_ref, wpp_ref, o_ref, *, fc):
    x = x_ref[...]
    h = _rms(x, gffn_ref[...], RMS_EPS).astype(BF16)
    F = wg_ref.shape[1]
    y = x
    for c in range(0, F, fc):
        g = _dot(h, wg_ref[:, c:c + fc])
        u = _dot(h, wu_ref[:, c:c + fc])
        y = y + _dot((g * _sigmoid(g) * u).astype(BF16), wd_ref[c:c + fc, :])
    o_ref[...] = _ple(y, p_ref[...], gple_ref[...], wpg_ref[...], wpp_ref[...])


def _ffn_ple(x2, p2, gffn, wg, wu, wd, gple, wpg, wpp, *, tm=256):
    T = x2.shape[0]
    F = wg.shape[1]
    return pl.pallas_call(
        functools.partial(_ffn_ple_kernel, fc=F // 2),
        out_shape=jax.ShapeDtypeStruct((T, D_MODEL), F32),
        grid=(T // tm,),
        in_specs=[
            pl.BlockSpec((tm, D_MODEL), lambda i: (i, 0)),
            pl.BlockSpec((tm, PLE_DIM), lambda i: (i, 0)),
            _full((1, D_MODEL)),
            _full((D_MODEL, F)),
            _full((D_MODEL, F)),
            _full((F, D_MODEL)),
            _full((1, D_MODEL)),
            _full((D_MODEL, D_MODEL)),
            _full((PLE_DIM, D_MODEL)),
        ],
        out_specs=pl.BlockSpec((tm, D_MODEL), lambda i: (i, 0)),
        compiler_params=pltpu.CompilerParams(dimension_semantics=("arbitrary",), vmem_limit_bytes=VMEM_LIMIT),
        name="ffn_ple",
    )(x2, p2, gffn, wg, wu, wd, gple, wpg, wpp)


def _router_kernel(x_ref, g_ref, whi_ref, wlo_ref, br_ref, h_ref, ri_ref, rw_ref, cnt_ref, run_ref, *, tm):
    i = pl.program_id(0)

    @pl.when(i == 0)
    def _():
        run_ref[...] = jnp.zeros_like(run_ref)

    h = _rms(x_ref[...], g_ref[...], RMS_EPS)
    h_ref[...] = h
    hh = h.astype(BF16)
    hl = (h - hh.astype(F32)).astype(BF16)
    logits = _dot(hh, whi_ref[...]) + _dot(hh, wlo_ref[...]) + _dot(hl, whi_ref[...]) + br_ref[...]
    lane = lax.broadcasted_iota(jnp.int32, (tm, LANES), 1)
    m1 = jnp.max(logits, axis=-1, keepdims=True)
    i1 = jnp.min(jnp.where(logits == m1, lane, LANES), axis=-1, keepdims=True)
    l2 = jnp.where(lane == i1, -jnp.inf, logits)
    m2 = jnp.max(l2, axis=-1, keepdims=True)
    i2 = jnp.min(jnp.where(l2 == m2, lane, LANES), axis=-1, keepdims=True)
    e = jnp.exp(m2 - m1)
    w1 = 1.0 / (1.0 + e)
    w2 = e / (1.0 + e)
    sel1 = lane == i1
    sel2 = lane == i2
    onehot = (sel1 | sel2).astype(BF16)
    row = lax.broadcasted_iota(jnp.int32, (tm, tm), 0)
    col = lax.broadcasted_iota(jnp.int32, (tm, tm), 1)
    before = _dot((col < row).astype(BF16), onehot) + run_ref[...]
    pos1 = jnp.sum(jnp.where(sel1, before, 0.0), axis=-1, keepdims=True).astype(jnp.int32)
    pos2 = jnp.sum(jnp.where(sel2, before, 0.0), axis=-1, keepdims=True).astype(jnp.int32)
    run = run_ref[...] + jnp.sum(onehot.astype(F32), axis=0, keepdims=True)
    run_ref[...] = run
    cnt_ref[...] = run
    ri_ref[...] = jnp.where(lane == 0, i1, jnp.where(lane == 1, i2, jnp.where(lane == 2, pos1, jnp.where(lane == 3, pos2, 0))))
    rw_ref[...] = jnp.where(lane == 0, w1, jnp.where(lane == 1, w2, 0.0))


def _router(x2, g, whi, wlo, br, *, tm=256):
    T = x2.shape[0]
    return pl.pallas_call(
        functools.partial(_router_kernel, tm=tm),
        out_shape=(
            jax.ShapeDtypeStruct((T, D_MODEL), F32),
            jax.ShapeDtypeStruct((T, LANES), jnp.int32),
            jax.ShapeDtypeStruct((T, LANES), F32),
            jax.ShapeDtypeStruct((1, LANES), F32),
        ),
        grid=(T // tm,),
        in_specs=[
            pl.BlockSpec((tm, D_MODEL), lambda i: (i, 0)),
            _full((1, D_MODEL)),
            _full((D_MODEL, LANES)),
            _full((D_MODEL, LANES)),
            _full((1, LANES)),
        ],
        out_specs=(
            pl.BlockSpec((tm, D_MODEL), lambda i: (i, 0)),
            pl.BlockSpec((tm, LANES), lambda i: (i, 0)),
            pl.BlockSpec((tm, LANES), lambda i: (i, 0)),
            _full((1, LANES)),
        ),
        scratch_shapes=[pltpu.VMEM((1, LANES), F32)],
        compiler_params=pltpu.CompilerParams(dimension_semantics=("arbitrary",), vmem_limit_bytes=VMEM_LIMIT),
        name="router",
    )(x2, g, whi, wlo, br)


def _row_copy(src_ref, src_row, dst_ref, dst_row, sem):
    return pltpu.make_async_copy(src_ref.at[pl.ds(src_row, 1)], dst_ref.at[pl.ds(dst_row, 1)], sem)


def _dispatch_kernel(dest_ref, h_ref, xs_in_ref, xs_ref, sem, *, td):
    del xs_in_ref

    def start(i, carry):
        for j in range(DMA_UNROLL):
            r = i * DMA_UNROLL + j
            _row_copy(h_ref, r, xs_ref, dest_ref[2 * r], sem).start()
            _row_copy(h_ref, r, xs_ref, dest_ref[2 * r + 1], sem).start()
        return carry

    lax.fori_loop(0, td // DMA_UNROLL, start, 0)
    for _ in range(2):
        pltpu.make_async_copy(h_ref, xs_ref.at[pl.ds(0, td)], sem).wait()


def _dispatch(dest_flat, h, xs_zero, *, td=512):
    T = h.shape[0]
    return pl.pallas_call(
        functools.partial(_dispatch_kernel, td=td),
        out_shape=jax.ShapeDtypeStruct(xs_zero.shape, xs_zero.dtype),
        grid=(T // td,),
        in_specs=[
            pl.BlockSpec((2 * td,), lambda i: (i,), memory_space=pltpu.SMEM),
            pl.BlockSpec((td, D_MODEL), lambda i: (i, 0)),
            pl.BlockSpec(memory_space=pl.ANY),
        ],
        out_specs=pl.BlockSpec(memory_space=pl.ANY),
        scratch_shapes=[pltpu.SemaphoreType.DMA],
        input_output_aliases={2: 0},
        compiler_params=pltpu.CompilerParams(dimension_semantics=("arbitrary",), vmem_limit_bytes=VMEM_LIMIT),
        name="moe_dispatch",
    )(dest_flat, h, xs_zero)


def _gmm_kernel(be_ref, nu_ref, xs_ref, wg_ref, wu_ref, wd_ref, o_ref):
    i = pl.program_id(0)
    f = pl.program_id(1)

    @pl.when(i < nu_ref[0])
    def _():
        x = xs_ref[...].astype(BF16)
        g = _dot(x, wg_ref[0])
        u = _dot(x, wu_ref[0])
        y = _dot((g * _sigmoid(g) * u).astype(BF16), wd_ref[0])

        @pl.when(f == 0)
        def _():
            o_ref[...] = y

        @pl.when(f > 0)
        def _():
            o_ref[...] += y

    @pl.when((i >= nu_ref[0]) & (f == 0))
    def _():
        o_ref[...] = jnp.zeros_like(o_ref)


def _gmm(block_e, n_used, xs, wg, wu, wd, *, tmm, tf):
    n_rows = xs.shape[0]
    n_blocks = n_rows // tmm
    F = wg.shape[2]
    nf = F // tf

    def row_map(i, f, be, nu):
        return (jnp.minimum(i, nu[0] - 1), 0)

    def f_idx(i, f, nu):
        return jnp.where(i < nu[0], f, nf - 1)

    return pl.pallas_call(
        _gmm_kernel,
        out_shape=jax.ShapeDtypeStruct((n_rows, D_MODEL), F32),
        grid_spec=pltpu.PrefetchScalarGridSpec(
            num_scalar_prefetch=2,
            grid=(n_blocks, nf),
            in_specs=[
                pl.BlockSpec((tmm, D_MODEL), row_map),
                pl.BlockSpec((1, D_MODEL, tf), lambda i, f, be, nu: (be[i], 0, f_idx(i, f, nu))),
                pl.BlockSpec((1, D_MODEL, tf), lambda i, f, be, nu: (be[i], 0, f_idx(i, f, nu))),
                pl.BlockSpec((1, tf, D_MODEL), lambda i, f, be, nu: (be[i], f_idx(i, f, nu), 0)),
            ],
            out_specs=pl.BlockSpec((tmm, D_MODEL), lambda i, f, be, nu: (i, 0)),
        ),
        compiler_params=pltpu.CompilerParams(dimension_semantics=("arbitrary", "arbitrary"), vmem_limit_bytes=VMEM_LIMIT),
        name="moe_grouped_swiglu",
    )(block_e, n_used, xs, wg, wu, wd)


def _combine_ple_kernel(dest_ref, dest_next_ref, x_ref, rw_ref, p_ref, gple_ref, wpg_ref, wpp_ref, gfin_ref,
                        yb_ref, o_ref, buf_ref, sem, *, tp):
    i = pl.program_id(0)
    slot = lax.rem(i, 2)

    def gather(d_ref, sl):
        def body(it, carry):
            for j in range(DMA_UNROLL):
                r = it * DMA_UNROLL + j
                _row_copy(yb_ref, d_ref[2 * r], buf_ref.at[sl, 0], r, sem.at[sl]).start()
                _row_copy(yb_ref, d_ref[2 * r + 1], buf_ref.at[sl, 1], r, sem.at[sl]).start()
            return carry

        lax.fori_loop(0, tp // DMA_UNROLL, body, 0)

    @pl.when(i == 0)
    def _():
        gather(dest_ref, 0)

    @pl.when(i + 1 < pl.num_programs(0))
    def _():
        gather(dest_next_ref, 1 - slot)

    for k in range(2):
        pltpu.make_async_copy(yb_ref.at[pl.ds(0, tp)], buf_ref.at[slot, k], sem.at[slot]).wait()
    rw = rw_ref[...]
    x1 = x_ref[...] + rw[:, 0:1] * buf_ref[slot, 0] + rw[:, 1:2] * buf_ref[slot, 1]
    x2 = _ple(x1, p_ref[...], gple_ref[...], wpg_ref[...], wpp_ref[...])
    o_ref[...] = _rms(x2, gfin_ref[...], RMS_EPS)


def _combine_ple(dest_flat, x2, rw, p2, gple, wpg, wpp, gfin, yb, *, tp=256):
    T = x2.shape[0]
    n = T // tp
    return pl.pallas_call(
        functools.partial(_combine_ple_kernel, tp=tp),
        out_shape=jax.ShapeDtypeStruct((T, D_MODEL), F32),
        grid=(n,),
        in_specs=[
            pl.BlockSpec((2 * tp,), lambda i: (i,), memory_space=pltpu.SMEM),
            pl.BlockSpec((2 * tp,), lambda i: (jnp.minimum(i + 1, n - 1),), memory_space=pltpu.SMEM),
            pl.BlockSpec((tp, D_MODEL), lambda i: (i, 0)),
            pl.BlockSpec((tp, LANES), lambda i: (i, 0)),
            pl.BlockSpec((tp, PLE_DIM), lambda i: (i, 0)),
            _full((1, D_MODEL)),
            _full((D_MODEL, D_MODEL)),
            _full((PLE_DIM, D_MODEL)),
            _full((1, D_MODEL)),
            pl.BlockSpec(memory_space=pl.ANY),
        ],
        out_specs=pl.BlockSpec((tp, D_MODEL), lambda i: (i, 0)),
        scratch_shapes=[pltpu.VMEM((2, 2, tp, D_MODEL), F32), pltpu.SemaphoreType.DMA((2,))],
        compiler_params=pltpu.CompilerParams(dimension_semantics=("arbitrary",), vmem_limit_bytes=VMEM_LIMIT),
        name="moe_combine_ple",
    )(dest_flat, dest_flat, x2, rw, p2, gple, wpg, wpp, gfin, yb)


def _moe_ple_final(x2, p2, gffn, w_router, b_router, wg, wu, wd, gple, wpg, wpp, gfin, *, tmm=512, tf=1792):
    T = x2.shape[0]
    wr = jnp.zeros((D_MODEL, LANES), F32).at[:, :N_EXPERTS].set(w_router)
    whi = wr.astype(BF16)
    wlo = (wr - whi.astype(F32)).astype(BF16)
    br = jnp.full((1, LANES), NEG_BIG, F32).at[0, :N_EXPERTS].set(b_router)
    h, ri, rw, cnt = _router(x2, gffn, whi, wlo, br)
    counts = cnt[0, :N_EXPERTS].astype(jnp.int32)
    padded = (counts + tmm - 1) // tmm * tmm
    pad_end = jnp.cumsum(padded)
    pad_start = pad_end - padded
    e_sel = ri[:, 0:2, None] == jnp.arange(N_EXPERTS, dtype=jnp.int32)
    dest = jnp.sum(jnp.where(e_sel, pad_start, 0), axis=-1) + ri[:, 2:4]
    dest_flat = dest.reshape(2 * T).astype(jnp.int32)
    n_blocks = (2 * T) // tmm + N_EXPERTS
    block_start = jnp.arange(n_blocks, dtype=jnp.int32) * tmm
    block_e = jnp.minimum(jnp.sum(block_start[:, None] >= pad_end[None, :], axis=1), N_EXPERTS - 1).astype(jnp.int32)
    n_used = (pad_end[-1] // tmm).astype(jnp.int32).reshape(1)
    xs = _dispatch(dest_flat, h, jnp.zeros((n_blocks * tmm, D_MODEL), F32))
    yb = _gmm(block_e, n_used, xs, wg, wu, wd, tmm=tmm, tf=tf)
    return _combine_ple(dest_flat, x2, rw, p2, gple, wpg, wpp, gfin, yb)


def kernel(x, p, g_mix, w_in, w_gk_up, b_gk, g_gla_norm, conv_w, conv_b, ln_conv_g, ln_conv_b, w_conv_out, w_pool, pool_scale, w_out, g_ffn, w_ffn_gate, w_ffn_up, w_ffn_down, w_router, b_router, w_moe_gate, w_moe_up, w_moe_down, g_ple, w_ple_gate, w_ple_proj, g_final):
    B, S, _ = x.shape
    T = B * S
    depth = w_in.shape[0]
    assert depth == 2 and x.shape[2] == D_MODEL, "dense layer 0, routed layer 1 with the final norm fused"
    x2 = x.reshape(T, D_MODEL)
    row = lambda a: a.reshape(1, -1)
    for i in range(depth):
        wi = w_in[i]
        w_main = jnp.concatenate([wi[:, :3072], wi[:, 3088:]], axis=1).astype(BF16)
        w_main = w_main.reshape(D_MODEL, Z_DIM // Z_CHUNK, Z_CHUNK).transpose(1, 0, 2)
        w_gk = jnp.zeros((D_MODEL, GK_PAD), F32).at[:, :GLA_GATE_RANK].set(wi[:, 3072:3088]).astype(BF16)
        w_gk_up_p = jnp.zeros((GK_PAD, GLA_KEY_DIM), F32).at[:GLA_GATE_RANK].set(w_gk_up[i]).astype(BF16)
        x2 = _mixer(x2, row(g_mix[i]), w_main, w_gk, w_gk_up_p, row(b_gk[i]), row(g_gla_norm[i]), conv_w[i],
                    row(conv_b[i]), row(ln_conv_g[i]), row(ln_conv_b[i]), w_conv_out[i].astype(BF16),
                    w_pool[i].astype(BF16), row(pool_scale[i]), w_out[i].astype(BF16), B=B, S=S)
        p2 = p[i].reshape(T, PLE_DIM)
        wpg = w_ple_gate[i].astype(BF16)
        wpp = w_ple_proj[i].astype(BF16)
        if i % 2 == 0:
            j = i // 2
            x2 = _ffn_ple(x2, p2, row(g_ffn[i]), w_ffn_gate[j].astype(BF16), w_ffn_up[j].astype(BF16),
                          w_ffn_down[j].astype(BF16), row(g_ple[i]), wpg, wpp)
        else:
            j = i // 2
            x2 = _moe_ple_final(x2, p2, row(g_ffn[i]), w_router[j], b_router[j], w_moe_gate[j].astype(BF16),
                                w_moe_up[j].astype(BF16), w_moe_down[j].astype(BF16), row(g_ple[i]), wpg, wpp,
                                row(g_final))
    return x2.reshape(B, S, D_MODEL)
```

```python
import functools

import jax
import jax.numpy as jnp
from jax import lax
from jax.experimental import pallas as pl
from jax.experimental.pallas import tpu as pltpu

F32 = jnp.float32
BF16 = jnp.bfloat16

D_MODEL = 1024
GLA_HEADS = 4
GLA_KEY_DIM = 512
GLA_HEAD_K = 128
GLA_HEAD_V = 256
GLA_GATE_RANK = 16
GLA_GATE_NORMALIZER = 16.0
GLA_CHUNK = 64
GLA_NORM_EPS = 1e-5
CONV_WIDTH = 31
LN_EPS = 1e-5
POOL_WINDOWS = (2, 4, 8, 16)
POOL_GROUP_CH = 256
N_EXPERTS = 8
PLE_DIM = 256
RMS_EPS = 1e-6

Z_Q, Z_K, Z_V, Z_G, Z_CA, Z_CG, Z_POOL, Z_GATE = 0, 512, 1024, 2048, 3072, 4096, 5120, 6144
Z_DIM = 9216
GK_PAD = 128
LANES = 128

CONV_HALO = 32
POOL_HALO = 128
CONV_RC = 32
VMEM_LIMIT = 56 * 1024 * 1024

NEG_BIG = -1e30


def _dot(a, b):
    return jnp.dot(a, b, preferred_element_type=F32)


def _dot_nt(a, b):
    return lax.dot_general(a, b, (((1,), (1,)), ((), ())), preferred_element_type=F32)


def _dot_tn(a, b):
    return lax.dot_general(a, b, (((0,), (0,)), ((), ())), preferred_element_type=F32)


def _sigmoid(x):
    return 1.0 / (1.0 + jnp.exp(-x))


def _rms(x, g, eps):
    return x * lax.rsqrt(jnp.mean(x * x, axis=-1, keepdims=True) + eps) * g


def _split3(x):
    hi = x.astype(BF16)
    r1 = x - hi.astype(F32)
    mid = r1.astype(BF16)
    lo = (r1 - mid.astype(F32)).astype(BF16)
    return hi, mid, lo


def _once(shape):
    n = len(shape)
    return pl.BlockSpec(shape, lambda *_: (0,) * n, pipeline_mode=pl.Buffered(1))


def _params(*semantics):
    return pltpu.CompilerParams(dimension_semantics=semantics, vmem_limit_bytes=VMEM_LIMIT)


def _in_proj_kernel(x_ref, g_ref, w_ref, wgk_ref, z_ref, zgk_ref, *, nc):
    h = _rms(x_ref[...], g_ref[...], RMS_EPS).astype(BF16)
    for c in range(0, Z_DIM, nc):
        z_ref[:, c:c + nc] = _dot(h, w_ref[:, c:c + nc]).astype(BF16)
    zgk_ref[...] = _dot(h, wgk_ref[...])


def _in_proj(x2, g, w_main, w_gk, *, tm=512, nc=1024):
    T = x2.shape[0]
    return pl.pallas_call(
        functools.partial(_in_proj_kernel, nc=nc),
        out_shape=(jax.ShapeDtypeStruct((T, Z_DIM), BF16), jax.ShapeDtypeStruct((T, GK_PAD), F32)),
        grid=(T // tm,),
        in_specs=[
            pl.BlockSpec((tm, D_MODEL), lambda i: (i, 0)),
            _once((1, D_MODEL)),
            _once((D_MODEL, Z_DIM)),
            _once((D_MODEL, GK_PAD)),
        ],
        out_specs=(pl.BlockSpec((tm, Z_DIM), lambda i: (i, 0)), pl.BlockSpec((tm, GK_PAD), lambda i: (i, 0))),
        compiler_params=_params("arbitrary"),
        name="in_proj",
    )(x2, g, w_main, w_gk)


def _mixer_kernel(z_ref, zgk_ref, x_ref, wgk_ref, bgk_ref, gnorm_ref, convw_ref, convb_ref, lng_ref, lnb_ref,
                  wco_ref, wpool_ref, pscale_ref, wout_ref, o_ref,
                  state_ref, ubuf_ref, pbuf_ref, ygla_ref, yconv_ref, *, ts):
    s = pl.program_id(1)
    C = GLA_CHUNK

    @pl.when(s == 0)
    def _():
        state_ref[...] = jnp.zeros_like(state_ref)
        ubuf_ref[0:CONV_HALO, :] = jnp.zeros((CONV_HALO, D_MODEL), F32)
        pbuf_ref[0:POOL_HALO, :] = jnp.zeros((POOL_HALO, D_MODEL), BF16)

    gk_pre = _dot(zgk_ref[...].astype(BF16), wgk_ref[...]) + bgk_ref[...]
    gk = (jnp.minimum(gk_pre, 0.0) - jnp.log(1.0 + jnp.exp(-jnp.abs(gk_pre)))) * (1.0 / GLA_GATE_NORMALIZER)
    row = lax.broadcasted_iota(jnp.int32, (ts, ts), 0)
    col = lax.broadcasted_iota(jnp.int32, (ts, ts), 1)
    same_chunk_causal = (jnp.right_shift(row, 6) == jnp.right_shift(col, 6)) & (col <= row)
    tri = same_chunk_causal.astype(BF16)
    g_hi, g_mid, g_lo = _split3(gk)
    b_all = _dot(tri, g_hi) + _dot(tri, g_mid) + _dot(tri, g_lo)
    nchunk = ts // C

    def per_chunk_row(r):
        return jnp.concatenate(
            [jnp.broadcast_to(b_all[c * C + r:c * C + r + 1, :], (C, GLA_KEY_DIM)) for c in range(nchunk)], axis=0)

    b_mid = per_chunk_row(C // 2 - 1)
    b_last = per_chunk_row(C - 1)
    q = z_ref[:, Z_Q:Z_Q + GLA_KEY_DIM].astype(F32) * (GLA_HEAD_K ** -0.5)
    k = z_ref[:, Z_K:Z_K + GLA_KEY_DIM].astype(F32)
    qa = (q * jnp.exp(b_all - b_mid)).astype(BF16)
    ka = (k * jnp.exp(b_mid - b_all)).astype(BF16)
    qd = (q * jnp.exp(b_all)).astype(BF16)
    kd = (k * jnp.exp(b_last - b_all)).astype(BF16)
    e_dec = [jnp.exp(b_all[c * C + C - 1:c * C + C, :]) for c in range(nchunk)]
    gnorm = gnorm_ref[...]
    for h in range(GLA_HEADS):
        ks = slice(h * GLA_HEAD_K, (h + 1) * GLA_HEAD_K)
        v = z_ref[:, Z_V + h * GLA_HEAD_V:Z_V + (h + 1) * GLA_HEAD_V]
        attn = jnp.where(same_chunk_causal, _dot_nt(qa[:, ks], ka[:, ks]), 0.0).astype(BF16)
        o = _dot(attn, v)
        st = state_ref[h]
        o_inter = []
        for c in range(nchunk):
            rs = slice(c * C, (c + 1) * C)
            o_inter.append(_dot_nt(qd[rs, ks], st.astype(BF16)))
            st = st * e_dec[c][:, ks] + _dot_tn(v[rs, :], kd[rs, ks])
        state_ref[h] = st
        o = o + jnp.concatenate(o_inter, axis=0)
        o = o * lax.rsqrt(jnp.mean(o * o, axis=-1, keepdims=True) + GLA_NORM_EPS) * gnorm
        gi = z_ref[:, Z_G + h * GLA_HEAD_V:Z_G + (h + 1) * GLA_HEAD_V].astype(F32)
        ygla_ref[:, h * GLA_HEAD_V:(h + 1) * GLA_HEAD_V] = o * (gi * _sigmoid(gi))

    ca = z_ref[:, Z_CA:Z_CA + D_MODEL].astype(F32)
    cg = z_ref[:, Z_CG:Z_CG + D_MODEL].astype(F32)
    ubuf_ref[CONV_HALO:CONV_HALO + ts, :] = ca * _sigmoid(cg)
    RC = CONV_RC

    def conv_rows(i, carry):
        base = pl.multiple_of(i * RC, RC)
        for cc in range(D_MODEL // LANES):
            cs = slice(cc * LANES, (cc + 1) * LANES)
            acc = None
            for r in range(8):
                n_rows = RC if r == 0 else RC + 8
                part = None
                for a in range(5):
                    o = 8 * a + r
                    if o < 2 or o > CONV_HALO:
                        continue
                    kk = o - 2
                    term = convw_ref[kk:kk + 1, cs] * ubuf_ref[pl.ds(base + 8 * a, n_rows), cs]
                    part = term if part is None else part + term
                part = part[r:r + RC, :]
                acc = part if acc is None else acc + part
            yconv_ref[pl.ds(base, RC), cs] = acc + convb_ref[:, cs]
        return carry

    lax.fori_loop(0, ts // RC, conv_rows, 0)
    ubuf_ref[0:CONV_HALO, :] = ubuf_ref[ts:ts + CONV_HALO, :]
    yc = yconv_ref[...]
    mu = jnp.mean(yc, axis=-1, keepdims=True)
    ycc = yc - mu
    var = jnp.mean(ycc * ycc, axis=-1, keepdims=True)
    yn = ycc * lax.rsqrt(var + LN_EPS) * lng_ref[...] + lnb_ref[...]
    y_conv = _dot((yn * _sigmoid(yn)).astype(BF16), wco_ref[...])

    pbuf_ref[POOL_HALO:POOL_HALO + ts, :] = z_ref[:, Z_POOL:Z_POOL + D_MODEL]
    tpos = s * ts + lax.broadcasted_iota(jnp.int32, (ts, 1), 0)
    prow = lax.broadcasted_iota(jnp.int32, (ts, POOL_HALO + ts), 0) + POOL_HALO
    pcol = lax.broadcasted_iota(jnp.int32, (ts, POOL_HALO + ts), 1)
    y_pool = []
    for gi_, w in enumerate(POOL_WINDOWS):
        cs = slice(gi_ * POOL_GROUP_CH, (gi_ + 1) * POOL_GROUP_CH)
        band = ((pcol <= prow) & (pcol > prow - w)).astype(BF16)
        wsum = _dot(band, pbuf_ref[:, cs])
        cnt = jnp.minimum(tpos + 1, w).astype(F32)
        d = wsum / cnt - z_ref[:, Z_POOL + gi_ * POOL_GROUP_CH:Z_POOL + (gi_ + 1) * POOL_GROUP_CH].astype(F32)
        y_pool.append(_dot(d.astype(BF16), wpool_ref[gi_]) * pscale_ref[:, cs])
    pbuf_ref[0:POOL_HALO, :] = pbuf_ref[ts:ts + POOL_HALO, :]

    mix_parts = []
    for gi_ in range(4):
        cs = slice(gi_ * POOL_GROUP_CH, (gi_ + 1) * POOL_GROUP_CH)
        g0 = _sigmoid(z_ref[:, Z_GATE + gi_ * 256:Z_GATE + (gi_ + 1) * 256])
        g1 = _sigmoid(z_ref[:, Z_GATE + D_MODEL + gi_ * 256:Z_GATE + D_MODEL + (gi_ + 1) * 256])
        g2 = _sigmoid(z_ref[:, Z_GATE + 2 * D_MODEL + gi_ * 256:Z_GATE + 2 * D_MODEL + (gi_ + 1) * 256])
        m = g0 * ygla_ref[:, cs].astype(BF16) + g1 * y_conv[:, cs].astype(BF16) + g2 * y_pool[gi_].astype(BF16)
        mix_parts.append(m)
    acc = x_ref[...]
    for gi_ in range(4):
        acc = acc + _dot(mix_parts[gi_], wout_ref[gi_ * 256:(gi_ + 1) * 256, :])
    o_ref[...] = acc


def _mixer(z, zgk, x2, wgk, bgk, gnorm, convw, convb, lng, lnb, wco, wpool, pscale, wout, *, B, S, ts=256):
    T = B * S
    nt = S // ts
    rowmap = lambda b, s: (b * nt + s, 0)
    return pl.pallas_call(
        functools.partial(_mixer_kernel, ts=ts),
        out_shape=jax.ShapeDtypeStruct((T, D_MODEL), F32),
        grid=(B, nt),
        in_specs=[
            pl.BlockSpec((ts, Z_DIM), rowmap),
            pl.BlockSpec((ts, GK_PAD), rowmap),
            pl.BlockSpec((ts, D_MODEL), rowmap),
            _once((GK_PAD, GLA_KEY_DIM)),
            _once((1, GLA_KEY_DIM)),
            _once((1, GLA_HEAD_V)),
            _once((CONV_WIDTH, D_MODEL)),
            _once((1, D_MODEL)),
            _once((1, D_MODEL)),
            _once((1, D_MODEL)),
            _once((D_MODEL, D_MODEL)),
            _once((4, POOL_GROUP_CH, POOL_GROUP_CH)),
            _once((1, D_MODEL)),
            _once((D_MODEL, D_MODEL)),
        ],
        out_specs=pl.BlockSpec((ts, D_MODEL), rowmap),
        scratch_shapes=[
            pltpu.VMEM((GLA_HEADS, GLA_HEAD_V, GLA_HEAD_K), F32),
            pltpu.VMEM((CONV_HALO + ts, D_MODEL), F32),
            pltpu.VMEM((POOL_HALO + ts, D_MODEL), BF16),
            pltpu.VMEM((ts, D_MODEL), F32),
            pltpu.VMEM((ts, D_MODEL), F32),
        ],
        compiler_params=_params("arbitrary", "arbitrary"),
        name="token_mixer",
    )(z, zgk, x2, wgk, bgk, gnorm, convw, convb, lng, lnb, wco, wpool, pscale, wout)


def _ple(x1, p, gple, wpg, wpp):
    hp = _rms(x1, gple, RMS_EPS).astype(BF16)
    return x1 + _sigmoid(_dot(hp, wpg)) * _dot(p.astype(BF16), wpp)


def _ffn_ple_kernel(x_ref, p_ref, gffn_ref, wg_ref, wu_ref, wd_ref, gple_ref, wpg_ref, wpp_ref, o_ref, *, fc):
    x = x_ref[...]
    h = _rms(x, gffn_ref[...], RMS_EPS).astype(BF16)
    F = wg_ref.shape[1]
    y = x
    for c in range(0, F, fc):
        g = _dot(h, wg_ref[:, c:c + fc])
        u = _dot(h, wu_ref[:, c:c + fc])
        y = y + _dot((g * _sigmoid(g) * u).astype(BF16), wd_ref[c:c + fc, :])
    o_ref[...] = _ple(y, p_ref[...], gple_ref[...], wpg_ref[...], wpp_ref[...])


def _ffn_ple(x2, p2, gffn, wg, wu, wd, gple, wpg, wpp, *, tm=512):
    T = x2.shape[0]
    F = wg.shape[1]
    return pl.pallas_call(
        functools.partial(_ffn_ple_kernel, fc=F // 2),
        out_shape=jax.ShapeDtypeStruct((T, D_MODEL), F32),
        grid=(T // tm,),
        in_specs=[
            pl.BlockSpec((tm, D_MODEL), lambda i: (i, 0)),
            pl.BlockSpec((tm, PLE_DIM), lambda i: (i, 0)),
            _once((1, D_MODEL)),
            _once((D_MODEL, F)),
            _once((D_MODEL, F)),
            _once((F, D_MODEL)),
            _once((1, D_MODEL)),
            _once((D_MODEL, D_MODEL)),
            _once((PLE_DIM, D_MODEL)),
        ],
        out_specs=pl.BlockSpec((tm, D_MODEL), lambda i: (i, 0)),
        compiler_params=_params("arbitrary"),
        name="ffn_ple",
    )(x2, p2, gffn, wg, wu, wd, gple, wpg, wpp)


def _router_kernel(x_ref, g_ref, whi_ref, wlo_ref, br_ref, h_ref, ri_ref, rw_ref, cnt_ref, run_ref, *, tm):
    i = pl.program_id(0)

    @pl.when(i == 0)
    def _():
        run_ref[...] = jnp.zeros_like(run_ref)

    h = _rms(x_ref[...], g_ref[...], RMS_EPS)
    h_ref[...] = h
    hh = h.astype(BF16)
    hl = (h - hh.astype(F32)).astype(BF16)
    logits = _dot(hh, whi_ref[...]) + _dot(hh, wlo_ref[...]) + _dot(hl, whi_ref[...]) + br_ref[...]
    lane = lax.broadcasted_iota(jnp.int32, (tm, LANES), 1)
    m1 = jnp.max(logits, axis=-1, keepdims=True)
    i1 = jnp.min(jnp.where(logits == m1, lane, LANES), axis=-1, keepdims=True)
    l2 = jnp.where(lane == i1, -jnp.inf, logits)
    m2 = jnp.max(l2, axis=-1, keepdims=True)
    i2 = jnp.min(jnp.where(l2 == m2, lane, LANES), axis=-1, keepdims=True)
    e = jnp.exp(m2 - m1)
    w1 = 1.0 / (1.0 + e)
    w2 = e / (1.0 + e)
    sel1 = lane == i1
    sel2 = lane == i2
    onehot = (sel1 | sel2).astype(BF16)
    row = lax.broadcasted_iota(jnp.int32, (tm, tm), 0)
    col = lax.broadcasted_iota(jnp.int32, (tm, tm), 1)
    before = _dot((col < row).astype(BF16), onehot) + run_ref[...]
    pos1 = jnp.sum(jnp.where(sel1, before, 0.0), axis=-1, keepdims=True).astype(jnp.int32)
    pos2 = jnp.sum(jnp.where(sel2, before, 0.0), axis=-1, keepdims=True).astype(jnp.int32)
    run = run_ref[...] + jnp.sum(onehot.astype(F32), axis=0, keepdims=True)
    run_ref[...] = run
    cnt_ref[...] = run
    ri_ref[...] = jnp.where(lane == 0, i1, jnp.where(lane == 1, i2, jnp.where(lane == 2, pos1, jnp.where(lane == 3, pos2, 0))))
    rw_ref[...] = jnp.where(lane == 0, w1, jnp.where(lane == 1, w2, 0.0))


def _router(x2, g, whi, wlo, br, *, tm=256):
    T = x2.shape[0]
    return pl.pallas_call(
        functools.partial(_router_kernel, tm=tm),
        out_shape=(
            jax.ShapeDtypeStruct((T, D_MODEL), F32),
            jax.ShapeDtypeStruct((T, LANES), jnp.int32),
            jax.ShapeDtypeStruct((T, LANES), F32),
            jax.ShapeDtypeStruct((1, LANES), F32),
        ),
        grid=(T // tm,),
        in_specs=[
            pl.BlockSpec((tm, D_MODEL), lambda i: (i, 0)),
            _once((1, D_MODEL)),
            _once((D_MODEL, LANES)),
            _once((D_MODEL, LANES)),
            _once((1, LANES)),
        ],
        out_specs=(
            pl.BlockSpec((tm, D_MODEL), lambda i: (i, 0)),
            pl.BlockSpec((tm, LANES), lambda i: (i, 0)),
            pl.BlockSpec((tm, LANES), lambda i: (i, 0)),
            pl.BlockSpec((1, LANES), lambda i: (0, 0)),
        ),
        scratch_shapes=[pltpu.VMEM((1, LANES), F32)],
        compiler_params=_params("arbitrary"),
        name="router",
    )(x2, g, whi, wlo, br)


def _row_copy(src_ref, src_row, dst_ref, dst_row, sem):
    return pltpu.make_async_copy(src_ref.at[pl.ds(src_row, 1)], dst_ref.at[pl.ds(dst_row, 1)], sem)


def _dispatch_kernel(pe_ref, nu_ref, dest_ref, h_ref, xs_ref, zero_ref, sem, *, td, tmm, n_blocks):
    @pl.when(pl.program_id(0) == 0)
    def _():
        zero_ref[...] = jnp.zeros_like(zero_ref)

        def fill(start):
            return pltpu.make_async_copy(zero_ref, xs_ref.at[pl.ds(pl.multiple_of(start, tmm), tmm)], sem)

        for e in range(N_EXPERTS):
            prev_end = pe_ref[e - 1] if e > 0 else 0

            @pl.when(pe_ref[e] > prev_end)
            def _():
                fill(pe_ref[e] - tmm).start()
                fill(pe_ref[e] - tmm).wait()

        for j in range(N_EXPERTS):
            @pl.when(n_blocks - 1 - j >= nu_ref[0])
            def _():
                fill((n_blocks - 1 - j) * tmm).start()
                fill((n_blocks - 1 - j) * tmm).wait()

    for r in range(td):
        _row_copy(h_ref, r, xs_ref, dest_ref[2 * r], sem).start()
        _row_copy(h_ref, r, xs_ref, dest_ref[2 * r + 1], sem).start()
    for _ in range(2):
        pltpu.make_async_copy(h_ref, xs_ref.at[pl.ds(0, td)], sem).wait()


def _dispatch(pad_end, n_used, dest_flat, h, *, n_blocks, tmm, td=256):
    T = h.shape[0]
    return pl.pallas_call(
        functools.partial(_dispatch_kernel, td=td, tmm=tmm, n_blocks=n_blocks),
        out_shape=jax.ShapeDtypeStruct((n_blocks * tmm, D_MODEL), F32),
        grid_spec=pltpu.PrefetchScalarGridSpec(
            num_scalar_prefetch=2,
            grid=(T // td,),
            in_specs=[
                pl.BlockSpec((2 * td,), lambda i, pe, nu: (i,), memory_space=pltpu.SMEM),
                pl.BlockSpec((td, D_MODEL), lambda i, pe, nu: (i, 0)),
            ],
            out_specs=pl.BlockSpec(memory_space=pl.ANY),
            scratch_shapes=[pltpu.VMEM((tmm, D_MODEL), F32), pltpu.SemaphoreType.DMA],
        ),
        compiler_params=_params("arbitrary"),
        name="moe_dispatch",
    )(pad_end, n_used, dest_flat, h)


def _gmm_kernel(be_ref, nu_ref, xs_ref, wg_ref, wu_ref, wd_ref, o_ref):
    i = pl.program_id(0)
    f = pl.program_id(1)

    @pl.when(i < nu_ref[0])
    def _():
        x = xs_ref[...].astype(BF16)
        g = _dot(x, wg_ref[0])
        u = _dot(x, wu_ref[0])
        y = _dot((g * _sigmoid(g) * u).astype(BF16), wd_ref[0])

        @pl.when(f == 0)
        def _():
            o_ref[...] = y

        @pl.when(f > 0)
        def _():
            o_ref[...] += y

    @pl.when((i >= nu_ref[0]) & (f == 0))
    def _():
        o_ref[...] = jnp.zeros_like(o_ref)


def _gmm(block_e, n_used, xs, wg, wu, wd, *, tmm, tf):
    n_rows = xs.shape[0]
    n_blocks = n_rows // tmm
    F = wg.shape[2]
    nf = F // tf

    def row_map(i, f, be, nu):
        return (jnp.minimum(i, nu[0] - 1), 0)

    def f_idx(i, f, nu):
        return jnp.where(i < nu[0], f, nf - 1)

    return pl.pallas_call(
        _gmm_kernel,
        out_shape=jax.ShapeDtypeStruct((n_rows, D_MODEL), F32),
        grid_spec=pltpu.PrefetchScalarGridSpec(
            num_scalar_prefetch=2,
            grid=(n_blocks, nf),
            in_specs=[
                pl.BlockSpec((tmm, D_MODEL), row_map),
                pl.BlockSpec((1, D_MODEL, tf), lambda i, f, be, nu: (be[i], 0, f_idx(i, f, nu))),
                pl.BlockSpec((1, D_MODEL, tf), lambda i, f, be, nu: (be[i], 0, f_idx(i, f, nu))),
                pl.BlockSpec((1, tf, D_MODEL), lambda i, f, be, nu: (be[i], f_idx(i, f, nu), 0)),
            ],
            out_specs=pl.BlockSpec((tmm, D_MODEL), lambda i, f, be, nu: (i, 0)),
        ),
        compiler_params=_params("arbitrary", "arbitrary"),
        name="moe_grouped_swiglu",
    )(block_e, n_used, xs, wg, wu, wd)


def _combine_ple_kernel(dest_ref, dest_next_ref, x_ref, rw_ref, p_ref, gple_ref, wpg_ref, wpp_ref, gfin_ref,
                        yb_ref, o_ref, buf_ref, sem, *, tp):
    i = pl.program_id(0)
    slot = lax.rem(i, 2)

    def gather(d_ref, sl):
        for r in range(tp):
            _row_copy(yb_ref, d_ref[2 * r], buf_ref.at[sl, 0], r, sem.at[sl]).start()
            _row_copy(yb_ref, d_ref[2 * r + 1], buf_ref.at[sl, 1], r, sem.at[sl]).start()

    def wait(sl):
        for k in range(2):
            pltpu.make_async_copy(yb_ref.at[pl.ds(0, tp)], buf_ref.at[sl, k], sem.at[sl]).wait()

    @pl.when(i == 0)
    def _():
        gather(dest_ref, 0)

    gather(dest_next_ref, 1 - slot)
    wait(slot)
    rw = rw_ref[...]
    x1 = x_ref[...] + rw[:, 0:1] * buf_ref[slot, 0] + rw[:, 1:2] * buf_ref[slot, 1]
    x2 = _ple(x1, p_ref[...], gple_ref[...], wpg_ref[...], wpp_ref[...])
    o_ref[...] = _rms(x2, gfin_ref[...], RMS_EPS)

    @pl.when(i == pl.num_programs(0) - 1)
    def _():
        wait(1 - slot)


def _combine_ple(dest_flat, x2, rw, p2, gple, wpg, wpp, gfin, yb, *, tp=256):
    T = x2.shape[0]
    n = T // tp
    return pl.pallas_call(
        functools.partial(_combine_ple_kernel, tp=tp),
        out_shape=jax.ShapeDtypeStruct((T, D_MODEL), F32),
        grid=(n,),
        in_specs=[
            pl.BlockSpec((2 * tp,), lambda i: (i,), memory_space=pltpu.SMEM),
            pl.BlockSpec((2 * tp,), lambda i: (jnp.minimum(i + 1, n - 1),), memory_space=pltpu.SMEM),
            pl.BlockSpec((tp, D_MODEL), lambda i: (i, 0)),
            pl.BlockSpec((tp, LANES), lambda i: (i, 0)),
            pl.BlockSpec((tp, PLE_DIM), lambda i: (i, 0)),
            _once((1, D_MODEL)),
            _once((D_MODEL, D_MODEL)),
            _once((PLE_DIM, D_MODEL)),
            _once((1, D_MODEL)),
            pl.BlockSpec(memory_space=pl.ANY),
        ],
        out_specs=pl.BlockSpec((tp, D_MODEL), lambda i: (i, 0)),
        scratch_shapes=[pltpu.VMEM((2, 2, tp, D_MODEL), F32), pltpu.SemaphoreType.DMA((2,))],
        compiler_params=_params("arbitrary"),
        name="moe_combine_ple",
    )(dest_flat, dest_flat, x2, rw, p2, gple, wpg, wpp, gfin, yb)


def _moe_ple_final(x2, p2, gffn, w_router, b_router, wg, wu, wd, gple, wpg, wpp, gfin, *, tmm=512, tf=1792):
    T = x2.shape[0]
    wr = jnp.zeros((D_MODEL, LANES), F32).at[:, :N_EXPERTS].set(w_router)
    whi = wr.astype(BF16)
    wlo = (wr - whi.astype(F32)).astype(BF16)
    br = jnp.full((1, LANES), NEG_BIG, F32).at[0, :N_EXPERTS].set(b_router)
    h, ri, rw, cnt = _router(x2, gffn, whi, wlo, br)
    counts = cnt[0, :N_EXPERTS].astype(jnp.int32)
    padded = (counts + tmm - 1) // tmm * tmm
    pad_end = jnp.cumsum(padded).astype(jnp.int32)
    pad_start = pad_end - padded
    e_sel = ri[:, 0:2, None] == jnp.arange(N_EXPERTS, dtype=jnp.int32)
    dest = jnp.sum(jnp.where(e_sel, pad_start, 0), axis=-1) + ri[:, 2:4]
    dest_flat = dest.reshape(2 * T).astype(jnp.int32)
    n_blocks = (2 * T) // tmm + N_EXPERTS
    block_start = jnp.arange(n_blocks, dtype=jnp.int32) * tmm
    block_e = jnp.minimum(jnp.sum(block_start[:, None] >= pad_end[None, :], axis=1), N_EXPERTS - 1).astype(jnp.int32)
    n_used = (pad_end[-1] // tmm).astype(jnp.int32).reshape(1)
    xs = _dispatch(pad_end, n_used, dest_flat, h, n_blocks=n_blocks, tmm=tmm)
    yb = _gmm(block_e, n_used, xs, wg, wu, wd, tmm=tmm, tf=tf)
    return _combine_ple(dest_flat, x2, rw, p2, gple, wpg, wpp, gfin, yb)


def kernel(x, p, g_mix, w_in, w_gk_up, b_gk, g_gla_norm, conv_w, conv_b, ln_conv_g, ln_conv_b, w_conv_out, w_pool, pool_scale, w_out, g_ffn, w_ffn_gate, w_ffn_up, w_ffn_down, w_router, b_router, w_moe_gate, w_moe_up, w_moe_down, g_ple, w_ple_gate, w_ple_proj, g_final):
    B, S, _ = x.shape
    T = B * S
    depth = w_in.shape[0]
    assert depth == 2 and x.shape[2] == D_MODEL, "dense layer 0, routed layer 1 with the final norm fused"
    x2 = x.reshape(T, D_MODEL)
    row = lambda a: a.reshape(1, -1)
    for i in range(depth):
        wi = w_in[i]
        w_main = jnp.concatenate([wi[:, :3072], wi[:, 3088:]], axis=1).astype(BF16)
        w_gk = jnp.zeros((D_MODEL, GK_PAD), F32).at[:, :GLA_GATE_RANK].set(wi[:, 3072:3088]).astype(BF16)
        w_gk_up_p = jnp.zeros((GK_PAD, GLA_KEY_DIM), F32).at[:GLA_GATE_RANK].set(w_gk_up[i]).astype(BF16)
        z, zgk = _in_proj(x2, row(g_mix[i]), w_main, w_gk)
        x2 = _mixer(z, zgk, x2, w_gk_up_p, row(b_gk[i]), row(g_gla_norm[i]), conv_w[i], row(conv_b[i]),
                    row(ln_conv_g[i]), row(ln_conv_b[i]), w_conv_out[i].astype(BF16), w_pool[i].astype(BF16),
                    row(pool_scale[i]), w_out[i].astype(BF16), B=B, S=S)
        p2 = p[i].reshape(T, PLE_DIM)
        wpg = w_ple_gate[i].astype(BF16)
        wpp = w_ple_proj[i].astype(BF16)
        if i % 2 == 0:
            j = i // 2
            x2 = _ffn_ple(x2, p2, row(g_ffn[i]), w_ffn_gate[j].astype(BF16), w_ffn_up[j].astype(BF16),
                          w_ffn_down[j].astype(BF16), row(g_ple[i]), wpg, wpp)
        else:
            j = i // 2
            x2 = _moe_ple_final(x2, p2, row(g_ffn[i]), w_router[j], b_router[j], w_moe_gate[j].astype(BF16),
                                w_moe_up[j].astype(BF16), w_moe_down[j].astype(BF16), row(g_ple[i]), wpg, wpp,
                                row(g_final))
    return x2.reshape(B, S, D_MODEL)
```

```python
import functools

import jax
import jax.numpy as jnp
from jax import lax
from jax.experimental import pallas as pl
from jax.experimental.pallas import tpu as pltpu

F32 = jnp.float32
BF16 = jnp.bfloat16

D_MODEL = 1024
GLA_HEADS = 4
GLA_KEY_DIM = 512
GLA_HEAD_K = 128
GLA_HEAD_V = 256
GLA_GATE_RANK = 16
GLA_GATE_NORMALIZER = 16.0
GLA_CHUNK = 64
GLA_NORM_EPS = 1e-5
CONV_WIDTH = 31
LN_EPS = 1e-5
POOL_WINDOWS = (2, 4, 8, 16)
POOL_GROUP_CH = 256
N_EXPERTS = 8
PLE_DIM = 256
RMS_EPS = 1e-6

Z_Q, Z_K, Z_V, Z_G, Z_CA, Z_CG, Z_POOL, Z_GATE = 0, 512, 1024, 2048, 3072, 4096, 5120, 6144
Z_DIM = 9216
GK_PAD = 128
LANES = 128

CONV_HALO = 32
POOL_HALO = 128
CONV_RC = 32
VMEM_LIMIT = 56 * 1024 * 1024

NEG_BIG = -1e30


def _dot(a, b):
    return jnp.dot(a, b, preferred_element_type=F32)


def _dot_nt(a, b):
    return lax.dot_general(a, b, (((1,), (1,)), ((), ())), preferred_element_type=F32)


def _dot_tn(a, b):
    return lax.dot_general(a, b, (((0,), (0,)), ((), ())), preferred_element_type=F32)


def _sigmoid(x):
    return 0.5 * jnp.tanh(0.5 * x) + 0.5


def _rms(x, g, eps):
    return x * lax.rsqrt(jnp.mean(x * x, axis=-1, keepdims=True) + eps) * g


def _split3(x):
    hi = x.astype(BF16)
    r1 = x - hi.astype(F32)
    mid = r1.astype(BF16)
    lo = (r1 - mid.astype(F32)).astype(BF16)
    return hi, mid, lo


def _once(shape):
    n = len(shape)
    return pl.BlockSpec(shape, lambda *_: (0,) * n, pipeline_mode=pl.Buffered(1))


def _params(*semantics):
    return pltpu.CompilerParams(dimension_semantics=semantics, vmem_limit_bytes=VMEM_LIMIT)


def _in_proj_kernel(x_ref, g_ref, w_ref, wgk_ref, z_ref, zgk_ref, *, nc):
    h = _rms(x_ref[...], g_ref[...], RMS_EPS).astype(BF16)
    for c in range(0, Z_DIM, nc):
        z_ref[:, c:c + nc] = _dot(h, w_ref[:, c:c + nc]).astype(BF16)
    zgk_ref[...] = _dot(h, wgk_ref[...])


def _in_proj(x2, g, w_main, w_gk, *, tm=512, nc=1024):
    T = x2.shape[0]
    return pl.pallas_call(
        functools.partial(_in_proj_kernel, nc=nc),
        out_shape=(jax.ShapeDtypeStruct((T, Z_DIM), BF16), jax.ShapeDtypeStruct((T, GK_PAD), F32)),
        grid=(T // tm,),
        in_specs=[
            pl.BlockSpec((tm, D_MODEL), lambda i: (i, 0)),
            _once((1, D_MODEL)),
            _once((D_MODEL, Z_DIM)),
            _once((D_MODEL, GK_PAD)),
        ],
        out_specs=(pl.BlockSpec((tm, Z_DIM), lambda i: (i, 0)), pl.BlockSpec((tm, GK_PAD), lambda i: (i, 0))),
        compiler_params=_params("arbitrary"),
        name="in_proj",
    )(x2, g, w_main, w_gk)


def _mixer_kernel(z_ref, zgk_ref, x_ref, wgk_ref, bgk_ref, gnorm_ref, convw_ref, convb_ref, lng_ref, lnb_ref,
                  wco_ref, wpool_ref, pscale_ref, wout_ref, o_ref,
                  state_ref, ubuf_ref, pbuf_ref, ygla_ref, yconv_ref, *, ts):
    s = pl.program_id(1)
    C = GLA_CHUNK

    @pl.when(s == 0)
    def _():
        state_ref[...] = jnp.zeros_like(state_ref)
        ubuf_ref[0:CONV_HALO, :] = jnp.zeros((CONV_HALO, D_MODEL), F32)
        pbuf_ref[0:POOL_HALO, :] = jnp.zeros((POOL_HALO, D_MODEL), BF16)

    gk_pre = _dot(zgk_ref[...].astype(BF16), wgk_ref[...]) + bgk_ref[...]
    gk = (jnp.minimum(gk_pre, 0.0) - jnp.log(1.0 + jnp.exp(-jnp.abs(gk_pre)))) * (1.0 / GLA_GATE_NORMALIZER)
    row = lax.broadcasted_iota(jnp.int32, (ts, ts), 0)
    col = lax.broadcasted_iota(jnp.int32, (ts, ts), 1)
    same_chunk_causal = (jnp.right_shift(row, 6) == jnp.right_shift(col, 6)) & (col <= row)
    tri = same_chunk_causal.astype(BF16)
    g_hi, g_mid, g_lo = _split3(gk)
    b_all = _dot(tri, g_hi) + _dot(tri, g_mid) + _dot(tri, g_lo)
    nchunk = ts // C

    def per_chunk_row(r):
        return jnp.concatenate(
            [jnp.broadcast_to(b_all[c * C + r:c * C + r + 1, :], (C, GLA_KEY_DIM)) for c in range(nchunk)], axis=0)

    b_mid = per_chunk_row(C // 2 - 1)
    b_last = per_chunk_row(C - 1)
    q = z_ref[:, Z_Q:Z_Q + GLA_KEY_DIM].astype(F32) * (GLA_HEAD_K ** -0.5)
    k = z_ref[:, Z_K:Z_K + GLA_KEY_DIM].astype(F32)
    qa = (q * jnp.exp(b_all - b_mid)).astype(BF16)
    ka = (k * jnp.exp(b_mid - b_all)).astype(BF16)
    qd = (q * jnp.exp(b_all)).astype(BF16)
    kd = (k * jnp.exp(b_last - b_all)).astype(BF16)
    e_dec = [jnp.exp(b_all[c * C + C - 1:c * C + C, :]) for c in range(nchunk)]
    gnorm = gnorm_ref[...]
    for h in range(GLA_HEADS):
        ks = slice(h * GLA_HEAD_K, (h + 1) * GLA_HEAD_K)
        v = z_ref[:, Z_V + h * GLA_HEAD_V:Z_V + (h + 1) * GLA_HEAD_V]
        attn = jnp.where(same_chunk_causal, _dot_nt(qa[:, ks], ka[:, ks]), 0.0).astype(BF16)
        o = _dot(attn, v)
        st = state_ref[h]
        o_inter = []
        for c in range(nchunk):
            rs = slice(c * C, (c + 1) * C)
            o_inter.append(_dot_nt(qd[rs, ks], st.astype(BF16)))
            st = st * e_dec[c][:, ks] + _dot_tn(v[rs, :], kd[rs, ks])
        state_ref[h] = st
        o = o + jnp.concatenate(o_inter, axis=0)
        o = o * lax.rsqrt(jnp.mean(o * o, axis=-1, keepdims=True) + GLA_NORM_EPS) * gnorm
        gi = z_ref[:, Z_G + h * GLA_HEAD_V:Z_G + (h + 1) * GLA_HEAD_V].astype(F32)
        ygla_ref[:, h * GLA_HEAD_V:(h + 1) * GLA_HEAD_V] = o * (gi * _sigmoid(gi))

    ca = z_ref[:, Z_CA:Z_CA + D_MODEL].astype(F32)
    cg = z_ref[:, Z_CG:Z_CG + D_MODEL].astype(F32)
    ubuf_ref[CONV_HALO:CONV_HALO + ts, :] = ca * _sigmoid(cg)
    RC = CONV_RC

    def conv_rows(i, carry):
        base = pl.multiple_of(i * RC, RC)
        for cc in range(D_MODEL // LANES):
            cs = slice(cc * LANES, (cc + 1) * LANES)
            acc = None
            for r in range(8):
                n_rows = RC if r == 0 else RC + 8
                part = None
                for a in range(5):
                    o = 8 * a + r
                    if o < 2 or o > CONV_HALO:
                        continue
                    kk = o - 2
                    term = convw_ref[kk:kk + 1, cs] * ubuf_ref[pl.ds(base + 8 * a, n_rows), cs]
                    part = term if part is None else part + term
                part = part[r:r + RC, :]
                acc = part if acc is None else acc + part
            yconv_ref[pl.ds(base, RC), cs] = acc + convb_ref[:, cs]
        return carry

    lax.fori_loop(0, ts // RC, conv_rows, 0)
    ubuf_ref[0:CONV_HALO, :] = ubuf_ref[ts:ts + CONV_HALO, :]
    yc = yconv_ref[...]
    mu = jnp.mean(yc, axis=-1, keepdims=True)
    ycc = yc - mu
    var = jnp.mean(ycc * ycc, axis=-1, keepdims=True)
    yn = ycc * lax.rsqrt(var + LN_EPS) * lng_ref[...] + lnb_ref[...]
    y_conv = _dot((yn * _sigmoid(yn)).astype(BF16), wco_ref[...])

    pbuf_ref[POOL_HALO:POOL_HALO + ts, :] = z_ref[:, Z_POOL:Z_POOL + D_MODEL]
    tpos = s * ts + lax.broadcasted_iota(jnp.int32, (ts, 1), 0)
    prow = lax.broadcasted_iota(jnp.int32, (ts, POOL_HALO + ts), 0) + POOL_HALO
    pcol = lax.broadcasted_iota(jnp.int32, (ts, POOL_HALO + ts), 1)
    y_pool = []
    for gi_, w in enumerate(POOL_WINDOWS):
        cs = slice(gi_ * POOL_GROUP_CH, (gi_ + 1) * POOL_GROUP_CH)
        band = ((pcol <= prow) & (pcol > prow - w)).astype(BF16)
        wsum = _dot(band, pbuf_ref[:, cs])
        cnt = jnp.minimum(tpos + 1, w).astype(F32)
        d = wsum / cnt - z_ref[:, Z_POOL + gi_ * POOL_GROUP_CH:Z_POOL + (gi_ + 1) * POOL_GROUP_CH].astype(F32)
        y_pool.append(_dot(d.astype(BF16), wpool_ref[gi_]) * pscale_ref[:, cs])
    pbuf_ref[0:POOL_HALO, :] = pbuf_ref[ts:ts + POOL_HALO, :]

    mix_parts = []
    for gi_ in range(4):
        cs = slice(gi_ * POOL_GROUP_CH, (gi_ + 1) * POOL_GROUP_CH)
        g0 = _sigmoid(z_ref[:, Z_GATE + gi_ * 256:Z_GATE + (gi_ + 1) * 256])
        g1 = _sigmoid(z_ref[:, Z_GATE + D_MODEL + gi_ * 256:Z_GATE + D_MODEL + (gi_ + 1) * 256])
        g2 = _sigmoid(z_ref[:, Z_GATE + 2 * D_MODEL + gi_ * 256:Z_GATE + 2 * D_MODEL + (gi_ + 1) * 256])
        m = g0 * ygla_ref[:, cs].astype(BF16) + g1 * y_conv[:, cs].astype(BF16) + g2 * y_pool[gi_].astype(BF16)
        mix_parts.append(m)
    o_ref[...] = x_ref[...] + _dot(jnp.concatenate(mix_parts, axis=1), wout_ref[...])


def _mixer(z, zgk, x2, wgk, bgk, gnorm, convw, convb, lng, lnb, wco, wpool, pscale, wout, *, B, S, ts=256):
    T = B * S
    nt = S // ts
    rowmap = lambda b, s: (b * nt + s, 0)
    return pl.pallas_call(
        functools.partial(_mixer_kernel, ts=ts),
        out_shape=jax.ShapeDtypeStruct((T, D_MODEL), F32),
        grid=(B, nt),
        in_specs=[
            pl.BlockSpec((ts, Z_DIM), rowmap),
            pl.BlockSpec((ts, GK_PAD), rowmap),
            pl.BlockSpec((ts, D_MODEL), rowmap),
            _once((GK_PAD, GLA_KEY_DIM)),
            _once((1, GLA_KEY_DIM)),
            _once((1, GLA_HEAD_V)),
            _once((CONV_WIDTH, D_MODEL)),
            _once((1, D_MODEL)),
            _once((1, D_MODEL)),
            _once((1, D_MODEL)),
            _once((D_MODEL, D_MODEL)),
            _once((4, POOL_GROUP_CH, POOL_GROUP_CH)),
            _once((1, D_MODEL)),
            _once((D_MODEL, D_MODEL)),
        ],
        out_specs=pl.BlockSpec((ts, D_MODEL), rowmap),
        scratch_shapes=[
            pltpu.VMEM((GLA_HEADS, GLA_HEAD_V, GLA_HEAD_K), F32),
            pltpu.VMEM((CONV_HALO + ts, D_MODEL), F32),
            pltpu.VMEM((POOL_HALO + ts, D_MODEL), BF16),
            pltpu.VMEM((ts, D_MODEL), F32),
            pltpu.VMEM((ts, D_MODEL), F32),
        ],
        compiler_params=_params("arbitrary", "arbitrary"),
        name="token_mixer",
    )(z, zgk, x2, wgk, bgk, gnorm, convw, convb, lng, lnb, wco, wpool, pscale, wout)


def _ple(x1, p, gple, wpg, wpp):
    hp = _rms(x1, gple, RMS_EPS).astype(BF16)
    return x1 + _sigmoid(_dot(hp, wpg)) * _dot(p.astype(BF16), wpp)


def _ffn_ple_kernel(x_ref, p_ref, gffn_ref, wgu_ref, wd_ref, gple_ref, wpg_ref, wpp_ref, o_ref):
    x = x_ref[...]
    h = _rms(x, gffn_ref[...], RMS_EPS).astype(BF16)
    F = wd_ref.shape[0]
    gu = _dot(h, wgu_ref[...])
    g = gu[:, :F]
    u = gu[:, F:]
    y = x + _dot((g * _sigmoid(g) * u).astype(BF16), wd_ref[...])
    o_ref[...] = _ple(y, p_ref[...], gple_ref[...], wpg_ref[...], wpp_ref[...])


def _ffn_ple(x2, p2, gffn, wgu, wd, gple, wpg, wpp, *, tm=512):
    T = x2.shape[0]
    F = wd.shape[0]
    return pl.pallas_call(
        _ffn_ple_kernel,
        out_shape=jax.ShapeDtypeStruct((T, D_MODEL), F32),
        grid=(T // tm,),
        in_specs=[
            pl.BlockSpec((tm, D_MODEL), lambda i: (i, 0)),
            pl.BlockSpec((tm, PLE_DIM), lambda i: (i, 0)),
            _once((1, D_MODEL)),
            _once((D_MODEL, 2 * F)),
            _once((F, D_MODEL)),
            _once((1, D_MODEL)),
            _once((D_MODEL, D_MODEL)),
            _once((PLE_DIM, D_MODEL)),
        ],
        out_specs=pl.BlockSpec((tm, D_MODEL), lambda i: (i, 0)),
        compiler_params=_params("arbitrary"),
        name="ffn_ple",
    )(x2, p2, gffn, wgu, wd, gple, wpg, wpp)


def _router_kernel(x_ref, g_ref, whi_ref, wlo_ref, br_ref, h_ref, ri_ref, rw_ref, cnt_ref, run_ref, *, tm):
    i = pl.program_id(0)

    @pl.when(i == 0)
    def _():
        run_ref[...] = jnp.zeros_like(run_ref)

    h = _rms(x_ref[...], g_ref[...], RMS_EPS)
    h_ref[...] = h
    hh = h.astype(BF16)
    hl = (h - hh.astype(F32)).astype(BF16)
    logits = _dot(hh, whi_ref[...]) + _dot(hh, wlo_ref[...]) + _dot(hl, whi_ref[...]) + br_ref[...]
    lane = lax.broadcasted_iota(jnp.int32, (tm, LANES), 1)
    m1 = jnp.max(logits, axis=-1, keepdims=True)
    i1 = jnp.min(jnp.where(logits == m1, lane, LANES), axis=-1, keepdims=True)
    l2 = jnp.where(lane == i1, -jnp.inf, logits)
    m2 = jnp.max(l2, axis=-1, keepdims=True)
    i2 = jnp.min(jnp.where(l2 == m2, lane, LANES), axis=-1, keepdims=True)
    e = jnp.exp(m2 - m1)
    w1 = 1.0 / (1.0 + e)
    w2 = e / (1.0 + e)
    sel1 = lane == i1
    sel2 = lane == i2
    onehot = (sel1 | sel2).astype(BF16)
    row = lax.broadcasted_iota(jnp.int32, (tm, tm), 0)
    col = lax.broadcasted_iota(jnp.int32, (tm, tm), 1)
    before = _dot((col < row).astype(BF16), onehot) + run_ref[...]
    pos1 = jnp.sum(jnp.where(sel1, before, 0.0), axis=-1, keepdims=True).astype(jnp.int32)
    pos2 = jnp.sum(jnp.where(sel2, before, 0.0), axis=-1, keepdims=True).astype(jnp.int32)
    run = run_ref[...] + jnp.sum(onehot.astype(F32), axis=0, keepdims=True)
    run_ref[...] = run
    cnt_ref[...] = run
    ri_ref[...] = jnp.where(lane == 0, i1, jnp.where(lane == 1, i2, jnp.where(lane == 2, pos1, jnp.where(lane == 3, pos2, 0))))
    rw_ref[...] = jnp.where(lane == 0, w1, jnp.where(lane == 1, w2, 0.0))


def _router(x2, g, whi, wlo, br, *, tm=512):
    T = x2.shape[0]
    return pl.pallas_call(
        functools.partial(_router_kernel, tm=tm),
        out_shape=(
            jax.ShapeDtypeStruct((T, D_MODEL), F32),
            jax.ShapeDtypeStruct((T, LANES), jnp.int32),
            jax.ShapeDtypeStruct((T, LANES), F32),
            jax.ShapeDtypeStruct((1, LANES), F32),
        ),
        grid=(T // tm,),
        in_specs=[
            pl.BlockSpec((tm, D_MODEL), lambda i: (i, 0)),
            _once((1, D_MODEL)),
            _once((D_MODEL, LANES)),
            _once((D_MODEL, LANES)),
            _once((1, LANES)),
        ],
        out_specs=(
            pl.BlockSpec((tm, D_MODEL), lambda i: (i, 0)),
            pl.BlockSpec((tm, LANES), lambda i: (i, 0)),
            pl.BlockSpec((tm, LANES), lambda i: (i, 0)),
            pl.BlockSpec((1, LANES), lambda i: (0, 0)),
        ),
        scratch_shapes=[pltpu.VMEM((1, LANES), F32)],
        compiler_params=_params("arbitrary"),
        name="router",
    )(x2, g, whi, wlo, br)


def _row_copy(src_ref, src_row, dst_ref, dst_row, sem):
    return pltpu.make_async_copy(src_ref.at[pl.ds(src_row, 1)], dst_ref.at[pl.ds(dst_row, 1)], sem)


def _dispatch_kernel(pe_ref, nu_ref, dest_ref, h_ref, xs_ref, zero_ref, sem, *, td, tmm, n_blocks):
    @pl.when(pl.program_id(0) == 0)
    def _():
        zero_ref[...] = jnp.zeros_like(zero_ref)

        def fill(start):
            return pltpu.make_async_copy(zero_ref, xs_ref.at[pl.ds(pl.multiple_of(start, tmm), tmm)], sem)

        for e in range(N_EXPERTS):
            prev_end = pe_ref[e - 1] if e > 0 else 0

            @pl.when(pe_ref[e] > prev_end)
            def _():
                fill(pe_ref[e] - tmm).start()
                fill(pe_ref[e] - tmm).wait()

        for j in range(N_EXPERTS):
            @pl.when(n_blocks - 1 - j >= nu_ref[0])
            def _():
                fill((n_blocks - 1 - j) * tmm).start()
                fill((n_blocks - 1 - j) * tmm).wait()

    for r in range(td):
        _row_copy(h_ref, r, xs_ref, dest_ref[2 * r], sem).start(priority=0)
        _row_copy(h_ref, r, xs_ref, dest_ref[2 * r + 1], sem).start(priority=1)
    for _ in range(2):
        pltpu.make_async_copy(h_ref, xs_ref.at[pl.ds(0, td)], sem).wait()


def _dispatch(pad_end, n_used, dest_flat, h, *, n_blocks, tmm, td=256):
    T = h.shape[0]
    return pl.pallas_call(
        functools.partial(_dispatch_kernel, td=td, tmm=tmm, n_blocks=n_blocks),
        out_shape=jax.ShapeDtypeStruct((n_blocks * tmm, D_MODEL), F32),
        grid_spec=pltpu.PrefetchScalarGridSpec(
            num_scalar_prefetch=2,
            grid=(T // td,),
            in_specs=[
                pl.BlockSpec((2 * td,), lambda i, pe, nu: (i,), memory_space=pltpu.SMEM),
                pl.BlockSpec((td, D_MODEL), lambda i, pe, nu: (i, 0)),
            ],
            out_specs=pl.BlockSpec(memory_space=pl.ANY),
            scratch_shapes=[pltpu.VMEM((tmm, D_MODEL), F32), pltpu.SemaphoreType.DMA],
        ),
        compiler_params=_params("arbitrary"),
        name="moe_dispatch",
    )(pad_end, n_used, dest_flat, h)


def _gmm_kernel(be_ref, nu_ref, xs_ref, wgu_ref, wd_ref, o_ref):
    i = pl.program_id(0)
    f = pl.program_id(1)
    tf = wd_ref.shape[1]

    @pl.when(i < nu_ref[0])
    def _():
        x = xs_ref[...].astype(BF16)
        gu = _dot(x, wgu_ref[0])
        g = gu[:, :tf]
        u = gu[:, tf:]
        y = _dot((g * _sigmoid(g) * u).astype(BF16), wd_ref[0])

        @pl.when(f == 0)
        def _():
            o_ref[...] = y

        @pl.when(f > 0)
        def _():
            o_ref[...] += y

    @pl.when((i >= nu_ref[0]) & (f == 0))
    def _():
        o_ref[...] = jnp.zeros_like(o_ref)


def _gmm(block_e, n_used, xs, wgu, wd, *, tmm, tf):
    n_rows = xs.shape[0]
    n_blocks = n_rows // tmm
    F = wd.shape[1]
    nf = F // tf

    def row_map(i, f, be, nu):
        return (jnp.minimum(i, nu[0] - 1), 0)

    def f_idx(i, f, nu):
        return jnp.where(i < nu[0], f, nf - 1)

    return pl.pallas_call(
        _gmm_kernel,
        out_shape=jax.ShapeDtypeStruct((n_rows, D_MODEL), F32),
        grid_spec=pltpu.PrefetchScalarGridSpec(
            num_scalar_prefetch=2,
            grid=(n_blocks, nf),
            in_specs=[
                pl.BlockSpec((tmm, D_MODEL), row_map),
                pl.BlockSpec((1, D_MODEL, 2 * tf), lambda i, f, be, nu: (be[i], 0, f_idx(i, f, nu))),
                pl.BlockSpec((1, tf, D_MODEL), lambda i, f, be, nu: (be[i], f_idx(i, f, nu), 0)),
            ],
            out_specs=pl.BlockSpec((tmm, D_MODEL), lambda i, f, be, nu: (i, 0)),
        ),
        compiler_params=_params("arbitrary", "arbitrary"),
        name="moe_grouped_swiglu",
    )(block_e, n_used, xs, wgu, wd)


def _combine_ple_kernel(dest_ref, dest_next_ref, x_ref, rw_ref, p_ref, gple_ref, wpg_ref, wpp_ref, gfin_ref,
                        yb_ref, o_ref, buf_ref, sem, *, tp):
    i = pl.program_id(0)
    slot = lax.rem(i, 2)

    def gather(d_ref, sl):
        for r in range(tp):
            _row_copy(yb_ref, d_ref[2 * r], buf_ref.at[sl, 0], r, sem.at[sl]).start(priority=0)
            _row_copy(yb_ref, d_ref[2 * r + 1], buf_ref.at[sl, 1], r, sem.at[sl]).start(priority=1)

    def wait(sl):
        for k in range(2):
            pltpu.make_async_copy(yb_ref.at[pl.ds(0, tp)], buf_ref.at[sl, k], sem.at[sl]).wait()

    @pl.when(i == 0)
    def _():
        gather(dest_ref, 0)

    gather(dest_next_ref, 1 - slot)
    wait(slot)
    rw = rw_ref[...]
    x1 = x_ref[...] + rw[:, 0:1] * buf_ref[slot, 0] + rw[:, 1:2] * buf_ref[slot, 1]
    x2 = _ple(x1, p_ref[...], gple_ref[...], wpg_ref[...], wpp_ref[...])
    o_ref[...] = _rms(x2, gfin_ref[...], RMS_EPS)

    @pl.when(i == pl.num_programs(0) - 1)
    def _():
        wait(1 - slot)


def _combine_ple(dest_flat, x2, rw, p2, gple, wpg, wpp, gfin, yb, *, tp=256):
    T = x2.shape[0]
    n = T // tp
    return pl.pallas_call(
        functools.partial(_combine_ple_kernel, tp=tp),
        out_shape=jax.ShapeDtypeStruct((T, D_MODEL), F32),
        grid=(n,),
        in_specs=[
            pl.BlockSpec((2 * tp,), lambda i: (i,), memory_space=pltpu.SMEM),
            pl.BlockSpec((2 * tp,), lambda i: (jnp.minimum(i + 1, n - 1),), memory_space=pltpu.SMEM),
            pl.BlockSpec((tp, D_MODEL), lambda i: (i, 0)),
            pl.BlockSpec((tp, LANES), lambda i: (i, 0)),
            pl.BlockSpec((tp, PLE_DIM), lambda i: (i, 0)),
            _once((1, D_MODEL)),
            _once((D_MODEL, D_MODEL)),
            _once((PLE_DIM, D_MODEL)),
            _once((1, D_MODEL)),
            pl.BlockSpec(memory_space=pl.ANY),
        ],
        out_specs=pl.BlockSpec((tp, D_MODEL), lambda i: (i, 0)),
        scratch_shapes=[pltpu.VMEM((2, 2, tp, D_MODEL), F32), pltpu.SemaphoreType.DMA((2,))],
        compiler_params=_params("arbitrary"),
        name="moe_combine_ple",
    )(dest_flat, dest_flat, x2, rw, p2, gple, wpg, wpp, gfin, yb)


def _moe_ple_final(x2, p2, gffn, w_router, b_router, wg, wu, wd, gple, wpg, wpp, gfin, *, tmm=512, tf=1792):
    T = x2.shape[0]
    wr = jnp.zeros((D_MODEL, LANES), F32).at[:, :N_EXPERTS].set(w_router)
    whi = wr.astype(BF16)
    wlo = (wr - whi.astype(F32)).astype(BF16)
    br = jnp.full((1, LANES), NEG_BIG, F32).at[0, :N_EXPERTS].set(b_router)
    h, ri, rw, cnt = _router(x2, gffn, whi, wlo, br)
    counts = cnt[0, :N_EXPERTS].astype(jnp.int32)
    padded = (counts + tmm - 1) // tmm * tmm
    pad_end = jnp.cumsum(padded).astype(jnp.int32)
    pad_start = pad_end - padded
    e_sel = ri[:, 0:2, None] == jnp.arange(N_EXPERTS, dtype=jnp.int32)
    dest = jnp.sum(jnp.where(e_sel, pad_start, 0), axis=-1) + ri[:, 2:4]
    dest_flat = dest.reshape(2 * T).astype(jnp.int32)
    n_blocks = (2 * T) // tmm + N_EXPERTS
    block_start = jnp.arange(n_blocks, dtype=jnp.int32) * tmm
    block_e = jnp.minimum(jnp.sum(block_start[:, None] >= pad_end[None, :], axis=1), N_EXPERTS - 1).astype(jnp.int32)
    n_used = (pad_end[-1] // tmm).astype(jnp.int32).reshape(1)
    xs = _dispatch(pad_end, n_used, dest_flat, h, n_blocks=n_blocks, tmm=tmm)
    E, _, F = wg.shape
    tiles = lambda w: w.astype(BF16).reshape(E, D_MODEL, F // tf, tf)
    wgu = jnp.concatenate([tiles(wg), tiles(wu)], axis=3).reshape(E, D_MODEL, 2 * F)
    yb = _gmm(block_e, n_used, xs, wgu, wd.astype(BF16), tmm=tmm, tf=tf)
    return _combine_ple(dest_flat, x2, rw, p2, gple, wpg, wpp, gfin, yb)


def kernel(x, p, g_mix, w_in, w_gk_up, b_gk, g_gla_norm, conv_w, conv_b, ln_conv_g, ln_conv_b, w_conv_out, w_pool, pool_scale, w_out, g_ffn, w_ffn_gate, w_ffn_up, w_ffn_down, w_router, b_router, w_moe_gate, w_moe_up, w_moe_down, g_ple, w_ple_gate, w_ple_proj, g_final):
    B, S, _ = x.shape
    T = B * S
    depth = w_in.shape[0]
    assert depth == 2 and x.shape[2] == D_MODEL, "dense layer 0, routed layer 1 with the final norm fused"
    x2 = x.reshape(T, D_MODEL)
    row = lambda a: a.reshape(1, -1)
    for i in range(depth):
        wi = w_in[i]
        w_main = jnp.concatenate([wi[:, :3072], wi[:, 3088:]], axis=1).astype(BF16)
        w_gk = jnp.zeros((D_MODEL, GK_PAD), F32).at[:, :GLA_GATE_RANK].set(wi[:, 3072:3088]).astype(BF16)
        w_gk_up_p = jnp.zeros((GK_PAD, GLA_KEY_DIM), F32).at[:GLA_GATE_RANK].set(w_gk_up[i]).astype(BF16)
        z, zgk = _in_proj(x2, row(g_mix[i]), w_main, w_gk)
        x2 = _mixer(z, zgk, x2, w_gk_up_p, row(b_gk[i]), row(g_gla_norm[i]), conv_w[i], row(conv_b[i]),
                    row(ln_conv_g[i]), row(ln_conv_b[i]), w_conv_out[i].astype(BF16), w_pool[i].astype(BF16),
                    row(pool_scale[i]), w_out[i].astype(BF16), B=B, S=S)
        p2 = p[i].reshape(T, PLE_DIM)
        wpg = w_ple_gate[i].astype(BF16)
        wpp = w_ple_proj[i].astype(BF16)
        if i % 2 == 0:
            j = i // 2
            wgu = jnp.concatenate([w_ffn_gate[j].astype(BF16), w_ffn_up[j].astype(BF16)], axis=1)
            x2 = _ffn_ple(x2, p2, row(g_ffn[i]), wgu, w_ffn_down[j].astype(BF16), row(g_ple[i]), wpg, wpp)
        else:
            j = i // 2
            x2 = _moe_ple_final(x2, p2, row(g_ffn[i]), w_router[j], b_router[j], w_moe_gate[j], w_moe_up[j],
                                w_moe_down[j], row(g_ple[i]), wpg, wpp, row(g_final))
    return x2.reshape(B, S, D_MODEL)
```

```python
import functools

import jax
import jax.numpy as jnp
from jax import lax
from jax.experimental import pallas as pl
from jax.experimental.pallas import tpu as pltpu

F32 = jnp.float32
BF16 = jnp.bfloat16

D_MODEL = 1024
GLA_HEADS = 4
GLA_KEY_DIM = 512
GLA_HEAD_K = 128
GLA_HEAD_V = 256
GLA_GATE_RANK = 16
GLA_GATE_NORMALIZER = 16.0
GLA_CHUNK = 64
GLA_NORM_EPS = 1e-5
CONV_WIDTH = 31
LN_EPS = 1e-5
POOL_WINDOWS = (2, 4, 8, 16)
POOL_GROUP_CH = 256
N_EXPERTS = 8
PLE_DIM = 256
RMS_EPS = 1e-6

Z_Q, Z_K, Z_V, Z_G, Z_CA, Z_CG, Z_POOL, Z_GATE = 0, 512, 1024, 2048, 3072, 4096, 5120, 6144
Z_DIM = 9216
GK_PAD = 128
LANES = 128

CONV_HALO = 32
POOL_HALO = 128
CONV_RC = 32
VMEM_LIMIT = 56 * 1024 * 1024

NEG_BIG = -1e30


def _dot(a, b):
    return jnp.dot(a, b, preferred_element_type=F32)


def _dot_nt(a, b):
    return lax.dot_general(a, b, (((1,), (1,)), ((), ())), preferred_element_type=F32)


def _dot_tn(a, b):
    return lax.dot_general(a, b, (((0,), (0,)), ((), ())), preferred_element_type=F32)


def _sigmoid(x):
    return 0.5 * jnp.tanh(0.5 * x) + 0.5


def _rms(x, g, eps):
    return x * lax.rsqrt(jnp.mean(x * x, axis=-1, keepdims=True) + eps) * g


def _split3(x):
    hi = x.astype(BF16)
    r1 = x - hi.astype(F32)
    mid = r1.astype(BF16)
    lo = (r1 - mid.astype(F32)).astype(BF16)
    return hi, mid, lo


def _once(shape):
    n = len(shape)
    return pl.BlockSpec(shape, lambda *_: (0,) * n, pipeline_mode=pl.Buffered(1))


def _params(*semantics):
    return pltpu.CompilerParams(dimension_semantics=semantics, vmem_limit_bytes=VMEM_LIMIT)


def _in_proj_kernel(x_ref, g_ref, w_ref, wgk_ref, z_ref, zgk_ref, *, nc):
    h = _rms(x_ref[...], g_ref[...], RMS_EPS).astype(BF16)
    for c in range(0, Z_DIM, nc):
        z_ref[:, c:c + nc] = _dot(h, w_ref[:, c:c + nc]).astype(BF16)
    zgk_ref[...] = _dot(h, wgk_ref[...])


def _in_proj(x2, g, w_main, w_gk, *, tm=512, nc=1024):
    T = x2.shape[0]
    return pl.pallas_call(
        functools.partial(_in_proj_kernel, nc=nc),
        out_shape=(jax.ShapeDtypeStruct((T, Z_DIM), BF16), jax.ShapeDtypeStruct((T, GK_PAD), F32)),
        grid=(T // tm,),
        in_specs=[
            pl.BlockSpec((tm, D_MODEL), lambda i: (i, 0)),
            _once((1, D_MODEL)),
            _once((D_MODEL, Z_DIM)),
            _once((D_MODEL, GK_PAD)),
        ],
        out_specs=(pl.BlockSpec((tm, Z_DIM), lambda i: (i, 0)), pl.BlockSpec((tm, GK_PAD), lambda i: (i, 0))),
        compiler_params=_params("arbitrary"),
        name="in_proj",
    )(x2, g, w_main, w_gk)


def _mixer_kernel(z_ref, zgk_ref, x_ref, wgk_ref, bgk_ref, gnorm_ref, convw_ref, convb_ref, lng_ref, lnb_ref,
                  wco_ref, wpool_ref, pscale_ref, wout_ref, o_ref,
                  state_ref, ubuf_ref, pbuf_ref, ygla_ref, yconv_ref, *, ts):
    s = pl.program_id(1)
    C = GLA_CHUNK

    @pl.when(s == 0)
    def _():
        state_ref[...] = jnp.zeros_like(state_ref)
        ubuf_ref[0:CONV_HALO, :] = jnp.zeros((CONV_HALO, D_MODEL), F32)
        pbuf_ref[0:POOL_HALO, :] = jnp.zeros((POOL_HALO, D_MODEL), BF16)

    gk_pre = _dot(zgk_ref[...].astype(BF16), wgk_ref[...]) + bgk_ref[...]
    gk = (jnp.minimum(gk_pre, 0.0) - jnp.log(1.0 + jnp.exp(-jnp.abs(gk_pre)))) * (1.0 / GLA_GATE_NORMALIZER)
    row = lax.broadcasted_iota(jnp.int32, (ts, ts), 0)
    col = lax.broadcasted_iota(jnp.int32, (ts, ts), 1)
    same_chunk_causal = (jnp.right_shift(row, 6) == jnp.right_shift(col, 6)) & (col <= row)
    tri = same_chunk_causal.astype(BF16)
    g_hi, g_mid, g_lo = _split3(gk)
    b_all = _dot(tri, g_hi) + _dot(tri, g_mid) + _dot(tri, g_lo)
    nchunk = ts // C

    def per_chunk_row(r):
        return jnp.concatenate(
            [jnp.broadcast_to(b_all[c * C + r:c * C + r + 1, :], (C, GLA_KEY_DIM)) for c in range(nchunk)], axis=0)

    b_mid = per_chunk_row(C // 2 - 1)
    b_last = per_chunk_row(C - 1)
    q = z_ref[:, Z_Q:Z_Q + GLA_KEY_DIM].astype(F32) * (GLA_HEAD_K ** -0.5)
    k = z_ref[:, Z_K:Z_K + GLA_KEY_DIM].astype(F32)
    qa = (q * jnp.exp(b_all - b_mid)).astype(BF16)
    ka = (k * jnp.exp(b_mid - b_all)).astype(BF16)
    qd = (q * jnp.exp(b_all)).astype(BF16)
    kd = (k * jnp.exp(b_last - b_all)).astype(BF16)
    e_dec = [jnp.exp(b_all[c * C + C - 1:c * C + C, :]) for c in range(nchunk)]
    gnorm = gnorm_ref[...]
    for h in range(GLA_HEADS):
        ks = slice(h * GLA_HEAD_K, (h + 1) * GLA_HEAD_K)
        v = z_ref[:, Z_V + h * GLA_HEAD_V:Z_V + (h + 1) * GLA_HEAD_V]
        attn = jnp.where(same_chunk_causal, _dot_nt(qa[:, ks], ka[:, ks]), 0.0).astype(BF16)
        o = _dot(attn, v)
        st = state_ref[h]
        o_inter = []
        for c in range(nchunk):
            rs = slice(c * C, (c + 1) * C)
            o_inter.append(_dot_nt(qd[rs, ks], st.astype(BF16)))
            st = st * e_dec[c][:, ks] + _dot_tn(v[rs, :], kd[rs, ks])
        state_ref[h] = st
        o = o + jnp.concatenate(o_inter, axis=0)
        o = o * lax.rsqrt(jnp.mean(o * o, axis=-1, keepdims=True) + GLA_NORM_EPS) * gnorm
        gi = z_ref[:, Z_G + h * GLA_HEAD_V:Z_G + (h + 1) * GLA_HEAD_V].astype(F32)
        ygla_ref[:, h * GLA_HEAD_V:(h + 1) * GLA_HEAD_V] = o * (gi * _sigmoid(gi))

    ca = z_ref[:, Z_CA:Z_CA + D_MODEL].astype(F32)
    cg = z_ref[:, Z_CG:Z_CG + D_MODEL].astype(F32)
    ubuf_ref[CONV_HALO:CONV_HALO + ts, :] = ca * _sigmoid(cg)
    RC = CONV_RC

    def conv_rows(i, carry):
        base = pl.multiple_of(i * RC, RC)
        for cc in range(D_MODEL // LANES):
            cs = slice(cc * LANES, (cc + 1) * LANES)
            acc = None
            for r in range(8):
                n_rows = RC if r == 0 else RC + 8
                part = None
                for a in range(5):
                    o = 8 * a + r
                    if o < 2 or o > CONV_HALO:
                        continue
                    kk = o - 2
                    term = convw_ref[kk:kk + 1, cs] * ubuf_ref[pl.ds(base + 8 * a, n_rows), cs]
                    part = term if part is None else part + term
                part = part[r:r + RC, :]
                acc = part if acc is None else acc + part
            yconv_ref[pl.ds(base, RC), cs] = acc + convb_ref[:, cs]
        return carry

    lax.fori_loop(0, ts // RC, conv_rows, 0)
    ubuf_ref[0:CONV_HALO, :] = ubuf_ref[ts:ts + CONV_HALO, :]
    yc = yconv_ref[...]
    mu = jnp.mean(yc, axis=-1, keepdims=True)
    ycc = yc - mu
    var = jnp.mean(ycc * ycc, axis=-1, keepdims=True)
    yn = ycc * lax.rsqrt(var + LN_EPS) * lng_ref[...] + lnb_ref[...]
    y_conv = _dot((yn * _sigmoid(yn)).astype(BF16), wco_ref[...])

    pbuf_ref[POOL_HALO:POOL_HALO + ts, :] = z_ref[:, Z_POOL:Z_POOL + D_MODEL]
    tpos = s * ts + lax.broadcasted_iota(jnp.int32, (ts, 1), 0)
    prow = lax.broadcasted_iota(jnp.int32, (ts, POOL_HALO + ts), 0) + POOL_HALO
    pcol = lax.broadcasted_iota(jnp.int32, (ts, POOL_HALO + ts), 1)
    y_pool = []
    for gi_, w in enumerate(POOL_WINDOWS):
        cs = slice(gi_ * POOL_GROUP_CH, (gi_ + 1) * POOL_GROUP_CH)
        band = ((pcol <= prow) & (pcol > prow - w)).astype(BF16)
        wsum = _dot(band, pbuf_ref[:, cs])
        cnt = jnp.minimum(tpos + 1, w).astype(F32)
        d = wsum / cnt - z_ref[:, Z_POOL + gi_ * POOL_GROUP_CH:Z_POOL + (gi_ + 1) * POOL_GROUP_CH].astype(F32)
        y_pool.append(_dot(d.astype(BF16), wpool_ref[gi_]) * pscale_ref[:, cs])
    pbuf_ref[0:POOL_HALO, :] = pbuf_ref[ts:ts + POOL_HALO, :]

    mix_parts = []
    for gi_ in range(4):
        cs = slice(gi_ * POOL_GROUP_CH, (gi_ + 1) * POOL_GROUP_CH)
        g0 = _sigmoid(z_ref[:, Z_GATE + gi_ * 256:Z_GATE + (gi_ + 1) * 256])
        g1 = _sigmoid(z_ref[:, Z_GATE + D_MODEL + gi_ * 256:Z_GATE + D_MODEL + (gi_ + 1) * 256])
        g2 = _sigmoid(z_ref[:, Z_GATE + 2 * D_MODEL + gi_ * 256:Z_GATE + 2 * D_MODEL + (gi_ + 1) * 256])
        m = g0 * ygla_ref[:, cs].astype(BF16) + g1 * y_conv[:, cs].astype(BF16) + g2 * y_pool[gi_].astype(BF16)
        mix_parts.append(m)
    o_ref[...] = x_ref[...] + _dot(jnp.concatenate(mix_parts, axis=1), wout_ref[...])


def _mixer(z, zgk, x2, wgk, bgk, gnorm, convw, convb, lng, lnb, wco, wpool, pscale, wout, *, B, S, ts=256):
    T = B * S
    nt = S // ts
    rowmap = lambda b, s: (b * nt + s, 0)
    return pl.pallas_call(
        functools.partial(_mixer_kernel, ts=ts),
        out_shape=jax.ShapeDtypeStruct((T, D_MODEL), F32),
        grid=(B, nt),
        in_specs=[
            pl.BlockSpec((ts, Z_DIM), rowmap),
            pl.BlockSpec((ts, GK_PAD), rowmap),
            pl.BlockSpec((ts, D_MODEL), rowmap),
            _once((GK_PAD, GLA_KEY_DIM)),
            _once((1, GLA_KEY_DIM)),
            _once((1, GLA_HEAD_V)),
            _once((CONV_WIDTH, D_MODEL)),
            _once((1, D_MODEL)),
            _once((1, D_MODEL)),
            _once((1, D_MODEL)),
            _once((D_MODEL, D_MODEL)),
            _once((4, POOL_GROUP_CH, POOL_GROUP_CH)),
            _once((1, D_MODEL)),
            _once((D_MODEL, D_MODEL)),
        ],
        out_specs=pl.BlockSpec((ts, D_MODEL), rowmap),
        scratch_shapes=[
            pltpu.VMEM((GLA_HEADS, GLA_HEAD_V, GLA_HEAD_K), F32),
            pltpu.VMEM((CONV_HALO + ts, D_MODEL), F32),
            pltpu.VMEM((POOL_HALO + ts, D_MODEL), BF16),
            pltpu.VMEM((ts, D_MODEL), F32),
            pltpu.VMEM((ts, D_MODEL), F32),
        ],
        compiler_params=_params("arbitrary", "arbitrary"),
        name="token_mixer",
    )(z, zgk, x2, wgk, bgk, gnorm, convw, convb, lng, lnb, wco, wpool, pscale, wout)


def _ple(x1, p, gple, wpg, wpp):
    hp = _rms(x1, gple, RMS_EPS).astype(BF16)
    return x1 + _sigmoid(_dot(hp, wpg)) * _dot(p.astype(BF16), wpp)


def _ffn_ple_kernel(x_ref, p_ref, gffn_ref, wgu_ref, wd_ref, gple_ref, wpg_ref, wpp_ref, o_ref):
    x = x_ref[...]
    h = _rms(x, gffn_ref[...], RMS_EPS).astype(BF16)
    F = wd_ref.shape[0]
    gu = _dot(h, wgu_ref[...])
    g = gu[:, :F]
    u = gu[:, F:]
    y = x + _dot((g * _sigmoid(g) * u).astype(BF16), wd_ref[...])
    o_ref[...] = _ple(y, p_ref[...], gple_ref[...], wpg_ref[...], wpp_ref[...])


def _ffn_ple(x2, p2, gffn, wgu, wd, gple, wpg, wpp, *, tm=512):
    T = x2.shape[0]
    F = wd.shape[0]
    return pl.pallas_call(
        _ffn_ple_kernel,
        out_shape=jax.ShapeDtypeStruct((T, D_MODEL), F32),
        grid=(T // tm,),
        in_specs=[
            pl.BlockSpec((tm, D_MODEL), lambda i: (i, 0)),
            pl.BlockSpec((tm, PLE_DIM), lambda i: (i, 0)),
            _once((1, D_MODEL)),
            _once((D_MODEL, 2 * F)),
            _once((F, D_MODEL)),
            _once((1, D_MODEL)),
            _once((D_MODEL, D_MODEL)),
            _once((PLE_DIM, D_MODEL)),
        ],
        out_specs=pl.BlockSpec((tm, D_MODEL), lambda i: (i, 0)),
        compiler_params=_params("arbitrary"),
        name="ffn_ple",
    )(x2, p2, gffn, wgu, wd, gple, wpg, wpp)


def _router_kernel(x_ref, g_ref, whi_ref, wlo_ref, br_ref, h_ref, ri_ref, rw_ref, cnt_ref, run_ref, *, tm):
    i = pl.program_id(0)

    @pl.when(i == 0)
    def _():
        run_ref[...] = jnp.zeros_like(run_ref)

    h = _rms(x_ref[...], g_ref[...], RMS_EPS)
    h_ref[...] = h
    hh = h.astype(BF16)
    hl = (h - hh.astype(F32)).astype(BF16)
    logits = _dot(hh, whi_ref[...]) + _dot(hh, wlo_ref[...]) + _dot(hl, whi_ref[...]) + br_ref[...]
    lane = lax.broadcasted_iota(jnp.int32, (tm, LANES), 1)
    m1 = jnp.max(logits, axis=-1, keepdims=True)
    i1 = jnp.min(jnp.where(logits == m1, lane, LANES), axis=-1, keepdims=True)
    l2 = jnp.where(lane == i1, -jnp.inf, logits)
    m2 = jnp.max(l2, axis=-1, keepdims=True)
    i2 = jnp.min(jnp.where(l2 == m2, lane, LANES), axis=-1, keepdims=True)
    e = jnp.exp(m2 - m1)
    w1 = 1.0 / (1.0 + e)
    w2 = e / (1.0 + e)
    sel1 = lane == i1
    sel2 = lane == i2
    onehot = (sel1 | sel2).astype(BF16)
    row = lax.broadcasted_iota(jnp.int32, (tm, tm), 0)
    col = lax.broadcasted_iota(jnp.int32, (tm, tm), 1)
    before = _dot((col < row).astype(BF16), onehot) + run_ref[...]
    pos1 = jnp.sum(jnp.where(sel1, before, 0.0), axis=-1, keepdims=True).astype(jnp.int32)
    pos2 = jnp.sum(jnp.where(sel2, before, 0.0), axis=-1, keepdims=True).astype(jnp.int32)
    run = run_ref[...] + jnp.sum(onehot.astype(F32), axis=0, keepdims=True)
    run_ref[...] = run
    cnt_ref[...] = run
    ri_ref[...] = jnp.where(lane == 0, i1, jnp.where(lane == 1, i2, jnp.where(lane == 2, pos1, jnp.where(lane == 3, pos2, 0))))
    rw_ref[...] = jnp.where(lane == 0, w1, jnp.where(lane == 1, w2, 0.0))


def _router(x2, g, whi, wlo, br, *, tm=512):
    T = x2.shape[0]
    return pl.pallas_call(
        functools.partial(_router_kernel, tm=tm),
        out_shape=(
            jax.ShapeDtypeStruct((T, D_MODEL), F32),
            jax.ShapeDtypeStruct((T, LANES), jnp.int32),
            jax.ShapeDtypeStruct((T, LANES), F32),
            jax.ShapeDtypeStruct((1, LANES), F32),
        ),
        grid=(T // tm,),
        in_specs=[
            pl.BlockSpec((tm, D_MODEL), lambda i: (i, 0)),
            _once((1, D_MODEL)),
            _once((D_MODEL, LANES)),
            _once((D_MODEL, LANES)),
            _once((1, LANES)),
        ],
        out_specs=(
            pl.BlockSpec((tm, D_MODEL), lambda i: (i, 0)),
            pl.BlockSpec((tm, LANES), lambda i: (i, 0)),
            pl.BlockSpec((tm, LANES), lambda i: (i, 0)),
            pl.BlockSpec((1, LANES), lambda i: (0, 0)),
        ),
        scratch_shapes=[pltpu.VMEM((1, LANES), F32)],
        compiler_params=_params("arbitrary"),
        name="router",
    )(x2, g, whi, wlo, br)


def _row_copy(src_ref, src_row, dst_ref, dst_row, sem):
    return pltpu.make_async_copy(src_ref.at[pl.ds(src_row, 1)], dst_ref.at[pl.ds(dst_row, 1)], sem)


def _dispatch_kernel(pe_ref, nu_ref, dest_ref, h_ref, xs_ref, zero_ref, sem, *, td, tmm, n_blocks):
    @pl.when(pl.program_id(0) == 0)
    def _():
        zero_ref[...] = jnp.zeros_like(zero_ref)

        def fill(start):
            return pltpu.make_async_copy(zero_ref, xs_ref.at[pl.ds(pl.multiple_of(start, tmm), tmm)], sem)

        for e in range(N_EXPERTS):
            prev_end = pe_ref[e - 1] if e > 0 else 0

            @pl.when(pe_ref[e] > prev_end)
            def _():
                fill(pe_ref[e] - tmm).start()
                fill(pe_ref[e] - tmm).wait()

        for j in range(N_EXPERTS):
            @pl.when(n_blocks - 1 - j >= nu_ref[0])
            def _():
                fill((n_blocks - 1 - j) * tmm).start()
                fill((n_blocks - 1 - j) * tmm).wait()

    for r in range(td):
        _row_copy(h_ref, r, xs_ref, dest_ref[2 * r], sem).start(priority=0)
        _row_copy(h_ref, r, xs_ref, dest_ref[2 * r + 1], sem).start(priority=1)
    for _ in range(2):
        pltpu.make_async_copy(h_ref, xs_ref.at[pl.ds(0, td)], sem).wait()


def _dispatch(pad_end, n_used, dest_flat, h, *, n_blocks, tmm, td=256):
    T = h.shape[0]
    return pl.pallas_call(
        functools.partial(_dispatch_kernel, td=td, tmm=tmm, n_blocks=n_blocks),
        out_shape=jax.ShapeDtypeStruct((n_blocks * tmm, D_MODEL), F32),
        grid_spec=pltpu.PrefetchScalarGridSpec(
            num_scalar_prefetch=2,
            grid=(T // td,),
            in_specs=[
                pl.BlockSpec((2 * td,), lambda i, pe, nu: (i,), memory_space=pltpu.SMEM),
                pl.BlockSpec((td, D_MODEL), lambda i, pe, nu: (i, 0)),
            ],
            out_specs=pl.BlockSpec(memory_space=pl.ANY),
            scratch_shapes=[pltpu.VMEM((tmm, D_MODEL), F32), pltpu.SemaphoreType.DMA],
        ),
        compiler_params=_params("arbitrary"),
        name="moe_dispatch",
    )(pad_end, n_used, dest_flat, h)


def _gmm_kernel(be_ref, nu_ref, xs_ref, wgu_ref, wd_ref, o_ref):
    i = pl.program_id(0)
    f = pl.program_id(1)
    tf = wd_ref.shape[1]

    @pl.when(i < nu_ref[0])
    def _():
        x = xs_ref[...].astype(BF16)
        gu = _dot(x, wgu_ref[0])
        g = gu[:, :tf]
        u = gu[:, tf:]
        y = _dot((g * _sigmoid(g) * u).astype(BF16), wd_ref[0])

        @pl.when(f == 0)
        def _():
            o_ref[...] = y

        @pl.when(f > 0)
        def _():
            o_ref[...] += y

    @pl.when((i >= nu_ref[0]) & (f == 0))
    def _():
        o_ref[...] = jnp.zeros_like(o_ref)


def _gmm(block_e, n_used, xs, wgu, wd, *, tmm, tf):
    n_rows = xs.shape[0]
    n_blocks = n_rows // tmm
    F = wd.shape[1]
    nf = F // tf

    def row_map(i, f, be, nu):
        return (jnp.minimum(i, nu[0] - 1), 0)

    def f_idx(i, f, nu):
        return jnp.where(i < nu[0], f, nf - 1)

    return pl.pallas_call(
        _gmm_kernel,
        out_shape=jax.ShapeDtypeStruct((n_rows, D_MODEL), F32),
        grid_spec=pltpu.PrefetchScalarGridSpec(
            num_scalar_prefetch=2,
            grid=(n_blocks, nf),
            in_specs=[
                pl.BlockSpec((tmm, D_MODEL), row_map),
                pl.BlockSpec((1, D_MODEL, 2 * tf), lambda i, f, be, nu: (be[i], 0, f_idx(i, f, nu))),
                pl.BlockSpec((1, tf, D_MODEL), lambda i, f, be, nu: (be[i], f_idx(i, f, nu), 0)),
            ],
            out_specs=pl.BlockSpec((tmm, D_MODEL), lambda i, f, be, nu: (i, 0)),
        ),
        compiler_params=_params("arbitrary", "arbitrary"),
        name="moe_grouped_swiglu",
    )(block_e, n_used, xs, wgu, wd)


def _combine_ple_kernel(dest_ref, dest_next_ref, x_ref, rw_ref, p_ref, gple_ref, wpg_ref, wpp_ref, gfin_ref,
                        yb_ref, o_ref, buf_ref, sem, *, tp):
    i = pl.program_id(0)
    slot = lax.rem(i, 2)

    def gather(d_ref, sl):
        for r in range(tp):
            _row_copy(yb_ref, d_ref[2 * r], buf_ref.at[sl, 0], r, sem.at[sl]).start(priority=0)
            _row_copy(yb_ref, d_ref[2 * r + 1], buf_ref.at[sl, 1], r, sem.at[sl]).start(priority=1)

    def wait(sl):
        for k in range(2):
            pltpu.make_async_copy(yb_ref.at[pl.ds(0, tp)], buf_ref.at[sl, k], sem.at[sl]).wait()

    @pl.when(i == 0)
    def _():
        gather(dest_ref, 0)

    gather(dest_next_ref, 1 - slot)
    wait(slot)
    rw = rw_ref[...]
    x1 = x_ref[...] + rw[:, 0:1] * buf_ref[slot, 0] + rw[:, 1:2] * buf_ref[slot, 1]
    x2 = _ple(x1, p_ref[...], gple_ref[...], wpg_ref[...], wpp_ref[...])
    o_ref[...] = _rms(x2, gfin_ref[...], RMS_EPS)

    @pl.when(i == pl.num_programs(0) - 1)
    def _():
        wait(1 - slot)


def _combine_ple(dest_flat, x2, rw, p2, gple, wpg, wpp, gfin, yb, *, tp=256):
    T = x2.shape[0]
    n = T // tp
    return pl.pallas_call(
        functools.partial(_combine_ple_kernel, tp=tp),
        out_shape=jax.ShapeDtypeStruct((T, D_MODEL), F32),
        grid=(n,),
        in_specs=[
            pl.BlockSpec((2 * tp,), lambda i: (i,), memory_space=pltpu.SMEM),
            pl.BlockSpec((2 * tp,), lambda i: (jnp.minimum(i + 1, n - 1),), memory_space=pltpu.SMEM),
            pl.BlockSpec((tp, D_MODEL), lambda i: (i, 0)),
            pl.BlockSpec((tp, LANES), lambda i: (i, 0)),
            pl.BlockSpec((tp, PLE_DIM), lambda i: (i, 0)),
            _once((1, D_MODEL)),
            _once((D_MODEL, D_MODEL)),
            _once((PLE_DIM, D_MODEL)),
            _once((1, D_MODEL)),
            pl.BlockSpec(memory_space=pl.ANY),
        ],
        out_specs=pl.BlockSpec((tp, D_MODEL), lambda i: (i, 0)),
        scratch_shapes=[pltpu.VMEM((2, 2, tp, D_MODEL), F32), pltpu.SemaphoreType.DMA((2,))],
        compiler_params=_params("arbitrary"),
        name="moe_combine_ple",
    )(dest_flat, dest_flat, x2, rw, p2, gple, wpg, wpp, gfin, yb)


def _moe_ple_final(x2, p2, gffn, w_router, b_router, wg, wu, wd, gple, wpg, wpp, gfin, *, tmm=512, tf=1792):
    T = x2.shape[0]
    wr = jnp.zeros((D_MODEL, LANES), F32).at[:, :N_EXPERTS].set(w_router)
    whi = wr.astype(BF16)
    wlo = (wr - whi.astype(F32)).astype(BF16)
    br = jnp.full((1, LANES), NEG_BIG, F32).at[0, :N_EXPERTS].set(b_router)
    h, ri, rw, cnt = _router(x2, gffn, whi, wlo, br)
    counts = cnt[0, :N_EXPERTS].astype(jnp.int32)
    padded = (counts + tmm - 1) // tmm * tmm
    pad_end = jnp.cumsum(padded).astype(jnp.int32)
    pad_start = pad_end - padded
    e_sel = ri[:, 0:2, None] == jnp.arange(N_EXPERTS, dtype=jnp.int32)
    dest = jnp.sum(jnp.where(e_sel, pad_start, 0), axis=-1) + ri[:, 2:4]
    dest_flat = dest.reshape(2 * T).astype(jnp.int32)
    n_blocks = (2 * T) // tmm + N_EXPERTS
    block_start = jnp.arange(n_blocks, dtype=jnp.int32) * tmm
    block_e = jnp.minimum(jnp.sum(block_start[:, None] >= pad_end[None, :], axis=1), N_EXPERTS - 1).astype(jnp.int32)
    n_used = (pad_end[-1] // tmm).astype(jnp.int32).reshape(1)
    xs = _dispatch(pad_end, n_used, dest_flat, h, n_blocks=n_blocks, tmm=tmm)
    F = wg.shape[2]
    wgu = jnp.concatenate([w[:, :, c:c + tf].astype(BF16) for c in range(0, F, tf) for w in (wg, wu)], axis=2)
    yb = _gmm(block_e, n_used, xs, wgu, wd.astype(BF16), tmm=tmm, tf=tf)
    return _combine_ple(dest_flat, x2, rw, p2, gple, wpg, wpp, gfin, yb)


def kernel(x, p, g_mix, w_in, w_gk_up, b_gk, g_gla_norm, conv_w, conv_b, ln_conv_g, ln_conv_b, w_conv_out, w_pool, pool_scale, w_out, g_ffn, w_ffn_gate, w_ffn_up, w_ffn_down, w_router, b_router, w_moe_gate, w_moe_up, w_moe_down, g_ple, w_ple_gate, w_ple_proj, g_final):
    B, S, _ = x.shape
    T = B * S
    depth = w_in.shape[0]
    assert depth == 2 and x.shape[2] == D_MODEL, "dense layer 0, routed layer 1 with the final norm fused"
    x2 = x.reshape(T, D_MODEL)
    row = lambda a: a.reshape(1, -1)
    for i in range(depth):
        wi = w_in[i]
        w_main = jnp.concatenate([wi[:, :3072], wi[:, 3088:]], axis=1).astype(BF16)
        w_gk = jnp.zeros((D_MODEL, GK_PAD), F32).at[:, :GLA_GATE_RANK].set(wi[:, 3072:3088]).astype(BF16)
        w_gk_up_p = jnp.zeros((GK_PAD, GLA_KEY_DIM), F32).at[:GLA_GATE_RANK].set(w_gk_up[i]).astype(BF16)
        z, zgk = _in_proj(x2, row(g_mix[i]), w_main, w_gk)
        x2 = _mixer(z, zgk, x2, w_gk_up_p, row(b_gk[i]), row(g_gla_norm[i]), conv_w[i], row(conv_b[i]),
                    row(ln_conv_g[i]), row(ln_conv_b[i]), w_conv_out[i].astype(BF16), w_pool[i].astype(BF16),
                    row(pool_scale[i]), w_out[i].astype(BF16), B=B, S=S)
        p2 = p[i].reshape(T, PLE_DIM)
        wpg = w_ple_gate[i].astype(BF16)
        wpp = w_ple_proj[i].astype(BF16)
        if i % 2 == 0:
            j = i // 2
            wgu = jnp.concatenate([w_ffn_gate[j].astype(BF16), w_ffn_up[j].astype(BF16)], axis=1)
            x2 = _ffn_ple(x2, p2, row(g_ffn[i]), wgu, w_ffn_down[j].astype(BF16), row(g_ple[i]), wpg, wpp)
        else:
            j = i // 2
            x2 = _moe_ple_final(x2, p2, row(g_ffn[i]), w_router[j], b_router[j], w_moe_gate[j], w_moe_up[j],
                                w_moe_down[j], row(g_ple[i]), wpg, wpp, row(g_final))
    return x2.reshape(B, S, D_MODEL)
```

```python
import functools

import jax
import jax.numpy as jnp
from jax import lax
from jax.experimental import pallas as pl
from jax.experimental.pallas import tpu as pltpu

F32 = jnp.float32
BF16 = jnp.bfloat16

D_MODEL = 1024
GLA_HEADS = 4
GLA_KEY_DIM = 512
GLA_HEAD_K = 128
GLA_HEAD_V = 256
GLA_GATE_RANK = 16
GLA_GATE_NORMALIZER = 16.0
GLA_CHUNK = 64
GLA_NORM_EPS = 1e-5
CONV_WIDTH = 31
LN_EPS = 1e-5
POOL_WINDOWS = (2, 4, 8, 16)
POOL_GROUP_CH = 256
N_EXPERTS = 8
PLE_DIM = 256
RMS_EPS = 1e-6

Z_Q, Z_K, Z_V, Z_G, Z_CA, Z_CG, Z_POOL, Z_GATE = 0, 512, 1024, 2048, 3072, 4096, 5120, 6144
Z_DIM = 9216
GK_PAD = 128
LANES = 128

CONV_HALO = 32
POOL_HALO = 128
CONV_RC = 32
VMEM_LIMIT = 56 * 1024 * 1024

NEG_BIG = -1e30
MXU_PASS_COLS = 512


def _dot(a, b):
    return jnp.dot(a, b, preferred_element_type=F32)


def _dot_nt(a, b):
    return lax.dot_general(a, b, (((1,), (1,)), ((), ())), preferred_element_type=F32)


def _dot_tn(a, b):
    return lax.dot_general(a, b, (((0,), (0,)), ((), ())), preferred_element_type=F32)


def _sigmoid(x):
    return 0.5 * jnp.tanh(0.5 * x) + 0.5


def _rms(x, g, eps):
    return x * lax.rsqrt(jnp.mean(x * x, axis=-1, keepdims=True) + eps) * g


def _split3(x):
    hi = x.astype(BF16)
    r1 = x - hi.astype(F32)
    mid = r1.astype(BF16)
    lo = (r1 - mid.astype(F32)).astype(BF16)
    return hi, mid, lo


def _once(shape):
    n = len(shape)
    return pl.BlockSpec(shape, lambda *_: (0,) * n, pipeline_mode=pl.Buffered(1))


def _params(*semantics):
    return pltpu.CompilerParams(dimension_semantics=semantics, vmem_limit_bytes=VMEM_LIMIT)


def _in_proj_kernel(x_ref, g_ref, w_ref, wgk_ref, z_ref, zgk_ref, *, nc):
    h = _rms(x_ref[...], g_ref[...], RMS_EPS).astype(BF16)
    for c in range(0, Z_DIM, nc):
        z_ref[:, c:c + nc] = _dot(h, w_ref[:, c:c + nc]).astype(BF16)
    zgk_ref[...] = _dot(h, wgk_ref[...])


def _in_proj(x2, g, w_main, w_gk, *, tm=512, nc=1024):
    T = x2.shape[0]
    return pl.pallas_call(
        functools.partial(_in_proj_kernel, nc=nc),
        out_shape=(jax.ShapeDtypeStruct((T, Z_DIM), BF16), jax.ShapeDtypeStruct((T, GK_PAD), F32)),
        grid=(T // tm,),
        in_specs=[
            pl.BlockSpec((tm, D_MODEL), lambda i: (i, 0)),
            _once((1, D_MODEL)),
            _once((D_MODEL, Z_DIM)),
            _once((D_MODEL, GK_PAD)),
        ],
        out_specs=(pl.BlockSpec((tm, Z_DIM), lambda i: (i, 0)), pl.BlockSpec((tm, GK_PAD), lambda i: (i, 0))),
        compiler_params=_params("arbitrary"),
        name="in_proj",
    )(x2, g, w_main, w_gk)


def _mixer_kernel(z_ref, zgk_ref, x_ref, wgk_ref, bgk_ref, gnorm_ref, convw_ref, convb_ref, lng_ref, lnb_ref,
                  wco_ref, wpool_ref, pscale_ref, wout_ref, o_ref,
                  state_ref, ubuf_ref, pbuf_ref, ygla_ref, yconv_ref, *, ts):
    s = pl.program_id(1)
    C = GLA_CHUNK

    @pl.when(s == 0)
    def _():
        state_ref[...] = jnp.zeros_like(state_ref)
        ubuf_ref[0:CONV_HALO, :] = jnp.zeros((CONV_HALO, D_MODEL), F32)
        pbuf_ref[0:POOL_HALO, :] = jnp.zeros((POOL_HALO, D_MODEL), BF16)

    gk_pre = _dot(zgk_ref[...].astype(BF16), wgk_ref[...]) + bgk_ref[...]
    gk = (jnp.minimum(gk_pre, 0.0) - jnp.log(1.0 + jnp.exp(-jnp.abs(gk_pre)))) * (1.0 / GLA_GATE_NORMALIZER)
    row = lax.broadcasted_iota(jnp.int32, (ts, ts), 0)
    col = lax.broadcasted_iota(jnp.int32, (ts, ts), 1)
    same_chunk_causal = (jnp.right_shift(row, 6) == jnp.right_shift(col, 6)) & (col <= row)
    tri = same_chunk_causal.astype(BF16)
    g_hi, g_mid, g_lo = _split3(gk)
    b_all = _dot(tri, g_hi) + _dot(tri, g_mid) + _dot(tri, g_lo)
    nchunk = ts // C

    def per_chunk_row(r):
        return jnp.concatenate(
            [jnp.broadcast_to(b_all[c * C + r:c * C + r + 1, :], (C, GLA_KEY_DIM)) for c in range(nchunk)], axis=0)

    b_mid = per_chunk_row(C // 2 - 1)
    b_last = per_chunk_row(C - 1)
    q = z_ref[:, Z_Q:Z_Q + GLA_KEY_DIM].astype(F32) * (GLA_HEAD_K ** -0.5)
    k = z_ref[:, Z_K:Z_K + GLA_KEY_DIM].astype(F32)
    qa = (q * jnp.exp(b_all - b_mid)).astype(BF16)
    ka = (k * jnp.exp(b_mid - b_all)).astype(BF16)
    qd = (q * jnp.exp(b_all)).astype(BF16)
    kd = (k * jnp.exp(b_last - b_all)).astype(BF16)
    e_dec = [jnp.exp(b_all[c * C + C - 1:c * C + C, :]) for c in range(nchunk)]
    gnorm = gnorm_ref[...]
    for h in range(GLA_HEADS):
        ks = slice(h * GLA_HEAD_K, (h + 1) * GLA_HEAD_K)
        v = z_ref[:, Z_V + h * GLA_HEAD_V:Z_V + (h + 1) * GLA_HEAD_V]
        attn = jnp.where(same_chunk_causal, _dot_nt(qa[:, ks], ka[:, ks]), 0.0).astype(BF16)
        o = _dot(attn, v)
        st = state_ref[h]
        o_inter = []
        for c in range(nchunk):
            rs = slice(c * C, (c + 1) * C)
            o_inter.append(_dot_nt(qd[rs, ks], st.astype(BF16)))
            st = st * e_dec[c][:, ks] + _dot_tn(v[rs, :], kd[rs, ks])
        state_ref[h] = st
        o = o + jnp.concatenate(o_inter, axis=0)
        o = o * lax.rsqrt(jnp.mean(o * o, axis=-1, keepdims=True) + GLA_NORM_EPS) * gnorm
        gi = z_ref[:, Z_G + h * GLA_HEAD_V:Z_G + (h + 1) * GLA_HEAD_V].astype(F32)
        ygla_ref[:, h * GLA_HEAD_V:(h + 1) * GLA_HEAD_V] = o * (gi * _sigmoid(gi))

    ca = z_ref[:, Z_CA:Z_CA + D_MODEL].astype(F32)
    cg = z_ref[:, Z_CG:Z_CG + D_MODEL].astype(F32)
    ubuf_ref[CONV_HALO:CONV_HALO + ts, :] = ca * _sigmoid(cg)
    RC = CONV_RC

    def conv_rows(i, carry):
        base = pl.multiple_of(i * RC, RC)
        for cc in range(D_MODEL // LANES):
            cs = slice(cc * LANES, (cc + 1) * LANES)
            acc = None
            for r in range(8):
                n_rows = RC if r == 0 else RC + 8
                part = None
                for a in range(5):
                    o = 8 * a + r
                    if o < 2 or o > CONV_HALO:
                        continue
                    kk = o - 2
                    term = convw_ref[kk:kk + 1, cs] * ubuf_ref[pl.ds(base + 8 * a, n_rows), cs]
                    part = term if part is None else part + term
                part = part[r:r + RC, :]
                acc = part if acc is None else acc + part
            yconv_ref[pl.ds(base, RC), cs] = acc + convb_ref[:, cs]
        return carry

    lax.fori_loop(0, ts // RC, conv_rows, 0)
    ubuf_ref[0:CONV_HALO, :] = ubuf_ref[ts:ts + CONV_HALO, :]
    yc = yconv_ref[...]
    mu = jnp.mean(yc, axis=-1, keepdims=True)
    ycc = yc - mu
    var = jnp.mean(ycc * ycc, axis=-1, keepdims=True)
    yn = ycc * lax.rsqrt(var + LN_EPS) * lng_ref[...] + lnb_ref[...]
    y_conv = _dot((yn * _sigmoid(yn)).astype(BF16), wco_ref[...])

    pbuf_ref[POOL_HALO:POOL_HALO + ts, :] = z_ref[:, Z_POOL:Z_POOL + D_MODEL]
    tpos = s * ts + lax.broadcasted_iota(jnp.int32, (ts, 1), 0)
    prow = lax.broadcasted_iota(jnp.int32, (ts, POOL_HALO + ts), 0) + POOL_HALO
    pcol = lax.broadcasted_iota(jnp.int32, (ts, POOL_HALO + ts), 1)
    y_pool = []
    for gi_, w in enumerate(POOL_WINDOWS):
        cs = slice(gi_ * POOL_GROUP_CH, (gi_ + 1) * POOL_GROUP_CH)
        band = ((pcol <= prow) & (pcol > prow - w)).astype(BF16)
        wsum = _dot(band, pbuf_ref[:, cs])
        cnt = jnp.minimum(tpos + 1, w).astype(F32)
        d = wsum / cnt - z_ref[:, Z_POOL + gi_ * POOL_GROUP_CH:Z_POOL + (gi_ + 1) * POOL_GROUP_CH].astype(F32)
        y_pool.append(_dot(d.astype(BF16), wpool_ref[gi_]) * pscale_ref[:, cs])
    pbuf_ref[0:POOL_HALO, :] = pbuf_ref[ts:ts + POOL_HALO, :]

    mix_parts = []
    for gi_ in range(4):
        cs = slice(gi_ * POOL_GROUP_CH, (gi_ + 1) * POOL_GROUP_CH)
        g0 = _sigmoid(z_ref[:, Z_GATE + gi_ * 256:Z_GATE + (gi_ + 1) * 256])
        g1 = _sigmoid(z_ref[:, Z_GATE + D_MODEL + gi_ * 256:Z_GATE + D_MODEL + (gi_ + 1) * 256])
        g2 = _sigmoid(z_ref[:, Z_GATE + 2 * D_MODEL + gi_ * 256:Z_GATE + 2 * D_MODEL + (gi_ + 1) * 256])
        m = g0 * ygla_ref[:, cs].astype(BF16) + g1 * y_conv[:, cs].astype(BF16) + g2 * y_pool[gi_].astype(BF16)
        mix_parts.append(m)
    o_ref[...] = x_ref[...] + _dot(jnp.concatenate(mix_parts, axis=1), wout_ref[...])


def _mixer(z, zgk, x2, wgk, bgk, gnorm, convw, convb, lng, lnb, wco, wpool, pscale, wout, *, B, S, ts=256):
    T = B * S
    nt = S // ts
    rowmap = lambda b, s: (b * nt + s, 0)
    return pl.pallas_call(
        functools.partial(_mixer_kernel, ts=ts),
        out_shape=jax.ShapeDtypeStruct((T, D_MODEL), F32),
        grid=(B, nt),
        in_specs=[
            pl.BlockSpec((ts, Z_DIM), rowmap),
            pl.BlockSpec((ts, GK_PAD), rowmap),
            pl.BlockSpec((ts, D_MODEL), rowmap),
            _once((GK_PAD, GLA_KEY_DIM)),
            _once((1, GLA_KEY_DIM)),
            _once((1, GLA_HEAD_V)),
            _once((CONV_WIDTH, D_MODEL)),
            _once((1, D_MODEL)),
            _once((1, D_MODEL)),
            _once((1, D_MODEL)),
            _once((D_MODEL, D_MODEL)),
            _once((4, POOL_GROUP_CH, POOL_GROUP_CH)),
            _once((1, D_MODEL)),
            _once((D_MODEL, D_MODEL)),
        ],
        out_specs=pl.BlockSpec((ts, D_MODEL), rowmap),
        scratch_shapes=[
            pltpu.VMEM((GLA_HEADS, GLA_HEAD_V, GLA_HEAD_K), F32),
            pltpu.VMEM((CONV_HALO + ts, D_MODEL), F32),
            pltpu.VMEM((POOL_HALO + ts, D_MODEL), BF16),
            pltpu.VMEM((ts, D_MODEL), F32),
            pltpu.VMEM((ts, D_MODEL), F32),
        ],
        compiler_params=_params("arbitrary", "arbitrary"),
        name="token_mixer",
    )(z, zgk, x2, wgk, bgk, gnorm, convw, convb, lng, lnb, wco, wpool, pscale, wout)


def _ple(x1, p, gple, wpg, wpp):
    hp = _rms(x1, gple, RMS_EPS).astype(BF16)
    return x1 + _sigmoid(_dot(hp, wpg)) * _dot(p.astype(BF16), wpp)


def _ffn_ple_kernel(x_ref, p_ref, gffn_ref, wgu_ref, wd_ref, gple_ref, wpg_ref, wpp_ref, o_ref):
    x = x_ref[...]
    h = _rms(x, gffn_ref[...], RMS_EPS).astype(BF16)
    F = wd_ref.shape[0]
    gu = _dot(h, wgu_ref[...])
    g = gu[:, :F]
    u = gu[:, F:]
    y = x + _dot((g * _sigmoid(g) * u).astype(BF16), wd_ref[...])
    o_ref[...] = _ple(y, p_ref[...], gple_ref[...], wpg_ref[...], wpp_ref[...])


def _ffn_ple(x2, p2, gffn, wgu, wd, gple, wpg, wpp, *, tm=512):
    T = x2.shape[0]
    F = wd.shape[0]
    return pl.pallas_call(
        _ffn_ple_kernel,
        out_shape=jax.ShapeDtypeStruct((T, D_MODEL), F32),
        grid=(T // tm,),
        in_specs=[
            pl.BlockSpec((tm, D_MODEL), lambda i: (i, 0)),
            pl.BlockSpec((tm, PLE_DIM), lambda i: (i, 0)),
            _once((1, D_MODEL)),
            _once((D_MODEL, 2 * F)),
            _once((F, D_MODEL)),
            _once((1, D_MODEL)),
            _once((D_MODEL, D_MODEL)),
            _once((PLE_DIM, D_MODEL)),
        ],
        out_specs=pl.BlockSpec((tm, D_MODEL), lambda i: (i, 0)),
        compiler_params=_params("arbitrary"),
        name="ffn_ple",
    )(x2, p2, gffn, wgu, wd, gple, wpg, wpp)


def _router_kernel(x_ref, g_ref, whi_ref, wlo_ref, br_ref, h_ref, ri_ref, rw_ref, cnt_ref, run_ref, *, tm):
    i = pl.program_id(0)

    @pl.when(i == 0)
    def _():
        run_ref[...] = jnp.zeros_like(run_ref)

    h = _rms(x_ref[...], g_ref[...], RMS_EPS)
    h_ref[...] = h
    hh = h.astype(BF16)
    hl = (h - hh.astype(F32)).astype(BF16)
    logits = _dot(hh, whi_ref[...]) + _dot(hh, wlo_ref[...]) + _dot(hl, whi_ref[...]) + br_ref[...]
    lane = lax.broadcasted_iota(jnp.int32, (tm, LANES), 1)
    m1 = jnp.max(logits, axis=-1, keepdims=True)
    i1 = jnp.min(jnp.where(logits == m1, lane, LANES), axis=-1, keepdims=True)
    l2 = jnp.where(lane == i1, -jnp.inf, logits)
    m2 = jnp.max(l2, axis=-1, keepdims=True)
    i2 = jnp.min(jnp.where(l2 == m2, lane, LANES), axis=-1, keepdims=True)
    e = jnp.exp(m2 - m1)
    w1 = 1.0 / (1.0 + e)
    w2 = e / (1.0 + e)
    sel1 = lane == i1
    sel2 = lane == i2
    onehot = (sel1 | sel2).astype(BF16)
    row = lax.broadcasted_iota(jnp.int32, (tm, tm), 0)
    col = lax.broadcasted_iota(jnp.int32, (tm, tm), 1)
    before = _dot((col < row).astype(BF16), onehot) + run_ref[...]
    pos1 = jnp.sum(jnp.where(sel1, before, 0.0), axis=-1, keepdims=True).astype(jnp.int32)
    pos2 = jnp.sum(jnp.where(sel2, before, 0.0), axis=-1, keepdims=True).astype(jnp.int32)
    run = run_ref[...] + jnp.sum(onehot.astype(F32), axis=0, keepdims=True)
    run_ref[...] = run
    cnt_ref[...] = run
    ri_ref[...] = jnp.where(lane == 0, i1, jnp.where(lane == 1, i2, jnp.where(lane == 2, pos1, jnp.where(lane == 3, pos2, 0))))
    rw_ref[...] = jnp.where(lane == 0, w1, jnp.where(lane == 1, w2, 0.0))


def _router(x2, g, whi, wlo, br, *, tm=512):
    T = x2.shape[0]
    return pl.pallas_call(
        functools.partial(_router_kernel, tm=tm),
        out_shape=(
            jax.ShapeDtypeStruct((T, D_MODEL), F32),
            jax.ShapeDtypeStruct((T, LANES), jnp.int32),
            jax.ShapeDtypeStruct((T, LANES), F32),
            jax.ShapeDtypeStruct((1, LANES), F32),
        ),
        grid=(T // tm,),
        in_specs=[
            pl.BlockSpec((tm, D_MODEL), lambda i: (i, 0)),
            _once((1, D_MODEL)),
            _once((D_MODEL, LANES)),
            _once((D_MODEL, LANES)),
            _once((1, LANES)),
        ],
        out_specs=(
            pl.BlockSpec((tm, D_MODEL), lambda i: (i, 0)),
            pl.BlockSpec((tm, LANES), lambda i: (i, 0)),
            pl.BlockSpec((tm, LANES), lambda i: (i, 0)),
            pl.BlockSpec((1, LANES), lambda i: (0, 0)),
        ),
        scratch_shapes=[pltpu.VMEM((1, LANES), F32)],
        compiler_params=_params("arbitrary"),
        name="router",
    )(x2, g, whi, wlo, br)


def _row_copy(src_ref, src_row, dst_ref, dst_row, sem):
    return pltpu.make_async_copy(src_ref.at[pl.ds(src_row, 1)], dst_ref.at[pl.ds(dst_row, 1)], sem)


def _dispatch_kernel(pe_ref, nu_ref, dest_ref, h_ref, xs_ref, zero_ref, sem, *, td, tmm, n_blocks):
    @pl.when(pl.program_id(0) == 0)
    def _():
        zero_ref[...] = jnp.zeros_like(zero_ref)

        def fill(start):
            return pltpu.make_async_copy(zero_ref, xs_ref.at[pl.ds(pl.multiple_of(start, tmm), tmm)], sem)

        for e in range(N_EXPERTS):
            prev_end = pe_ref[e - 1] if e > 0 else 0

            @pl.when(pe_ref[e] > prev_end)
            def _():
                fill(pe_ref[e] - tmm).start()
                fill(pe_ref[e] - tmm).wait()

        for j in range(N_EXPERTS):
            @pl.when(n_blocks - 1 - j >= nu_ref[0])
            def _():
                fill((n_blocks - 1 - j) * tmm).start()
                fill((n_blocks - 1 - j) * tmm).wait()

    for r in range(td):
        _row_copy(h_ref, r, xs_ref, dest_ref[2 * r], sem).start(priority=0)
        _row_copy(h_ref, r, xs_ref, dest_ref[2 * r + 1], sem).start(priority=1)
    for _ in range(2):
        pltpu.make_async_copy(h_ref, xs_ref.at[pl.ds(0, td)], sem).wait()


def _dispatch(pad_end, n_used, dest_flat, h, *, n_blocks, tmm, td=256):
    T = h.shape[0]
    return pl.pallas_call(
        functools.partial(_dispatch_kernel, td=td, tmm=tmm, n_blocks=n_blocks),
        out_shape=jax.ShapeDtypeStruct((n_blocks * tmm, D_MODEL), F32),
        grid_spec=pltpu.PrefetchScalarGridSpec(
            num_scalar_prefetch=2,
            grid=(T // td,),
            in_specs=[
                pl.BlockSpec((2 * td,), lambda i, pe, nu: (i,), memory_space=pltpu.SMEM),
                pl.BlockSpec((td, D_MODEL), lambda i, pe, nu: (i, 0)),
            ],
            out_specs=pl.BlockSpec(memory_space=pl.ANY),
            scratch_shapes=[pltpu.VMEM((tmm, D_MODEL), F32), pltpu.SemaphoreType.DMA],
        ),
        compiler_params=_params("arbitrary"),
        name="moe_dispatch",
    )(pad_end, n_used, dest_flat, h)


def _gmm_kernel(be_ref, nu_ref, xs_ref, wg_ref, wu_ref, wd_ref, o_ref, *, f_chunks):
    i = pl.program_id(0)

    @pl.when(i < nu_ref[0])
    def _():
        x = xs_ref[...].astype(BF16)
        y = None
        for lo, hi in f_chunks:
            g = _dot(x, wg_ref[0, :, lo:hi])
            u = _dot(x, wu_ref[0, :, lo:hi])
            part = _dot((g * _sigmoid(g) * u).astype(BF16), wd_ref[0, lo:hi, :])
            y = part if y is None else y + part
        o_ref[...] = y

    @pl.when(i >= nu_ref[0])
    def _():
        o_ref[...] = jnp.zeros_like(o_ref)


def _gmm(block_e, n_used, xs, wg, wu, wd, *, tmm):
    n_rows = xs.shape[0]
    n_blocks = n_rows // tmm
    F = wd.shape[1]
    n_pass = F // MXU_PASS_COLS
    assert n_pass * MXU_PASS_COLS == F
    cut = (n_pass // 2) * MXU_PASS_COLS
    f_chunks = ((0, cut), (cut, F))

    def expert_spec(shape):
        return pl.BlockSpec((1,) + shape, lambda i, be, nu: (be[i], 0, 0), pipeline_mode=pl.Buffered(1))

    return pl.pallas_call(
        functools.partial(_gmm_kernel, f_chunks=f_chunks),
        out_shape=jax.ShapeDtypeStruct((n_rows, D_MODEL), F32),
        grid_spec=pltpu.PrefetchScalarGridSpec(
            num_scalar_prefetch=2,
            grid=(n_blocks,),
            in_specs=[
                pl.BlockSpec((tmm, D_MODEL), lambda i, be, nu: (jnp.minimum(i, nu[0] - 1), 0)),
                expert_spec((D_MODEL, F)),
                expert_spec((D_MODEL, F)),
                expert_spec((F, D_MODEL)),
            ],
            out_specs=pl.BlockSpec((tmm, D_MODEL), lambda i, be, nu: (i, 0)),
        ),
        compiler_params=_params("arbitrary"),
        name="moe_grouped_swiglu",
    )(block_e, n_used, xs, wg, wu, wd)


def _combine_ple_kernel(dest_ref, dest_next_ref, x_ref, rw_ref, p_ref, gple_ref, wpg_ref, wpp_ref, gfin_ref,
                        yb_ref, o_ref, buf_ref, sem, *, tp):
    i = pl.program_id(0)
    slot = lax.rem(i, 2)

    def gather(d_ref, sl):
        for r in range(tp):
            _row_copy(yb_ref, d_ref[2 * r], buf_ref.at[sl, 0], r, sem.at[sl]).start(priority=0)
            _row_copy(yb_ref, d_ref[2 * r + 1], buf_ref.at[sl, 1], r, sem.at[sl]).start(priority=1)

    def wait(sl):
        for k in range(2):
            pltpu.make_async_copy(yb_ref.at[pl.ds(0, tp)], buf_ref.at[sl, k], sem.at[sl]).wait()

    @pl.when(i == 0)
    def _():
        gather(dest_ref, 0)

    gather(dest_next_ref, 1 - slot)
    wait(slot)
    rw = rw_ref[...]
    x1 = x_ref[...] + rw[:, 0:1] * buf_ref[slot, 0] + rw[:, 1:2] * buf_ref[slot, 1]
    x2 = _ple(x1, p_ref[...], gple_ref[...], wpg_ref[...], wpp_ref[...])
    o_ref[...] = _rms(x2, gfin_ref[...], RMS_EPS)

    @pl.when(i == pl.num_programs(0) - 1)
    def _():
        wait(1 - slot)


def _combine_ple(dest_flat, x2, rw, p2, gple, wpg, wpp, gfin, yb, *, tp=256):
    T = x2.shape[0]
    n = T // tp
    return pl.pallas_call(
        functools.partial(_combine_ple_kernel, tp=tp),
        out_shape=jax.ShapeDtypeStruct((T, D_MODEL), F32),
        grid=(n,),
        in_specs=[
            pl.BlockSpec((2 * tp,), lambda i: (i,), memory_space=pltpu.SMEM),
            pl.BlockSpec((2 * tp,), lambda i: (jnp.minimum(i + 1, n - 1),), memory_space=pltpu.SMEM),
            pl.BlockSpec((tp, D_MODEL), lambda i: (i, 0)),
            pl.BlockSpec((tp, LANES), lambda i: (i, 0)),
            pl.BlockSpec((tp, PLE_DIM), lambda i: (i, 0)),
            _once((1, D_MODEL)),
            _once((D_MODEL, D_MODEL)),
            _once((PLE_DIM, D_MODEL)),
            _once((1, D_MODEL)),
            pl.BlockSpec(memory_space=pl.ANY),
        ],
        out_specs=pl.BlockSpec((tp, D_MODEL), lambda i: (i, 0)),
        scratch_shapes=[pltpu.VMEM((2, 2, tp, D_MODEL), F32), pltpu.SemaphoreType.DMA((2,))],
        compiler_params=_params("arbitrary"),
        name="moe_combine_ple",
    )(dest_flat, dest_flat, x2, rw, p2, gple, wpg, wpp, gfin, yb)


def _moe_ple_final(x2, p2, gffn, w_router, b_router, wg, wu, wd, gple, wpg, wpp, gfin, *, tmm=512):
    T = x2.shape[0]
    wr = jnp.zeros((D_MODEL, LANES), F32).at[:, :N_EXPERTS].set(w_router)
    whi = wr.astype(BF16)
    wlo = (wr - whi.astype(F32)).astype(BF16)
    br = jnp.full((1, LANES), NEG_BIG, F32).at[0, :N_EXPERTS].set(b_router)
    h, ri, rw, cnt = _router(x2, gffn, whi, wlo, br)
    counts = cnt[0, :N_EXPERTS].astype(jnp.int32)
    padded = (counts + tmm - 1) // tmm * tmm
    pad_end = jnp.cumsum(padded).astype(jnp.int32)
    pad_start = pad_end - padded
    e_sel = ri[:, 0:2, None] == jnp.arange(N_EXPERTS, dtype=jnp.int32)
    dest = jnp.sum(jnp.where(e_sel, pad_start, 0), axis=-1) + ri[:, 2:4]
    dest_flat = dest.reshape(2 * T).astype(jnp.int32)
    n_blocks = (2 * T) // tmm + N_EXPERTS
    block_start = jnp.arange(n_blocks, dtype=jnp.int32) * tmm
    block_e = jnp.minimum(jnp.sum(block_start[:, None] >= pad_end[None, :], axis=1), N_EXPERTS - 1).astype(jnp.int32)
    n_used = (pad_end[-1] // tmm).astype(jnp.int32).reshape(1)
    xs = _dispatch(pad_end, n_used, dest_flat, h, n_blocks=n_blocks, tmm=tmm)
    yb = _gmm(block_e, n_used, xs, wg.astype(BF16), wu.astype(BF16), wd.astype(BF16), tmm=tmm)
    return _combine_ple(dest_flat, x2, rw, p2, gple, wpg, wpp, gfin, yb)


def kernel(x, p, g_mix, w_in, w_gk_up, b_gk, g_gla_norm, conv_w, conv_b, ln_conv_g, ln_conv_b, w_conv_out, w_pool, pool_scale, w_out, g_ffn, w_ffn_gate, w_ffn_up, w_ffn_down, w_router, b_router, w_moe_gate, w_moe_up, w_moe_down, g_ple, w_ple_gate, w_ple_proj, g_final):
    B, S, _ = x.shape
    T = B * S
    depth = w_in.shape[0]
    assert depth == 2 and x.shape[2] == D_MODEL, "dense layer 0, routed layer 1 with the final norm fused"
    x2 = x.reshape(T, D_MODEL)
    row = lambda a: a.reshape(1, -1)
    for i in range(depth):
        wi = w_in[i]
        w_main = jnp.concatenate([wi[:, :3072], wi[:, 3088:]], axis=1).astype(BF16)
        w_gk = jnp.zeros((D_MODEL, GK_PAD), F32).at[:, :GLA_GATE_RANK].set(wi[:, 3072:3088]).astype(BF16)
        w_gk_up_p = jnp.zeros((GK_PAD, GLA_KEY_DIM), F32).at[:GLA_GATE_RANK].set(w_gk_up[i]).astype(BF16)
        z, zgk = _in_proj(x2, row(g_mix[i]), w_main, w_gk)
        x2 = _mixer(z, zgk, x2, w_gk_up_p, row(b_gk[i]), row(g_gla_norm[i]), conv_w[i], row(conv_b[i]),
                    row(ln_conv_g[i]), row(ln_conv_b[i]), w_conv_out[i].astype(BF16), w_pool[i].astype(BF16),
                    row(pool_scale[i]), w_out[i].astype(BF16), B=B, S=S)
        p2 = p[i].reshape(T, PLE_DIM)
        wpg = w_ple_gate[i].astype(BF16)
        wpp = w_ple_proj[i].astype(BF16)
        if i % 2 == 0:
            j = i // 2
            wgu = jnp.concatenate([w_ffn_gate[j].astype(BF16), w_ffn_up[j].astype(BF16)], axis=1)
            x2 = _ffn_ple(x2, p2, row(g_ffn[i]), wgu, w_ffn_down[j].astype(BF16), row(g_ple[i]), wpg, wpp)
        else:
            j = i // 2
            x2 = _moe_ple_final(x2, p2, row(g_ffn[i]), w_router[j], b_router[j], w_moe_gate[j], w_moe_up[j],
                                w_moe_down[j], row(g_ple[i]), wpg, wpp, row(g_final))
    return x2.reshape(B, S, D_MODEL)
```

```python
import functools

import jax
import jax.numpy as jnp
from jax import lax
from jax.experimental import pallas as pl
from jax.experimental.pallas import tpu as pltpu

F32 = jnp.float32
BF16 = jnp.bfloat16

D_MODEL = 1024
GLA_HEADS = 4
GLA_KEY_DIM = 512
GLA_HEAD_K = 128
GLA_HEAD_V = 256
GLA_GATE_RANK = 16
GLA_GATE_NORMALIZER = 16.0
GLA_CHUNK = 64
GLA_NORM_EPS = 1e-5
CONV_WIDTH = 31
LN_EPS = 1e-5
POOL_WINDOWS = (2, 4, 8, 16)
POOL_GROUP_CH = 256
N_EXPERTS = 8
PLE_DIM = 256
RMS_EPS = 1e-6

Z_Q, Z_K, Z_V, Z_G, Z_CA, Z_CG, Z_POOL, Z_GATE = 0, 512, 1024, 2048, 3072, 4096, 5120, 6144
Z_DIM = 9216
GK_PAD = 128
LANES = 128

CONV_HALO = 32
POOL_HALO = 128
CONV_RC = 64
VMEM_LIMIT = 56 * 1024 * 1024
GMM_VMEM_LIMIT = 60 * 1024 * 1024

NEG_BIG = -1e30
MXU_PASS_COLS = 512


def _dot(a, b):
    return jnp.dot(a, b, preferred_element_type=F32)


def _dot_nt(a, b):
    return lax.dot_general(a, b, (((1,), (1,)), ((), ())), preferred_element_type=F32)


def _dot_tn(a, b):
    return lax.dot_general(a, b, (((0,), (0,)), ((), ())), preferred_element_type=F32)


def _sigmoid(x):
    return 0.5 * jnp.tanh(0.5 * x) + 0.5


def _rms(x, g, eps):
    return x * lax.rsqrt(jnp.mean(x * x, axis=-1, keepdims=True) + eps) * g


def _split3(x):
    hi = x.astype(BF16)
    r1 = x - hi.astype(F32)
    mid = r1.astype(BF16)
    lo = (r1 - mid.astype(F32)).astype(BF16)
    return hi, mid, lo


def _once(shape):
    n = len(shape)
    return pl.BlockSpec(shape, lambda *_: (0,) * n, pipeline_mode=pl.Buffered(1))


def _params(*semantics):
    return pltpu.CompilerParams(dimension_semantics=semantics, vmem_limit_bytes=VMEM_LIMIT)


def _in_proj_kernel(x_ref, g_ref, w_ref, wgk_ref, z_ref, zgk_ref, *, nc):
    h = _rms(x_ref[...], g_ref[...], RMS_EPS).astype(BF16)
    for c in range(0, Z_DIM, nc):
        z_ref[:, c:c + nc] = _dot(h, w_ref[:, c:c + nc]).astype(BF16)
    zgk_ref[...] = _dot(h, wgk_ref[...])


def _in_proj(x2, g, w_main, w_gk, *, tm=512, nc=1024):
    T = x2.shape[0]
    return pl.pallas_call(
        functools.partial(_in_proj_kernel, nc=nc),
        out_shape=(jax.ShapeDtypeStruct((T, Z_DIM), BF16), jax.ShapeDtypeStruct((T, GK_PAD), F32)),
        grid=(T // tm,),
        in_specs=[
            pl.BlockSpec((tm, D_MODEL), lambda i: (i, 0)),
            _once((1, D_MODEL)),
            _once((D_MODEL, Z_DIM)),
            _once((D_MODEL, GK_PAD)),
        ],
        out_specs=(pl.BlockSpec((tm, Z_DIM), lambda i: (i, 0)), pl.BlockSpec((tm, GK_PAD), lambda i: (i, 0))),
        compiler_params=_params("arbitrary"),
        name="in_proj",
    )(x2, g, w_main, w_gk)


def _mixer_kernel(z_ref, zgk_ref, x_ref, wgk_ref, bgk_ref, gnorm_ref, convw_ref, convb_ref, lng_ref, lnb_ref,
                  wco_ref, wpool_ref, pscale_ref, wout_ref, o_ref,
                  state_ref, ubuf_ref, pbuf_ref, ygla_ref, yconv_ref, *, ts):
    s = pl.program_id(1)
    C = GLA_CHUNK

    @pl.when(s == 0)
    def _():
        state_ref[...] = jnp.zeros_like(state_ref)
        ubuf_ref[0:CONV_HALO, :] = jnp.zeros((CONV_HALO, D_MODEL), F32)
        pbuf_ref[0:POOL_HALO, :] = jnp.zeros((POOL_HALO, D_MODEL), BF16)

    gk_pre = _dot(zgk_ref[...].astype(BF16), wgk_ref[...]) + bgk_ref[...]
    gk = (jnp.minimum(gk_pre, 0.0) - jnp.log(1.0 + jnp.exp(-jnp.abs(gk_pre)))) * (1.0 / GLA_GATE_NORMALIZER)
    row = lax.broadcasted_iota(jnp.int32, (ts, ts), 0)
    col = lax.broadcasted_iota(jnp.int32, (ts, ts), 1)
    same_chunk_causal = (jnp.right_shift(row, 6) == jnp.right_shift(col, 6)) & (col <= row)
    tri = same_chunk_causal.astype(BF16)
    g_hi, g_mid, g_lo = _split3(gk)
    b_all = _dot(tri, g_hi) + _dot(tri, g_mid) + _dot(tri, g_lo)
    nchunk = ts // C

    def per_chunk_row(r):
        return jnp.concatenate(
            [jnp.broadcast_to(b_all[c * C + r:c * C + r + 1, :], (C, GLA_KEY_DIM)) for c in range(nchunk)], axis=0)

    b_mid = per_chunk_row(C // 2 - 1)
    b_last = per_chunk_row(C - 1)
    q = z_ref[:, Z_Q:Z_Q + GLA_KEY_DIM].astype(F32) * (GLA_HEAD_K ** -0.5)
    k = z_ref[:, Z_K:Z_K + GLA_KEY_DIM].astype(F32)
    qa = (q * jnp.exp(b_all - b_mid)).astype(BF16)
    ka = (k * jnp.exp(b_mid - b_all)).astype(BF16)
    qd = (q * jnp.exp(b_all)).astype(BF16)
    kd = (k * jnp.exp(b_last - b_all)).astype(BF16)
    e_dec = [jnp.exp(b_all[c * C + C - 1:c * C + C, :]) for c in range(nchunk)]
    gnorm = gnorm_ref[...]
    for h in range(GLA_HEADS):
        ks = slice(h * GLA_HEAD_K, (h + 1) * GLA_HEAD_K)
        v = z_ref[:, Z_V + h * GLA_HEAD_V:Z_V + (h + 1) * GLA_HEAD_V]
        attn = jnp.where(same_chunk_causal, _dot_nt(qa[:, ks], ka[:, ks]), 0.0).astype(BF16)
        o = _dot(attn, v)
        st = state_ref[h]
        o_inter = []
        for c in range(nchunk):
            rs = slice(c * C, (c + 1) * C)
            o_inter.append(_dot_nt(qd[rs, ks], st.astype(BF16)))
            st = st * e_dec[c][:, ks] + _dot_tn(v[rs, :], kd[rs, ks])
        state_ref[h] = st
        o = o + jnp.concatenate(o_inter, axis=0)
        o = o * lax.rsqrt(jnp.mean(o * o, axis=-1, keepdims=True) + GLA_NORM_EPS) * gnorm
        gi = z_ref[:, Z_G + h * GLA_HEAD_V:Z_G + (h + 1) * GLA_HEAD_V].astype(F32)
        ygla_ref[:, h * GLA_HEAD_V:(h + 1) * GLA_HEAD_V] = o * (gi * _sigmoid(gi))

    ca = z_ref[:, Z_CA:Z_CA + D_MODEL].astype(F32)
    cg = z_ref[:, Z_CG:Z_CG + D_MODEL].astype(F32)
    ubuf_ref[CONV_HALO:CONV_HALO + ts, :] = ca * _sigmoid(cg)
    RC = CONV_RC

    def conv_rows(i, carry):
        base = pl.multiple_of(i * RC, RC)
        for cc in range(D_MODEL // LANES):
            cs = slice(cc * LANES, (cc + 1) * LANES)
            acc = None
            for r in range(8):
                n_rows = RC if r == 0 else RC + 8
                part = None
                for a in range(5):
                    o = 8 * a + r
                    if o < 2 or o > CONV_HALO:
                        continue
                    kk = o - 2
                    term = convw_ref[kk:kk + 1, cs] * ubuf_ref[pl.ds(base + 8 * a, n_rows), cs]
                    part = term if part is None else part + term
                part = part[r:r + RC, :]
                acc = part if acc is None else acc + part
            yconv_ref[pl.ds(base, RC), cs] = acc + convb_ref[:, cs]
        return carry

    lax.fori_loop(0, ts // RC, conv_rows, 0)
    ubuf_ref[0:CONV_HALO, :] = ubuf_ref[ts:ts + CONV_HALO, :]
    yc = yconv_ref[...]
    mu = jnp.mean(yc, axis=-1, keepdims=True)
    ycc = yc - mu
    var = jnp.mean(ycc * ycc, axis=-1, keepdims=True)
    yn = ycc * lax.rsqrt(var + LN_EPS) * lng_ref[...] + lnb_ref[...]
    y_conv = _dot((yn * _sigmoid(yn)).astype(BF16), wco_ref[...])

    pbuf_ref[POOL_HALO:POOL_HALO + ts, :] = z_ref[:, Z_POOL:Z_POOL + D_MODEL]
    tpos = s * ts + lax.broadcasted_iota(jnp.int32, (ts, 1), 0)
    prow = lax.broadcasted_iota(jnp.int32, (ts, POOL_HALO + ts), 0) + POOL_HALO
    pcol = lax.broadcasted_iota(jnp.int32, (ts, POOL_HALO + ts), 1)
    y_pool = []
    for gi_, w in enumerate(POOL_WINDOWS):
        cs = slice(gi_ * POOL_GROUP_CH, (gi_ + 1) * POOL_GROUP_CH)
        band = ((pcol <= prow) & (pcol > prow - w)).astype(BF16)
        wsum = _dot(band, pbuf_ref[:, cs])
        cnt = jnp.minimum(tpos + 1, w).astype(F32)
        d = wsum / cnt - z_ref[:, Z_POOL + gi_ * POOL_GROUP_CH:Z_POOL + (gi_ + 1) * POOL_GROUP_CH].astype(F32)
        y_pool.append(_dot(d.astype(BF16), wpool_ref[gi_]) * pscale_ref[:, cs])
    pbuf_ref[0:POOL_HALO, :] = pbuf_ref[ts:ts + POOL_HALO, :]

    mix_parts = []
    for gi_ in range(4):
        cs = slice(gi_ * POOL_GROUP_CH, (gi_ + 1) * POOL_GROUP_CH)
        g0 = _sigmoid(z_ref[:, Z_GATE + gi_ * 256:Z_GATE + (gi_ + 1) * 256])
        g1 = _sigmoid(z_ref[:, Z_GATE + D_MODEL + gi_ * 256:Z_GATE + D_MODEL + (gi_ + 1) * 256])
        g2 = _sigmoid(z_ref[:, Z_GATE + 2 * D_MODEL + gi_ * 256:Z_GATE + 2 * D_MODEL + (gi_ + 1) * 256])
        m = g0 * ygla_ref[:, cs].astype(BF16) + g1 * y_conv[:, cs].astype(BF16) + g2 * y_pool[gi_].astype(BF16)
        mix_parts.append(m)
    o_ref[...] = x_ref[...] + _dot(jnp.concatenate(mix_parts, axis=1), wout_ref[...])


def _mixer(z, zgk, x2, wgk, bgk, gnorm, convw, convb, lng, lnb, wco, wpool, pscale, wout, *, B, S, ts=256):
    T = B * S
    nt = S // ts
    rowmap = lambda b, s: (b * nt + s, 0)
    return pl.pallas_call(
        functools.partial(_mixer_kernel, ts=ts),
        out_shape=jax.ShapeDtypeStruct((T, D_MODEL), F32),
        grid=(B, nt),
        in_specs=[
            pl.BlockSpec((ts, Z_DIM), rowmap),
            pl.BlockSpec((ts, GK_PAD), rowmap),
            pl.BlockSpec((ts, D_MODEL), rowmap),
            _once((GK_PAD, GLA_KEY_DIM)),
            _once((1, GLA_KEY_DIM)),
            _once((1, GLA_HEAD_V)),
            _once((CONV_WIDTH, D_MODEL)),
            _once((1, D_MODEL)),
            _once((1, D_MODEL)),
            _once((1, D_MODEL)),
            _once((D_MODEL, D_MODEL)),
            _once((4, POOL_GROUP_CH, POOL_GROUP_CH)),
            _once((1, D_MODEL)),
            _once((D_MODEL, D_MODEL)),
        ],
        out_specs=pl.BlockSpec((ts, D_MODEL), rowmap),
        scratch_shapes=[
            pltpu.VMEM((GLA_HEADS, GLA_HEAD_V, GLA_HEAD_K), F32),
            pltpu.VMEM((CONV_HALO + ts, D_MODEL), F32),
            pltpu.VMEM((POOL_HALO + ts, D_MODEL), BF16),
            pltpu.VMEM((ts, D_MODEL), F32),
            pltpu.VMEM((ts, D_MODEL), F32),
        ],
        compiler_params=_params("arbitrary", "arbitrary"),
        name="token_mixer",
    )(z, zgk, x2, wgk, bgk, gnorm, convw, convb, lng, lnb, wco, wpool, pscale, wout)


def _ple(x1, p, gple, wpg, wpp):
    hp = _rms(x1, gple, RMS_EPS).astype(BF16)
    return x1 + _sigmoid(_dot(hp, wpg)) * _dot(p.astype(BF16), wpp)


def _ffn_ple_kernel(x_ref, p_ref, gffn_ref, wgu_ref, wd_ref, gple_ref, wpg_ref, wpp_ref, o_ref):
    x = x_ref[...]
    h = _rms(x, gffn_ref[...], RMS_EPS).astype(BF16)
    F = wd_ref.shape[0]
    gu = _dot(h, wgu_ref[...])
    g = gu[:, :F]
    u = gu[:, F:]
    y = x + _dot((g * _sigmoid(g) * u).astype(BF16), wd_ref[...])
    o_ref[...] = _ple(y, p_ref[...], gple_ref[...], wpg_ref[...], wpp_ref[...])


def _ffn_ple(x2, p2, gffn, wgu, wd, gple, wpg, wpp, *, tm=512):
    T = x2.shape[0]
    F = wd.shape[0]
    return pl.pallas_call(
        _ffn_ple_kernel,
        out_shape=jax.ShapeDtypeStruct((T, D_MODEL), F32),
        grid=(T // tm,),
        in_specs=[
            pl.BlockSpec((tm, D_MODEL), lambda i: (i, 0)),
            pl.BlockSpec((tm, PLE_DIM), lambda i: (i, 0)),
            _once((1, D_MODEL)),
            _once((D_MODEL, 2 * F)),
            _once((F, D_MODEL)),
            _once((1, D_MODEL)),
            _once((D_MODEL, D_MODEL)),
            _once((PLE_DIM, D_MODEL)),
        ],
        out_specs=pl.BlockSpec((tm, D_MODEL), lambda i: (i, 0)),
        compiler_params=_params("arbitrary"),
        name="ffn_ple",
    )(x2, p2, gffn, wgu, wd, gple, wpg, wpp)


def _router_kernel(x_ref, g_ref, whi_ref, wlo_ref, br_ref, h_ref, ri_ref, rw_ref, cnt_ref, run_ref, *, tm):
    i = pl.program_id(0)

    @pl.when(i == 0)
    def _():
        run_ref[...] = jnp.zeros_like(run_ref)

    h = _rms(x_ref[...], g_ref[...], RMS_EPS)
    h_ref[...] = h
    hh = h.astype(BF16)
    hl = (h - hh.astype(F32)).astype(BF16)
    logits = _dot(hh, whi_ref[...]) + _dot(hh, wlo_ref[...]) + _dot(hl, whi_ref[...]) + br_ref[...]
    lane = lax.broadcasted_iota(jnp.int32, (tm, LANES), 1)
    m1 = jnp.max(logits, axis=-1, keepdims=True)
    i1 = jnp.min(jnp.where(logits == m1, lane, LANES), axis=-1, keepdims=True)
    l2 = jnp.where(lane == i1, -jnp.inf, logits)
    m2 = jnp.max(l2, axis=-1, keepdims=True)
    i2 = jnp.min(jnp.where(l2 == m2, lane, LANES), axis=-1, keepdims=True)
    e = jnp.exp(m2 - m1)
    w1 = 1.0 / (1.0 + e)
    w2 = e / (1.0 + e)
    sel1 = lane == i1
    sel2 = lane == i2
    onehot = (sel1 | sel2).astype(BF16)
    row = lax.broadcasted_iota(jnp.int32, (tm, tm), 0)
    col = lax.broadcasted_iota(jnp.int32, (tm, tm), 1)
    before = _dot((col < row).astype(BF16), onehot) + run_ref[...]
    pos1 = jnp.sum(jnp.where(sel1, before, 0.0), axis=-1, keepdims=True).astype(jnp.int32)
    pos2 = jnp.sum(jnp.where(sel2, before, 0.0), axis=-1, keepdims=True).astype(jnp.int32)
    run = run_ref[...] + jnp.sum(onehot.astype(F32), axis=0, keepdims=True)
    run_ref[...] = run
    cnt_ref[...] = run
    ri_ref[...] = jnp.where(lane == 0, i1, jnp.where(lane == 1, i2, jnp.where(lane == 2, pos1, jnp.where(lane == 3, pos2, 0))))
    rw_ref[...] = jnp.where(lane == 0, w1, jnp.where(lane == 1, w2, 0.0))


def _router(x2, g, whi, wlo, br, *, tm=512):
    T = x2.shape[0]
    return pl.pallas_call(
        functools.partial(_router_kernel, tm=tm),
        out_shape=(
            jax.ShapeDtypeStruct((T, D_MODEL), F32),
            jax.ShapeDtypeStruct((T, LANES), jnp.int32),
            jax.ShapeDtypeStruct((T, LANES), F32),
            jax.ShapeDtypeStruct((1, LANES), F32),
        ),
        grid=(T // tm,),
        in_specs=[
            pl.BlockSpec((tm, D_MODEL), lambda i: (i, 0)),
            _once((1, D_MODEL)),
            _once((D_MODEL, LANES)),
            _once((D_MODEL, LANES)),
            _once((1, LANES)),
        ],
        out_specs=(
            pl.BlockSpec((tm, D_MODEL), lambda i: (i, 0)),
            pl.BlockSpec((tm, LANES), lambda i: (i, 0)),
            pl.BlockSpec((tm, LANES), lambda i: (i, 0)),
            pl.BlockSpec((1, LANES), lambda i: (0, 0)),
        ),
        scratch_shapes=[pltpu.VMEM((1, LANES), F32)],
        compiler_params=_params("arbitrary"),
        name="router",
    )(x2, g, whi, wlo, br)


def _row_copy(src_ref, src_row, dst_ref, dst_row, sem):
    return pltpu.make_async_copy(src_ref.at[pl.ds(src_row, 1)], dst_ref.at[pl.ds(dst_row, 1)], sem)


def _dispatch_kernel(pe_ref, nu_ref, dest_ref, h_ref, xs_ref, zero_ref, sem, *, td, tmm, n_blocks):
    @pl.when(pl.program_id(0) == 0)
    def _():
        zero_ref[...] = jnp.zeros_like(zero_ref)

        def fill(start):
            return pltpu.make_async_copy(zero_ref, xs_ref.at[pl.ds(pl.multiple_of(start, tmm), tmm)], sem)

        for e in range(N_EXPERTS):
            prev_end = pe_ref[e - 1] if e > 0 else 0

            @pl.when(pe_ref[e] > prev_end)
            def _():
                fill(pe_ref[e] - tmm).start()
                fill(pe_ref[e] - tmm).wait()

        for j in range(N_EXPERTS):
            @pl.when(n_blocks - 1 - j >= nu_ref[0])
            def _():
                fill((n_blocks - 1 - j) * tmm).start()
                fill((n_blocks - 1 - j) * tmm).wait()

    for r in range(td):
        _row_copy(h_ref, r, xs_ref, dest_ref[2 * r], sem).start(priority=0)
        _row_copy(h_ref, r, xs_ref, dest_ref[2 * r + 1], sem).start(priority=1)
    for _ in range(2):
        pltpu.make_async_copy(h_ref, xs_ref.at[pl.ds(0, td)], sem).wait()


def _dispatch(pad_end, n_used, dest_flat, h, *, n_blocks, tmm, td=512):
    T = h.shape[0]
    return pl.pallas_call(
        functools.partial(_dispatch_kernel, td=td, tmm=tmm, n_blocks=n_blocks),
        out_shape=jax.ShapeDtypeStruct((n_blocks * tmm, D_MODEL), F32),
        grid_spec=pltpu.PrefetchScalarGridSpec(
            num_scalar_prefetch=2,
            grid=(T // td,),
            in_specs=[
                pl.BlockSpec((2 * td,), lambda i, pe, nu: (i,), memory_space=pltpu.SMEM),
                pl.BlockSpec((td, D_MODEL), lambda i, pe, nu: (i, 0)),
            ],
            out_specs=pl.BlockSpec(memory_space=pl.ANY),
            scratch_shapes=[pltpu.VMEM((tmm, D_MODEL), F32), pltpu.SemaphoreType.DMA],
        ),
        compiler_params=_params("arbitrary"),
        name="moe_dispatch",
    )(pad_end, n_used, dest_flat, h)


def _gmm_kernel(be_ref, nu_ref, xs_ref, wg_ref, wu_ref, wd_ref, o_ref, *, f_chunks):
    i = pl.program_id(0)

    @pl.when(i < nu_ref[0])
    def _():
        x = xs_ref[...].astype(BF16)
        y = None
        for lo, hi in f_chunks:
            g = _dot(x, wg_ref[0, :, lo:hi])
            u = _dot(x, wu_ref[0, :, lo:hi])
            part = _dot((g * _sigmoid(g) * u).astype(BF16), wd_ref[0, lo:hi, :])
            y = part if y is None else y + part
        o_ref[...] = y

    @pl.when(i >= nu_ref[0])
    def _():
        o_ref[...] = jnp.zeros_like(o_ref)


def _gmm(block_e, n_used, xs, wg, wu, wd, *, tmm):
    n_rows = xs.shape[0]
    n_blocks = n_rows // tmm
    F = wd.shape[1]
    n_pass = F // MXU_PASS_COLS
    assert n_pass * MXU_PASS_COLS == F
    cut = (n_pass // 2) * MXU_PASS_COLS
    f_chunks = ((0, cut), (cut, F))

    def expert_spec(shape):
        return pl.BlockSpec((1,) + shape, lambda i, be, nu: (be[i], 0, 0))

    return pl.pallas_call(
        functools.partial(_gmm_kernel, f_chunks=f_chunks),
        out_shape=jax.ShapeDtypeStruct((n_rows, D_MODEL), F32),
        grid_spec=pltpu.PrefetchScalarGridSpec(
            num_scalar_prefetch=2,
            grid=(n_blocks,),
            in_specs=[
                pl.BlockSpec((tmm, D_MODEL), lambda i, be, nu: (jnp.minimum(i, nu[0] - 1), 0)),
                expert_spec((D_MODEL, F)),
                expert_spec((D_MODEL, F)),
                expert_spec((F, D_MODEL)),
            ],
            out_specs=pl.BlockSpec((tmm, D_MODEL), lambda i, be, nu: (i, 0)),
        ),
        compiler_params=pltpu.CompilerParams(dimension_semantics=("arbitrary",), vmem_limit_bytes=GMM_VMEM_LIMIT),
        name="moe_grouped_swiglu",
    )(block_e, n_used, xs, wg, wu, wd)


def _combine_ple_kernel(dest_ref, dest_next_ref, x_ref, rw_ref, p_ref, gple_ref, wpg_ref, wpp_ref, gfin_ref,
                        yb_ref, o_ref, buf_ref, sem, *, tp):
    i = pl.program_id(0)
    slot = lax.rem(i, 2)

    def gather(d_ref, sl):
        for r in range(tp):
            _row_copy(yb_ref, d_ref[2 * r], buf_ref.at[sl, 0], r, sem.at[sl]).start(priority=0)
            _row_copy(yb_ref, d_ref[2 * r + 1], buf_ref.at[sl, 1], r, sem.at[sl]).start(priority=1)

    def wait(sl):
        for k in range(2):
            pltpu.make_async_copy(yb_ref.at[pl.ds(0, tp)], buf_ref.at[sl, k], sem.at[sl]).wait()

    @pl.when(i == 0)
    def _():
        gather(dest_ref, 0)

    gather(dest_next_ref, 1 - slot)
    wait(slot)
    rw = rw_ref[...]
    x1 = x_ref[...] + rw[:, 0:1] * buf_ref[slot, 0] + rw[:, 1:2] * buf_ref[slot, 1]
    x2 = _ple(x1, p_ref[...], gple_ref[...], wpg_ref[...], wpp_ref[...])
    o_ref[...] = _rms(x2, gfin_ref[...], RMS_EPS)

    @pl.when(i == pl.num_programs(0) - 1)
    def _():
        wait(1 - slot)


def _combine_ple(dest_flat, x2, rw, p2, gple, wpg, wpp, gfin, yb, *, tp=512):
    T = x2.shape[0]
    n = T // tp
    return pl.pallas_call(
        functools.partial(_combine_ple_kernel, tp=tp),
        out_shape=jax.ShapeDtypeStruct((T, D_MODEL), F32),
        grid=(n,),
        in_specs=[
            pl.BlockSpec((2 * tp,), lambda i: (i,), memory_space=pltpu.SMEM),
            pl.BlockSpec((2 * tp,), lambda i: (jnp.minimum(i + 1, n - 1),), memory_space=pltpu.SMEM),
            pl.BlockSpec((tp, D_MODEL), lambda i: (i, 0)),
            pl.BlockSpec((tp, LANES), lambda i: (i, 0)),
            pl.BlockSpec((tp, PLE_DIM), lambda i: (i, 0)),
            _once((1, D_MODEL)),
            _once((D_MODEL, D_MODEL)),
            _once((PLE_DIM, D_MODEL)),
            _once((1, D_MODEL)),
            pl.BlockSpec(memory_space=pl.ANY),
        ],
        out_specs=pl.BlockSpec((tp, D_MODEL), lambda i: (i, 0)),
        scratch_shapes=[pltpu.VMEM((2, 2, tp, D_MODEL), F32), pltpu.SemaphoreType.DMA((2,))],
        compiler_params=_params("arbitrary"),
        name="moe_combine_ple",
    )(dest_flat, dest_flat, x2, rw, p2, gple, wpg, wpp, gfin, yb)


def _moe_ple_final(x2, p2, gffn, w_router, b_router, wg, wu, wd, gple, wpg, wpp, gfin, *, tmm=512):
    T = x2.shape[0]
    wr = jnp.zeros((D_MODEL, LANES), F32).at[:, :N_EXPERTS].set(w_router)
    whi = wr.astype(BF16)
    wlo = (wr - whi.astype(F32)).astype(BF16)
    br = jnp.full((1, LANES), NEG_BIG, F32).at[0, :N_EXPERTS].set(b_router)
    h, ri, rw, cnt = _router(x2, gffn, whi, wlo, br)
    counts = cnt[0, :N_EXPERTS].astype(jnp.int32)
    padded = (counts + tmm - 1) // tmm * tmm
    pad_end = jnp.cumsum(padded).astype(jnp.int32)
    pad_start = pad_end - padded
    e_sel = ri[:, 0:2, None] == jnp.arange(N_EXPERTS, dtype=jnp.int32)
    dest = jnp.sum(jnp.where(e_sel, pad_start, 0), axis=-1) + ri[:, 2:4]
    dest_flat = dest.reshape(2 * T).astype(jnp.int32)
    n_blocks = (2 * T) // tmm + N_EXPERTS
    block_start = jnp.arange(n_blocks, dtype=jnp.int32) * tmm
    block_e = jnp.minimum(jnp.sum(block_start[:, None] >= pad_end[None, :], axis=1), N_EXPERTS - 1).astype(jnp.int32)
    n_used = (pad_end[-1] // tmm).astype(jnp.int32).reshape(1)
    xs = _dispatch(pad_end, n_used, dest_flat, h, n_blocks=n_blocks, tmm=tmm)
    yb = _gmm(block_e, n_used, xs, wg.astype(BF16), wu.astype(BF16), wd.astype(BF16), tmm=tmm)
    return _combine_ple(dest_flat, x2, rw, p2, gple, wpg, wpp, gfin, yb)


def kernel(x, p, g_mix, w_in, w_gk_up, b_gk, g_gla_norm, conv_w, conv_b, ln_conv_g, ln_conv_b, w_conv_out, w_pool, pool_scale, w_out, g_ffn, w_ffn_gate, w_ffn_up, w_ffn_down, w_router, b_router, w_moe_gate, w_moe_up, w_moe_down, g_ple, w_ple_gate, w_ple_proj, g_final):
    B, S, _ = x.shape
    T = B * S
    depth = w_in.shape[0]
    assert depth == 2 and x.shape[2] == D_MODEL, "dense layer 0, routed layer 1 with the final norm fused"
    x2 = x.reshape(T, D_MODEL)
    row = lambda a: a.reshape(1, -1)
    for i in range(depth):
        wi = w_in[i]
        w_main = jnp.concatenate([wi[:, :3072], wi[:, 3088:]], axis=1).astype(BF16)
        w_gk = jnp.zeros((D_MODEL, GK_PAD), F32).at[:, :GLA_GATE_RANK].set(wi[:, 3072:3088]).astype(BF16)
        w_gk_up_p = jnp.zeros((GK_PAD, GLA_KEY_DIM), F32).at[:GLA_GATE_RANK].set(w_gk_up[i]).astype(BF16)
        z, zgk = _in_proj(x2, row(g_mix[i]), w_main, w_gk)
        x2 = _mixer(z, zgk, x2, w_gk_up_p, row(b_gk[i]), row(g_gla_norm[i]), conv_w[i], row(conv_b[i]),
                    row(ln_conv_g[i]), row(ln_conv_b[i]), w_conv_out[i].astype(BF16), w_pool[i].astype(BF16),
                    row(pool_scale[i]), w_out[i].astype(BF16), B=B, S=S)
        p2 = p[i].reshape(T, PLE_DIM)
        wpg = w_ple_gate[i].astype(BF16)
        wpp = w_ple_proj[i].astype(BF16)
        if i % 2 == 0:
            j = i // 2
            wgu = jnp.concatenate([w_ffn_gate[j].astype(BF16), w_ffn_up[j].astype(BF16)], axis=1)
            x2 = _ffn_ple(x2, p2, row(g_ffn[i]), wgu, w_ffn_down[j].astype(BF16), row(g_ple[i]), wpg, wpp)
        else:
            j = i // 2
            x2 = _moe_ple_final(x2, p2, row(g_ffn[i]), w_router[j], b_router[j], w_moe_gate[j], w_moe_up[j],
                                w_moe_down[j], row(g_ple[i]), wpg, wpp, row(g_final))
    return x2.reshape(B, S, D_MODEL)
```

```python
import functools

import jax
import jax.numpy as jnp
from jax import lax
from jax.experimental import pallas as pl
from jax.experimental.pallas import tpu as pltpu

F32 = jnp.float32
BF16 = jnp.bfloat16

D_MODEL = 1024
GLA_HEADS = 4
GLA_KEY_DIM = 512
GLA_HEAD_K = 128
GLA_HEAD_V = 256
GLA_GATE_RANK = 16
GLA_GATE_NORMALIZER = 16.0
GLA_CHUNK = 64
GLA_NORM_EPS = 1e-5
CONV_WIDTH = 31
LN_EPS = 1e-5
POOL_WINDOWS = (2, 4, 8, 16)
POOL_GROUP_CH = 256
N_EXPERTS = 8
PLE_DIM = 256
RMS_EPS = 1e-6

Z_Q, Z_K, Z_V, Z_G, Z_CA, Z_CG, Z_POOL, Z_GATE = 0, 512, 1024, 2048, 3072, 4096, 5120, 6144
Z_DIM = 9216
GK_PAD = 128
LANES = 128

CONV_HALO = 32
POOL_HALO = 128
CONV_RC = 64
VMEM_LIMIT = 56 * 1024 * 1024
GMM_VMEM_LIMIT = 60 * 1024 * 1024

NEG_BIG = -1e30
MXU_PASS_COLS = 512


def _dot(a, b):
    return jnp.dot(a, b, preferred_element_type=F32)


def _dot_nt(a, b):
    return lax.dot_general(a, b, (((1,), (1,)), ((), ())), preferred_element_type=F32)


def _dot_tn(a, b):
    return lax.dot_general(a, b, (((0,), (0,)), ((), ())), preferred_element_type=F32)


def _sigmoid(x):
    return 0.5 * jnp.tanh(0.5 * x) + 0.5


def _rms(x, g, eps):
    return x * lax.rsqrt(jnp.mean(x * x, axis=-1, keepdims=True) + eps) * g


def _split3(x):
    hi = x.astype(BF16)
    r1 = x - hi.astype(F32)
    mid = r1.astype(BF16)
    lo = (r1 - mid.astype(F32)).astype(BF16)
    return hi, mid, lo


def _once(shape):
    n = len(shape)
    return pl.BlockSpec(shape, lambda *_: (0,) * n, pipeline_mode=pl.Buffered(1))


def _params(*semantics):
    return pltpu.CompilerParams(dimension_semantics=semantics, vmem_limit_bytes=VMEM_LIMIT)


def _in_proj_kernel(x_ref, g_ref, w_ref, wgk_ref, z_ref, zgk_ref, *, nc):
    h = _rms(x_ref[...], g_ref[...], RMS_EPS).astype(BF16)
    for c in range(0, Z_DIM, nc):
        z_ref[:, c:c + nc] = _dot(h, w_ref[:, c:c + nc]).astype(BF16)
    zgk_ref[...] = _dot(h, wgk_ref[...])


def _in_proj(x2, g, w_main, w_gk, *, tm=512, nc=1024):
    T = x2.shape[0]
    return pl.pallas_call(
        functools.partial(_in_proj_kernel, nc=nc),
        out_shape=(jax.ShapeDtypeStruct((T, Z_DIM), BF16), jax.ShapeDtypeStruct((T, GK_PAD), F32)),
        grid=(T // tm,),
        in_specs=[
            pl.BlockSpec((tm, D_MODEL), lambda i: (i, 0)),
            _once((1, D_MODEL)),
            _once((D_MODEL, Z_DIM)),
            _once((D_MODEL, GK_PAD)),
        ],
        out_specs=(pl.BlockSpec((tm, Z_DIM), lambda i: (i, 0)), pl.BlockSpec((tm, GK_PAD), lambda i: (i, 0))),
        compiler_params=_params("arbitrary"),
        name="in_proj",
    )(x2, g, w_main, w_gk)


def _mixer_kernel(z_ref, zgk_ref, x_ref, wgk_ref, bgk_ref, gnorm_ref, convw_ref, convb_ref, lng_ref, lnb_ref,
                  wco_ref, wpool_ref, pscale_ref, wout_ref, o_ref,
                  state_ref, ubuf_ref, pbuf_ref, ygla_ref, yconv_ref, *, ts):
    s = pl.program_id(1)
    C = GLA_CHUNK

    @pl.when(s == 0)
    def _():
        state_ref[...] = jnp.zeros_like(state_ref)
        ubuf_ref[0:CONV_HALO, :] = jnp.zeros((CONV_HALO, D_MODEL), F32)
        pbuf_ref[0:POOL_HALO, :] = jnp.zeros((POOL_HALO, D_MODEL), BF16)

    gk_pre = _dot(zgk_ref[...].astype(BF16), wgk_ref[...]) + bgk_ref[...]
    gk = (jnp.minimum(gk_pre, 0.0) - jnp.log(1.0 + jnp.exp(-jnp.abs(gk_pre)))) * (1.0 / GLA_GATE_NORMALIZER)
    row = lax.broadcasted_iota(jnp.int32, (ts, ts), 0)
    col = lax.broadcasted_iota(jnp.int32, (ts, ts), 1)
    same_chunk_causal = (jnp.right_shift(row, 6) == jnp.right_shift(col, 6)) & (col <= row)
    tri = same_chunk_causal.astype(BF16)
    g_hi, g_mid, g_lo = _split3(gk)
    b_all = _dot(tri, g_hi) + _dot(tri, g_mid) + _dot(tri, g_lo)
    nchunk = ts // C

    def per_chunk_row(r):
        return jnp.concatenate(
            [jnp.broadcast_to(b_all[c * C + r:c * C + r + 1, :], (C, GLA_KEY_DIM)) for c in range(nchunk)], axis=0)

    b_mid = per_chunk_row(C // 2 - 1)
    b_last = per_chunk_row(C - 1)
    q = z_ref[:, Z_Q:Z_Q + GLA_KEY_DIM].astype(F32) * (GLA_HEAD_K ** -0.5)
    k = z_ref[:, Z_K:Z_K + GLA_KEY_DIM].astype(F32)
    qa = (q * jnp.exp(b_all - b_mid)).astype(BF16)
    ka = (k * jnp.exp(b_mid - b_all)).astype(BF16)
    qd = (q * jnp.exp(b_all)).astype(BF16)
    kd = (k * jnp.exp(b_last - b_all)).astype(BF16)
    e_dec = [jnp.exp(b_all[c * C + C - 1:c * C + C, :]) for c in range(nchunk)]
    gnorm = gnorm_ref[...]
    for h in range(GLA_HEADS):
        ks = slice(h * GLA_HEAD_K, (h + 1) * GLA_HEAD_K)
        v = z_ref[:, Z_V + h * GLA_HEAD_V:Z_V + (h + 1) * GLA_HEAD_V]
        attn = jnp.where(same_chunk_causal, _dot_nt(qa[:, ks], ka[:, ks]), 0.0).astype(BF16)
        o = _dot(attn, v)
        st = state_ref[h]
        o_inter = []
        for c in range(nchunk):
            rs = slice(c * C, (c + 1) * C)
            o_inter.append(_dot_nt(qd[rs, ks], st.astype(BF16)))
            st = st * e_dec[c][:, ks] + _dot_tn(v[rs, :], kd[rs, ks])
        state_ref[h] = st
        o = o + jnp.concatenate(o_inter, axis=0)
        o = o * lax.rsqrt(jnp.mean(o * o, axis=-1, keepdims=True) + GLA_NORM_EPS) * gnorm
        gi = z_ref[:, Z_G + h * GLA_HEAD_V:Z_G + (h + 1) * GLA_HEAD_V].astype(F32)
        ygla_ref[:, h * GLA_HEAD_V:(h + 1) * GLA_HEAD_V] = o * (gi * _sigmoid(gi))

    ca = z_ref[:, Z_CA:Z_CA + D_MODEL].astype(F32)
    cg = z_ref[:, Z_CG:Z_CG + D_MODEL].astype(F32)
    ubuf_ref[CONV_HALO:CONV_HALO + ts, :] = ca * _sigmoid(cg)
    RC = CONV_RC

    def conv_rows(i, carry):
        base = pl.multiple_of(i * RC, RC)
        for cc in range(D_MODEL // LANES):
            cs = slice(cc * LANES, (cc + 1) * LANES)
            acc = None
            for r in range(8):
                n_rows = RC if r == 0 else RC + 8
                part = None
                for a in range(5):
                    o = 8 * a + r
                    if o < 2 or o > CONV_HALO:
                        continue
                    kk = o - 2
                    term = convw_ref[kk:kk + 1, cs] * ubuf_ref[pl.ds(base + 8 * a, n_rows), cs]
                    part = term if part is None else part + term
                part = part[r:r + RC, :]
                acc = part if acc is None else acc + part
            yconv_ref[pl.ds(base, RC), cs] = acc + convb_ref[:, cs]
        return carry

    lax.fori_loop(0, ts // RC, conv_rows, 0)
    ubuf_ref[0:CONV_HALO, :] = ubuf_ref[ts:ts + CONV_HALO, :]
    yc = yconv_ref[...]
    mu = jnp.mean(yc, axis=-1, keepdims=True)
    ycc = yc - mu
    var = jnp.mean(ycc * ycc, axis=-1, keepdims=True)
    yn = ycc * lax.rsqrt(var + LN_EPS) * lng_ref[...] + lnb_ref[...]
    y_conv = _dot((yn * _sigmoid(yn)).astype(BF16), wco_ref[...])

    pbuf_ref[POOL_HALO:POOL_HALO + ts, :] = z_ref[:, Z_POOL:Z_POOL + D_MODEL]
    tpos = s * ts + lax.broadcasted_iota(jnp.int32, (ts, 1), 0)
    prow = lax.broadcasted_iota(jnp.int32, (ts, POOL_HALO + ts), 0) + POOL_HALO
    pcol = lax.broadcasted_iota(jnp.int32, (ts, POOL_HALO + ts), 1)
    y_pool = []
    for gi_, w in enumerate(POOL_WINDOWS):
        cs = slice(gi_ * POOL_GROUP_CH, (gi_ + 1) * POOL_GROUP_CH)
        band = ((pcol <= prow) & (pcol > prow - w)).astype(BF16)
        wsum = _dot(band, pbuf_ref[:, cs])
        cnt = jnp.minimum(tpos + 1, w).astype(F32)
        d = wsum / cnt - z_ref[:, Z_POOL + gi_ * POOL_GROUP_CH:Z_POOL + (gi_ + 1) * POOL_GROUP_CH].astype(F32)
        y_pool.append(_dot(d.astype(BF16), wpool_ref[gi_]) * pscale_ref[:, cs])
    pbuf_ref[0:POOL_HALO, :] = pbuf_ref[ts:ts + POOL_HALO, :]

    mix_parts = []
    for gi_ in range(4):
        cs = slice(gi_ * POOL_GROUP_CH, (gi_ + 1) * POOL_GROUP_CH)
        g0 = _sigmoid(z_ref[:, Z_GATE + gi_ * 256:Z_GATE + (gi_ + 1) * 256])
        g1 = _sigmoid(z_ref[:, Z_GATE + D_MODEL + gi_ * 256:Z_GATE + D_MODEL + (gi_ + 1) * 256])
        g2 = _sigmoid(z_ref[:, Z_GATE + 2 * D_MODEL + gi_ * 256:Z_GATE + 2 * D_MODEL + (gi_ + 1) * 256])
        m = g0 * ygla_ref[:, cs].astype(BF16) + g1 * y_conv[:, cs].astype(BF16) + g2 * y_pool[gi_].astype(BF16)
        mix_parts.append(m)
    o_ref[...] = x_ref[...] + _dot(jnp.concatenate(mix_parts, axis=1), wout_ref[...])


def _mixer(z, zgk, x2, wgk, bgk, gnorm, convw, convb, lng, lnb, wco, wpool, pscale, wout, *, B, S, ts=256):
    T = B * S
    nt = S // ts
    rowmap = lambda b, s: (b * nt + s, 0)
    return pl.pallas_call(
        functools.partial(_mixer_kernel, ts=ts),
        out_shape=jax.ShapeDtypeStruct((T, D_MODEL), F32),
        grid=(B, nt),
        in_specs=[
            pl.BlockSpec((ts, Z_DIM), rowmap),
            pl.BlockSpec((ts, GK_PAD), rowmap),
            pl.BlockSpec((ts, D_MODEL), rowmap),
            _once((GK_PAD, GLA_KEY_DIM)),
            _once((1, GLA_KEY_DIM)),
            _once((1, GLA_HEAD_V)),
            _once((CONV_WIDTH, D_MODEL)),
            _once((1, D_MODEL)),
            _once((1, D_MODEL)),
            _once((1, D_MODEL)),
            _once((D_MODEL, D_MODEL)),
            _once((4, POOL_GROUP_CH, POOL_GROUP_CH)),
            _once((1, D_MODEL)),
            _once((D_MODEL, D_MODEL)),
        ],
        out_specs=pl.BlockSpec((ts, D_MODEL), rowmap),
        scratch_shapes=[
            pltpu.VMEM((GLA_HEADS, GLA_HEAD_V, GLA_HEAD_K), F32),
            pltpu.VMEM((CONV_HALO + ts, D_MODEL), F32),
            pltpu.VMEM((POOL_HALO + ts, D_MODEL), BF16),
            pltpu.VMEM((ts, D_MODEL), F32),
            pltpu.VMEM((ts, D_MODEL), F32),
        ],
        compiler_params=_params("arbitrary", "arbitrary"),
        name="token_mixer",
    )(z, zgk, x2, wgk, bgk, gnorm, convw, convb, lng, lnb, wco, wpool, pscale, wout)


def _ple(x1, p, gple, wpg, wpp):
    hp = _rms(x1, gple, RMS_EPS).astype(BF16)
    return x1 + _sigmoid(_dot(hp, wpg)) * _dot(p.astype(BF16), wpp)


def _ffn_ple_kernel(x_ref, p_ref, gffn_ref, wgu_ref, wd_ref, gple_ref, wpg_ref, wpp_ref, o_ref):
    x = x_ref[...]
    h = _rms(x, gffn_ref[...], RMS_EPS).astype(BF16)
    F = wd_ref.shape[0]
    gu = _dot(h, wgu_ref[...])
    g = gu[:, :F]
    u = gu[:, F:]
    y = x + _dot((g * _sigmoid(g) * u).astype(BF16), wd_ref[...])
    o_ref[...] = _ple(y, p_ref[...], gple_ref[...], wpg_ref[...], wpp_ref[...])


def _ffn_ple(x2, p2, gffn, wgu, wd, gple, wpg, wpp, *, tm=512):
    T = x2.shape[0]
    F = wd.shape[0]
    return pl.pallas_call(
        _ffn_ple_kernel,
        out_shape=jax.ShapeDtypeStruct((T, D_MODEL), F32),
        grid=(T // tm,),
        in_specs=[
            pl.BlockSpec((tm, D_MODEL), lambda i: (i, 0)),
            pl.BlockSpec((tm, PLE_DIM), lambda i: (i, 0)),
            _once((1, D_MODEL)),
            _once((D_MODEL, 2 * F)),
            _once((F, D_MODEL)),
            _once((1, D_MODEL)),
            _once((D_MODEL, D_MODEL)),
            _once((PLE_DIM, D_MODEL)),
        ],
        out_specs=pl.BlockSpec((tm, D_MODEL), lambda i: (i, 0)),
        compiler_params=_params("arbitrary"),
        name="ffn_ple",
    )(x2, p2, gffn, wgu, wd, gple, wpg, wpp)


def _router_kernel(x_ref, g_ref, whi_ref, wlo_ref, br_ref, h_ref, ri_ref, rw_ref, cnt_ref, run_ref, *, tm):
    i = pl.program_id(0)

    @pl.when(i == 0)
    def _():
        run_ref[...] = jnp.zeros_like(run_ref)

    h = _rms(x_ref[...], g_ref[...], RMS_EPS)
    h_ref[...] = h
    hh = h.astype(BF16)
    hl = (h - hh.astype(F32)).astype(BF16)
    logits = _dot(hh, whi_ref[...]) + _dot(hh, wlo_ref[...]) + _dot(hl, whi_ref[...]) + br_ref[...]
    lane = lax.broadcasted_iota(jnp.int32, (tm, LANES), 1)
    m1 = jnp.max(logits, axis=-1, keepdims=True)
    i1 = jnp.min(jnp.where(logits == m1, lane, LANES), axis=-1, keepdims=True)
    l2 = jnp.where(lane == i1, -jnp.inf, logits)
    m2 = jnp.max(l2, axis=-1, keepdims=True)
    i2 = jnp.min(jnp.where(l2 == m2, lane, LANES), axis=-1, keepdims=True)
    e = jnp.exp(m2 - m1)
    w1 = 1.0 / (1.0 + e)
    w2 = e / (1.0 + e)
    sel1 = lane == i1
    sel2 = lane == i2
    onehot = (sel1 | sel2).astype(BF16)
    row = lax.broadcasted_iota(jnp.int32, (tm, tm), 0)
    col = lax.broadcasted_iota(jnp.int32, (tm, tm), 1)
    before = _dot((col < row).astype(BF16), onehot) + run_ref[...]
    pos1 = jnp.sum(jnp.where(sel1, before, 0.0), axis=-1, keepdims=True).astype(jnp.int32)
    pos2 = jnp.sum(jnp.where(sel2, before, 0.0), axis=-1, keepdims=True).astype(jnp.int32)
    run = run_ref[...] + jnp.sum(onehot.astype(F32), axis=0, keepdims=True)
    run_ref[...] = run
    cnt_ref[...] = run
    ri_ref[...] = jnp.where(lane == 0, i1, jnp.where(lane == 1, i2, jnp.where(lane == 2, pos1, jnp.where(lane == 3, pos2, 0))))
    rw_ref[...] = jnp.where(lane == 0, w1, jnp.where(lane == 1, w2, 0.0))


def _router(x2, g, whi, wlo, br, *, tm=512):
    T = x2.shape[0]
    return pl.pallas_call(
        functools.partial(_router_kernel, tm=tm),
        out_shape=(
            jax.ShapeDtypeStruct((T, D_MODEL), F32),
            jax.ShapeDtypeStruct((T, LANES), jnp.int32),
            jax.ShapeDtypeStruct((T, LANES), F32),
            jax.ShapeDtypeStruct((1, LANES), F32),
        ),
        grid=(T // tm,),
        in_specs=[
            pl.BlockSpec((tm, D_MODEL), lambda i: (i, 0)),
            _once((1, D_MODEL)),
            _once((D_MODEL, LANES)),
            _once((D_MODEL, LANES)),
            _once((1, LANES)),
        ],
        out_specs=(
            pl.BlockSpec((tm, D_MODEL), lambda i: (i, 0)),
            pl.BlockSpec((tm, LANES), lambda i: (i, 0)),
            pl.BlockSpec((tm, LANES), lambda i: (i, 0)),
            pl.BlockSpec((1, LANES), lambda i: (0, 0)),
        ),
        scratch_shapes=[pltpu.VMEM((1, LANES), F32)],
        compiler_params=_params("arbitrary"),
        name="router",
    )(x2, g, whi, wlo, br)


def _row_copy(src_ref, src_row, dst_ref, dst_row, sem):
    return pltpu.make_async_copy(src_ref.at[pl.ds(src_row, 1)], dst_ref.at[pl.ds(dst_row, 1)], sem)


def _dispatch_kernel(pe_ref, nu_ref, dest_ref, h_ref, xs_ref, zero_ref, sem, *, td, tmm, n_blocks):
    @pl.when(pl.program_id(0) == 0)
    def _():
        zero_ref[...] = jnp.zeros_like(zero_ref)

        def fill(start):
            return pltpu.make_async_copy(zero_ref, xs_ref.at[pl.ds(pl.multiple_of(start, tmm), tmm)], sem)

        for e in range(N_EXPERTS):
            prev_end = pe_ref[e - 1] if e > 0 else 0

            @pl.when(pe_ref[e] > prev_end)
            def _():
                fill(pe_ref[e] - tmm).start()
                fill(pe_ref[e] - tmm).wait()

        for j in range(N_EXPERTS):
            @pl.when(n_blocks - 1 - j >= nu_ref[0])
            def _():
                fill((n_blocks - 1 - j) * tmm).start()
                fill((n_blocks - 1 - j) * tmm).wait()

    for r in range(td):
        _row_copy(h_ref, r, xs_ref, dest_ref[2 * r], sem).start(priority=0)
        _row_copy(h_ref, r, xs_ref, dest_ref[2 * r + 1], sem).start(priority=1)
    for _ in range(2):
        pltpu.make_async_copy(h_ref, xs_ref.at[pl.ds(0, td)], sem).wait()


def _dispatch(pad_end, n_used, dest_flat, h, *, n_blocks, tmm, td=512):
    T = h.shape[0]
    return pl.pallas_call(
        functools.partial(_dispatch_kernel, td=td, tmm=tmm, n_blocks=n_blocks),
        out_shape=jax.ShapeDtypeStruct((n_blocks * tmm, D_MODEL), F32),
        grid_spec=pltpu.PrefetchScalarGridSpec(
            num_scalar_prefetch=2,
            grid=(T // td,),
            in_specs=[
                pl.BlockSpec((2 * td,), lambda i, pe, nu: (i,), memory_space=pltpu.SMEM),
                pl.BlockSpec((td, D_MODEL), lambda i, pe, nu: (i, 0)),
            ],
            out_specs=pl.BlockSpec(memory_space=pl.ANY),
            scratch_shapes=[pltpu.VMEM((tmm, D_MODEL), F32), pltpu.SemaphoreType.DMA],
        ),
        compiler_params=_params("arbitrary"),
        name="moe_dispatch",
    )(pad_end, n_used, dest_flat, h)


def _gmm_kernel(be_ref, nu_ref, xs_ref, wg_ref, wu_ref, wd_ref, o_ref, *, f_chunks):
    i = pl.program_id(0)

    @pl.when(i < nu_ref[0])
    def _():
        x = xs_ref[...].astype(BF16)
        y = None
        for lo, hi in f_chunks:
            g = _dot(x, wg_ref[0, :, lo:hi])
            u = _dot(x, wu_ref[0, :, lo:hi])
            part = _dot((g * _sigmoid(g) * u).astype(BF16), wd_ref[0, lo:hi, :])
            y = part if y is None else y + part
        o_ref[...] = y

    @pl.when(i >= nu_ref[0])
    def _():
        o_ref[...] = jnp.zeros_like(o_ref)


def _gmm(block_e, n_used, xs, wg, wu, wd, *, tmm):
    n_rows = xs.shape[0]
    n_blocks = n_rows // tmm
    F = wd.shape[1]
    n_pass = F // MXU_PASS_COLS
    assert n_pass * MXU_PASS_COLS == F
    cut = (n_pass // 2) * MXU_PASS_COLS
    f_chunks = ((0, cut), (cut, F))

    def expert_spec(shape):
        return pl.BlockSpec((1,) + shape, lambda i, be, nu: (be[i], 0, 0))

    return pl.pallas_call(
        functools.partial(_gmm_kernel, f_chunks=f_chunks),
        out_shape=jax.ShapeDtypeStruct((n_rows, D_MODEL), F32),
        grid_spec=pltpu.PrefetchScalarGridSpec(
            num_scalar_prefetch=2,
            grid=(n_blocks,),
            in_specs=[
                pl.BlockSpec((tmm, D_MODEL), lambda i, be, nu: (jnp.minimum(i, nu[0] - 1), 0)),
                expert_spec((D_MODEL, F)),
                expert_spec((D_MODEL, F)),
                expert_spec((F, D_MODEL)),
            ],
            out_specs=pl.BlockSpec((tmm, D_MODEL), lambda i, be, nu: (i, 0)),
        ),
        compiler_params=pltpu.CompilerParams(dimension_semantics=("arbitrary",), vmem_limit_bytes=GMM_VMEM_LIMIT),
        name="moe_grouped_swiglu",
    )(block_e, n_used, xs, wg, wu, wd)


def _combine_ple_kernel(dest_ref, dest_next_ref, x_ref, rw_ref, p_ref, gple_ref, wpg_ref, wpp_ref, gfin_ref,
                        yb_ref, o_ref, bufa_ref, bufb_ref, sem, *, tp):
    i = pl.program_id(0)
    th = tp // 2
    bufs = (bufa_ref, bufb_ref)

    def gather(d_ref, first, b):
        for r in range(th):
            _row_copy(yb_ref, d_ref[2 * (first + r)], bufs[b].at[0], r, sem.at[b]).start(priority=0)
            _row_copy(yb_ref, d_ref[2 * (first + r) + 1], bufs[b].at[1], r, sem.at[b]).start(priority=1)

    def wait(b):
        for k in range(2):
            pltpu.make_async_copy(yb_ref.at[pl.ds(0, th)], bufs[b].at[k], sem.at[b]).wait()

    def finish(rows, b):
        rw = rw_ref[rows, :]
        x1 = x_ref[rows, :] + rw[:, 0:1] * bufs[b][0] + rw[:, 1:2] * bufs[b][1]
        x2 = _ple(x1, p_ref[rows, :], gple_ref[...], wpg_ref[...], wpp_ref[...])
        o_ref[rows, :] = _rms(x2, gfin_ref[...], RMS_EPS)

    @pl.when(i == 0)
    def _():
        gather(dest_ref, 0, 0)

    wait(0)
    gather(dest_ref, th, 1)
    finish(slice(0, th), 0)
    wait(1)
    gather(dest_next_ref, 0, 0)
    finish(slice(th, tp), 1)

    @pl.when(i == pl.num_programs(0) - 1)
    def _():
        wait(0)


def _combine_ple(dest_flat, x2, rw, p2, gple, wpg, wpp, gfin, yb, *, tp=512):
    T = x2.shape[0]
    n = T // tp
    return pl.pallas_call(
        functools.partial(_combine_ple_kernel, tp=tp),
        out_shape=jax.ShapeDtypeStruct((T, D_MODEL), F32),
        grid=(n,),
        in_specs=[
            pl.BlockSpec((2 * tp,), lambda i: (i,), memory_space=pltpu.SMEM),
            pl.BlockSpec((2 * tp,), lambda i: (jnp.minimum(i + 1, n - 1),), memory_space=pltpu.SMEM),
            pl.BlockSpec((tp, D_MODEL), lambda i: (i, 0)),
            pl.BlockSpec((tp, LANES), lambda i: (i, 0)),
            pl.BlockSpec((tp, PLE_DIM), lambda i: (i, 0)),
            _once((1, D_MODEL)),
            _once((D_MODEL, D_MODEL)),
            _once((PLE_DIM, D_MODEL)),
            _once((1, D_MODEL)),
            pl.BlockSpec(memory_space=pl.ANY),
        ],
        out_specs=pl.BlockSpec((tp, D_MODEL), lambda i: (i, 0)),
        scratch_shapes=[pltpu.VMEM((2, tp // 2, D_MODEL), F32), pltpu.VMEM((2, tp // 2, D_MODEL), F32),
                        pltpu.SemaphoreType.DMA((2,))],
        compiler_params=_params("arbitrary"),
        name="moe_combine_ple",
    )(dest_flat, dest_flat, x2, rw, p2, gple, wpg, wpp, gfin, yb)


def _moe_ple_final(x2, p2, gffn, w_router, b_router, wg, wu, wd, gple, wpg, wpp, gfin, *, tmm=512):
    T = x2.shape[0]
    wr = jnp.zeros((D_MODEL, LANES), F32).at[:, :N_EXPERTS].set(w_router)
    whi = wr.astype(BF16)
    wlo = (wr - whi.astype(F32)).astype(BF16)
    br = jnp.full((1, LANES), NEG_BIG, F32).at[0, :N_EXPERTS].set(b_router)
    h, ri, rw, cnt = _router(x2, gffn, whi, wlo, br)
    counts = cnt[0, :N_EXPERTS].astype(jnp.int32)
    padded = (counts + tmm - 1) // tmm * tmm
    pad_end = jnp.cumsum(padded).astype(jnp.int32)
    pad_start = pad_end - padded
    e_sel = ri[:, 0:2, None] == jnp.arange(N_EXPERTS, dtype=jnp.int32)
    dest = jnp.sum(jnp.where(e_sel, pad_start, 0), axis=-1) + ri[:, 2:4]
    dest_flat = dest.reshape(2 * T).astype(jnp.int32)
    n_blocks = (2 * T) // tmm + N_EXPERTS
    block_start = jnp.arange(n_blocks, dtype=jnp.int32) * tmm
    block_e = jnp.minimum(jnp.sum(block_start[:, None] >= pad_end[None, :], axis=1), N_EXPERTS - 1).astype(jnp.int32)
    n_used = (pad_end[-1] // tmm).astype(jnp.int32).reshape(1)
    xs = _dispatch(pad_end, n_used, dest_flat, h, n_blocks=n_blocks, tmm=tmm)
    yb = _gmm(block_e, n_used, xs, wg.astype(BF16), wu.astype(BF16), wd.astype(BF16), tmm=tmm)
    return _combine_ple(dest_flat, x2, rw, p2, gple, wpg, wpp, gfin, yb)


def kernel(x, p, g_mix, w_in, w_gk_up, b_gk, g_gla_norm, conv_w, conv_b, ln_conv_g, ln_conv_b, w_conv_out, w_pool, pool_scale, w_out, g_ffn, w_ffn_gate, w_ffn_up, w_ffn_down, w_router, b_router, w_moe_gate, w_moe_up, w_moe_down, g_ple, w_ple_gate, w_ple_proj, g_final):
    B, S, _ = x.shape
    T = B * S
    depth = w_in.shape[0]
    assert depth == 2 and x.shape[2] == D_MODEL, "dense layer 0, routed layer 1 with the final norm fused"
    x2 = x.reshape(T, D_MODEL)
    row = lambda a: a.reshape(1, -1)
    for i in range(depth):
        wi = w_in[i]
        w_main = jnp.concatenate([wi[:, :3072], wi[:, 3088:]], axis=1).astype(BF16)
        w_gk = jnp.zeros((D_MODEL, GK_PAD), F32).at[:, :GLA_GATE_RANK].set(wi[:, 3072:3088]).astype(BF16)
        w_gk_up_p = jnp.zeros((GK_PAD, GLA_KEY_DIM), F32).at[:GLA_GATE_RANK].set(w_gk_up[i]).astype(BF16)
        z, zgk = _in_proj(x2, row(g_mix[i]), w_main, w_gk)
        x2 = _mixer(z, zgk, x2, w_gk_up_p, row(b_gk[i]), row(g_gla_norm[i]), conv_w[i], row(conv_b[i]),
                    row(ln_conv_g[i]), row(ln_conv_b[i]), w_conv_out[i].astype(BF16), w_pool[i].astype(BF16),
                    row(pool_scale[i]), w_out[i].astype(BF16), B=B, S=S)
        p2 = p[i].reshape(T, PLE_DIM)
        wpg = w_ple_gate[i].astype(BF16)
        wpp = w_ple_proj[i].astype(BF16)
        if i % 2 == 0:
            j = i // 2
            wgu = jnp.concatenate([w_ffn_gate[j].astype(BF16), w_ffn_up[j].astype(BF16)], axis=1)
            x2 = _ffn_ple(x2, p2, row(g_ffn[i]), wgu, w_ffn_down[j].astype(BF16), row(g_ple[i]), wpg, wpp)
        else:
            j = i // 2
            x2 = _moe_ple_final(x2, p2, row(g_ffn[i]), w_router[j], b_router[j], w_moe_gate[j], w_moe_up[j],
                                w_moe_down[j], row(g_ple[i]), wpg, wpp, row(g_final))
    return x2.reshape(B, S, D_MODEL)
```

```python
import functools

import jax
import jax.numpy as jnp
from jax import lax
from jax.experimental import pallas as pl
from jax.experimental.pallas import tpu as pltpu

F32 = jnp.float32
BF16 = jnp.bfloat16

D_MODEL = 1024
GLA_HEADS = 4
GLA_KEY_DIM = 512
GLA_HEAD_K = 128
GLA_HEAD_V = 256
GLA_GATE_RANK = 16
GLA_GATE_NORMALIZER = 16.0
GLA_CHUNK = 64
GLA_NORM_EPS = 1e-5
CONV_WIDTH = 31
LN_EPS = 1e-5
POOL_WINDOWS = (2, 4, 8, 16)
POOL_GROUP_CH = 256
N_EXPERTS = 8
PLE_DIM = 256
RMS_EPS = 1e-6

Z_Q, Z_K, Z_V, Z_G, Z_CA, Z_CG, Z_POOL, Z_GATE = 0, 512, 1024, 2048, 3072, 4096, 5120, 6144
Z_DIM = 9216
GK_PAD = 128
LANES = 128

CONV_HALO = 32
POOL_HALO = 128
CONV_RC = 64
VMEM_LIMIT = 56 * 1024 * 1024
GMM_VMEM_LIMIT = 60 * 1024 * 1024

NEG_BIG = -1e30
MXU_PASS_COLS = 512
ROW_TILE = (D_MODEL // LANES, LANES)


def _dot(a, b):
    return jnp.dot(a, b, preferred_element_type=F32)


def _dot_nt(a, b):
    return lax.dot_general(a, b, (((1,), (1,)), ((), ())), preferred_element_type=F32)


def _dot_tn(a, b):
    return lax.dot_general(a, b, (((0,), (0,)), ((), ())), preferred_element_type=F32)


def _sigmoid(x):
    return 0.5 * jnp.tanh(0.5 * x) + 0.5


def _rms(x, g, eps):
    return x * lax.rsqrt(jnp.mean(x * x, axis=-1, keepdims=True) + eps) * g


def _split3(x):
    hi = x.astype(BF16)
    r1 = x - hi.astype(F32)
    mid = r1.astype(BF16)
    lo = (r1 - mid.astype(F32)).astype(BF16)
    return hi, mid, lo


def _once(shape):
    n = len(shape)
    return pl.BlockSpec(shape, lambda *_: (0,) * n, pipeline_mode=pl.Buffered(1))


def _params(*semantics):
    return pltpu.CompilerParams(dimension_semantics=semantics, vmem_limit_bytes=VMEM_LIMIT)


def _in_proj_kernel(x_ref, g_ref, w_ref, wgk_ref, z_ref, zgk_ref, *, nc):
    h = _rms(x_ref[...], g_ref[...], RMS_EPS).astype(BF16)
    for c in range(0, Z_DIM, nc):
        z_ref[:, c:c + nc] = _dot(h, w_ref[:, c:c + nc]).astype(BF16)
    zgk_ref[...] = _dot(h, wgk_ref[...])


def _in_proj(x2, g, w_main, w_gk, *, tm=512, nc=1024):
    T = x2.shape[0]
    return pl.pallas_call(
        functools.partial(_in_proj_kernel, nc=nc),
        out_shape=(jax.ShapeDtypeStruct((T, Z_DIM), BF16), jax.ShapeDtypeStruct((T, GK_PAD), F32)),
        grid=(T // tm,),
        in_specs=[
            pl.BlockSpec((tm, D_MODEL), lambda i: (i, 0)),
            _once((1, D_MODEL)),
            _once((D_MODEL, Z_DIM)),
            _once((D_MODEL, GK_PAD)),
        ],
        out_specs=(pl.BlockSpec((tm, Z_DIM), lambda i: (i, 0)), pl.BlockSpec((tm, GK_PAD), lambda i: (i, 0))),
        compiler_params=_params("arbitrary"),
        name="in_proj",
    )(x2, g, w_main, w_gk)


def _mixer_kernel(z_ref, zgk_ref, x_ref, wgk_ref, bgk_ref, gnorm_ref, convw_ref, convb_ref, lng_ref, lnb_ref,
                  wco_ref, wpool_ref, pscale_ref, wout_ref, o_ref,
                  state_ref, ubuf_ref, pbuf_ref, ygla_ref, yconv_ref, *, ts):
    s = pl.program_id(1)
    C = GLA_CHUNK

    @pl.when(s == 0)
    def _():
        state_ref[...] = jnp.zeros_like(state_ref)
        ubuf_ref[0:CONV_HALO, :] = jnp.zeros((CONV_HALO, D_MODEL), F32)
        pbuf_ref[0:POOL_HALO, :] = jnp.zeros((POOL_HALO, D_MODEL), BF16)

    gk_pre = _dot(zgk_ref[...].astype(BF16), wgk_ref[...]) + bgk_ref[...]
    gk = (jnp.minimum(gk_pre, 0.0) - jnp.log(1.0 + jnp.exp(-jnp.abs(gk_pre)))) * (1.0 / GLA_GATE_NORMALIZER)
    row = lax.broadcasted_iota(jnp.int32, (ts, ts), 0)
    col = lax.broadcasted_iota(jnp.int32, (ts, ts), 1)
    same_chunk_causal = (jnp.right_shift(row, 6) == jnp.right_shift(col, 6)) & (col <= row)
    tri = same_chunk_causal.astype(BF16)
    g_hi, g_mid, g_lo = _split3(gk)
    b_all = _dot(tri, g_hi) + _dot(tri, g_mid) + _dot(tri, g_lo)
    nchunk = ts // C

    def per_chunk_row(r):
        return jnp.concatenate(
            [jnp.broadcast_to(b_all[c * C + r:c * C + r + 1, :], (C, GLA_KEY_DIM)) for c in range(nchunk)], axis=0)

    b_mid = per_chunk_row(C // 2 - 1)
    b_last = per_chunk_row(C - 1)
    q = z_ref[:, Z_Q:Z_Q + GLA_KEY_DIM].astype(F32) * (GLA_HEAD_K ** -0.5)
    k = z_ref[:, Z_K:Z_K + GLA_KEY_DIM].astype(F32)
    qa = (q * jnp.exp(b_all - b_mid)).astype(BF16)
    ka = (k * jnp.exp(b_mid - b_all)).astype(BF16)
    qd = (q * jnp.exp(b_all)).astype(BF16)
    kd = (k * jnp.exp(b_last - b_all)).astype(BF16)
    e_dec = [jnp.exp(b_all[c * C + C - 1:c * C + C, :]) for c in range(nchunk)]
    gnorm = gnorm_ref[...]
    for h in range(GLA_HEADS):
        ks = slice(h * GLA_HEAD_K, (h + 1) * GLA_HEAD_K)
        v = z_ref[:, Z_V + h * GLA_HEAD_V:Z_V + (h + 1) * GLA_HEAD_V]
        attn = jnp.where(same_chunk_causal, _dot_nt(qa[:, ks], ka[:, ks]), 0.0).astype(BF16)
        o = _dot(attn, v)
        st = state_ref[h]
        o_inter = []
        for c in range(nchunk):
            rs = slice(c * C, (c + 1) * C)
            o_inter.append(_dot_nt(qd[rs, ks], st.astype(BF16)))
            st = st * e_dec[c][:, ks] + _dot_tn(v[rs, :], kd[rs, ks])
        state_ref[h] = st
        o = o + jnp.concatenate(o_inter, axis=0)
        o = o * lax.rsqrt(jnp.mean(o * o, axis=-1, keepdims=True) + GLA_NORM_EPS) * gnorm
        gi = z_ref[:, Z_G + h * GLA_HEAD_V:Z_G + (h + 1) * GLA_HEAD_V].astype(F32)
        ygla_ref[:, h * GLA_HEAD_V:(h + 1) * GLA_HEAD_V] = o * (gi * _sigmoid(gi))

    ca = z_ref[:, Z_CA:Z_CA + D_MODEL].astype(F32)
    cg = z_ref[:, Z_CG:Z_CG + D_MODEL].astype(F32)
    ubuf_ref[CONV_HALO:CONV_HALO + ts, :] = ca * _sigmoid(cg)
    RC = CONV_RC

    def conv_rows(i, carry):
        base = pl.multiple_of(i * RC, RC)
        for cc in range(D_MODEL // LANES):
            cs = slice(cc * LANES, (cc + 1) * LANES)
            acc = None
            for r in range(8):
                n_rows = RC if r == 0 else RC + 8
                part = None
                for a in range(5):
                    o = 8 * a + r
                    if o < 2 or o > CONV_HALO:
                        continue
                    kk = o - 2
                    term = convw_ref[kk:kk + 1, cs] * ubuf_ref[pl.ds(base + 8 * a, n_rows), cs]
                    part = term if part is None else part + term
                part = part[r:r + RC, :]
                acc = part if acc is None else acc + part
            yconv_ref[pl.ds(base, RC), cs] = acc + convb_ref[:, cs]
        return carry

    lax.fori_loop(0, ts // RC, conv_rows, 0)
    ubuf_ref[0:CONV_HALO, :] = ubuf_ref[ts:ts + CONV_HALO, :]
    yc = yconv_ref[...]
    mu = jnp.mean(yc, axis=-1, keepdims=True)
    ycc = yc - mu
    var = jnp.mean(ycc * ycc, axis=-1, keepdims=True)
    yn = ycc * lax.rsqrt(var + LN_EPS) * lng_ref[...] + lnb_ref[...]
    y_conv = _dot((yn * _sigmoid(yn)).astype(BF16), wco_ref[...])

    pbuf_ref[POOL_HALO:POOL_HALO + ts, :] = z_ref[:, Z_POOL:Z_POOL + D_MODEL]
    tpos = s * ts + lax.broadcasted_iota(jnp.int32, (ts, 1), 0)
    prow = lax.broadcasted_iota(jnp.int32, (ts, POOL_HALO + ts), 0) + POOL_HALO
    pcol = lax.broadcasted_iota(jnp.int32, (ts, POOL_HALO + ts), 1)
    y_pool = []
    for gi_, w in enumerate(POOL_WINDOWS):
        cs = slice(gi_ * POOL_GROUP_CH, (gi_ + 1) * POOL_GROUP_CH)
        band = ((pcol <= prow) & (pcol > prow - w)).astype(BF16)
        wsum = _dot(band, pbuf_ref[:, cs])
        cnt = jnp.minimum(tpos + 1, w).astype(F32)
        d = wsum / cnt - z_ref[:, Z_POOL + gi_ * POOL_GROUP_CH:Z_POOL + (gi_ + 1) * POOL_GROUP_CH].astype(F32)
        y_pool.append(_dot(d.astype(BF16), wpool_ref[gi_]) * pscale_ref[:, cs])
    pbuf_ref[0:POOL_HALO, :] = pbuf_ref[ts:ts + POOL_HALO, :]

    mix_parts = []
    for gi_ in range(4):
        cs = slice(gi_ * POOL_GROUP_CH, (gi_ + 1) * POOL_GROUP_CH)
        g0 = _sigmoid(z_ref[:, Z_GATE + gi_ * 256:Z_GATE + (gi_ + 1) * 256])
        g1 = _sigmoid(z_ref[:, Z_GATE + D_MODEL + gi_ * 256:Z_GATE + D_MODEL + (gi_ + 1) * 256])
        g2 = _sigmoid(z_ref[:, Z_GATE + 2 * D_MODEL + gi_ * 256:Z_GATE + 2 * D_MODEL + (gi_ + 1) * 256])
        m = g0 * ygla_ref[:, cs].astype(BF16) + g1 * y_conv[:, cs].astype(BF16) + g2 * y_pool[gi_].astype(BF16)
        mix_parts.append(m)
    o_ref[...] = x_ref[...] + _dot(jnp.concatenate(mix_parts, axis=1), wout_ref[...])


def _mixer(z, zgk, x2, wgk, bgk, gnorm, convw, convb, lng, lnb, wco, wpool, pscale, wout, *, B, S, ts=256):
    T = B * S
    nt = S // ts
    rowmap = lambda b, s: (b * nt + s, 0)
    return pl.pallas_call(
        functools.partial(_mixer_kernel, ts=ts),
        out_shape=jax.ShapeDtypeStruct((T, D_MODEL), F32),
        grid=(B, nt),
        in_specs=[
            pl.BlockSpec((ts, Z_DIM), rowmap),
            pl.BlockSpec((ts, GK_PAD), rowmap),
            pl.BlockSpec((ts, D_MODEL), rowmap),
            _once((GK_PAD, GLA_KEY_DIM)),
            _once((1, GLA_KEY_DIM)),
            _once((1, GLA_HEAD_V)),
            _once((CONV_WIDTH, D_MODEL)),
            _once((1, D_MODEL)),
            _once((1, D_MODEL)),
            _once((1, D_MODEL)),
            _once((D_MODEL, D_MODEL)),
            _once((4, POOL_GROUP_CH, POOL_GROUP_CH)),
            _once((1, D_MODEL)),
            _once((D_MODEL, D_MODEL)),
        ],
        out_specs=pl.BlockSpec((ts, D_MODEL), rowmap),
        scratch_shapes=[
            pltpu.VMEM((GLA_HEADS, GLA_HEAD_V, GLA_HEAD_K), F32),
            pltpu.VMEM((CONV_HALO + ts, D_MODEL), F32),
            pltpu.VMEM((POOL_HALO + ts, D_MODEL), BF16),
            pltpu.VMEM((ts, D_MODEL), F32),
            pltpu.VMEM((ts, D_MODEL), F32),
        ],
        compiler_params=_params("arbitrary", "arbitrary"),
        name="token_mixer",
    )(z, zgk, x2, wgk, bgk, gnorm, convw, convb, lng, lnb, wco, wpool, pscale, wout)


def _ple(x1, p, gple, wpg, wpp):
    hp = _rms(x1, gple, RMS_EPS).astype(BF16)
    return x1 + _sigmoid(_dot(hp, wpg)) * _dot(p.astype(BF16), wpp)


def _ffn_ple_kernel(x_ref, p_ref, gffn_ref, wgu_ref, wd_ref, gple_ref, wpg_ref, wpp_ref, o_ref):
    x = x_ref[...]
    h = _rms(x, gffn_ref[...], RMS_EPS).astype(BF16)
    F = wd_ref.shape[0]
    gu = _dot(h, wgu_ref[...])
    g = gu[:, :F]
    u = gu[:, F:]
    y = x + _dot((g * _sigmoid(g) * u).astype(BF16), wd_ref[...])
    o_ref[...] = _ple(y, p_ref[...], gple_ref[...], wpg_ref[...], wpp_ref[...])


def _ffn_ple(x2, p2, gffn, wgu, wd, gple, wpg, wpp, *, tm=512):
    T = x2.shape[0]
    F = wd.shape[0]
    return pl.pallas_call(
        _ffn_ple_kernel,
        out_shape=jax.ShapeDtypeStruct((T, D_MODEL), F32),
        grid=(T // tm,),
        in_specs=[
            pl.BlockSpec((tm, D_MODEL), lambda i: (i, 0)),
            pl.BlockSpec((tm, PLE_DIM), lambda i: (i, 0)),
            _once((1, D_MODEL)),
            _once((D_MODEL, 2 * F)),
            _once((F, D_MODEL)),
            _once((1, D_MODEL)),
            _once((D_MODEL, D_MODEL)),
            _once((PLE_DIM, D_MODEL)),
        ],
        out_specs=pl.BlockSpec((tm, D_MODEL), lambda i: (i, 0)),
        compiler_params=_params("arbitrary"),
        name="ffn_ple",
    )(x2, p2, gffn, wgu, wd, gple, wpg, wpp)


def _router_kernel(x_ref, g_ref, whi_ref, wlo_ref, br_ref, h_ref, ri_ref, rw_ref, cnt_ref, run_ref, *, tm):
    i = pl.program_id(0)

    @pl.when(i == 0)
    def _():
        run_ref[...] = jnp.zeros_like(run_ref)

    h = _rms(x_ref[...], g_ref[...], RMS_EPS)
    h_ref[...] = h.reshape((tm,) + ROW_TILE)
    hh = h.astype(BF16)
    hl = (h - hh.astype(F32)).astype(BF16)
    logits = _dot(hh, whi_ref[...]) + _dot(hh, wlo_ref[...]) + _dot(hl, whi_ref[...]) + br_ref[...]
    lane = lax.broadcasted_iota(jnp.int32, (tm, LANES), 1)
    m1 = jnp.max(logits, axis=-1, keepdims=True)
    i1 = jnp.min(jnp.where(logits == m1, lane, LANES), axis=-1, keepdims=True)
    l2 = jnp.where(lane == i1, -jnp.inf, logits)
    m2 = jnp.max(l2, axis=-1, keepdims=True)
    i2 = jnp.min(jnp.where(l2 == m2, lane, LANES), axis=-1, keepdims=True)
    e = jnp.exp(m2 - m1)
    w1 = 1.0 / (1.0 + e)
    w2 = e / (1.0 + e)
    sel1 = lane == i1
    sel2 = lane == i2
    onehot = (sel1 | sel2).astype(BF16)
    row = lax.broadcasted_iota(jnp.int32, (tm, tm), 0)
    col = lax.broadcasted_iota(jnp.int32, (tm, tm), 1)
    before = _dot((col < row).astype(BF16), onehot) + run_ref[...]
    pos1 = jnp.sum(jnp.where(sel1, before, 0.0), axis=-1, keepdims=True).astype(jnp.int32)
    pos2 = jnp.sum(jnp.where(sel2, before, 0.0), axis=-1, keepdims=True).astype(jnp.int32)
    run = run_ref[...] + jnp.sum(onehot.astype(F32), axis=0, keepdims=True)
    run_ref[...] = run
    cnt_ref[...] = run
    ri_ref[...] = jnp.where(lane == 0, i1, jnp.where(lane == 1, i2, jnp.where(lane == 2, pos1, jnp.where(lane == 3, pos2, 0))))
    rw_ref[...] = jnp.where(lane == 0, w1, jnp.where(lane == 1, w2, 0.0))


def _router(x2, g, whi, wlo, br, *, tm=512):
    T = x2.shape[0]
    return pl.pallas_call(
        functools.partial(_router_kernel, tm=tm),
        out_shape=(
            jax.ShapeDtypeStruct((T,) + ROW_TILE, F32),
            jax.ShapeDtypeStruct((T, LANES), jnp.int32),
            jax.ShapeDtypeStruct((T, LANES), F32),
            jax.ShapeDtypeStruct((1, LANES), F32),
        ),
        grid=(T // tm,),
        in_specs=[
            pl.BlockSpec((tm, D_MODEL), lambda i: (i, 0)),
            _once((1, D_MODEL)),
            _once((D_MODEL, LANES)),
            _once((D_MODEL, LANES)),
            _once((1, LANES)),
        ],
        out_specs=(
            pl.BlockSpec((tm,) + ROW_TILE, lambda i: (i, 0, 0)),
            pl.BlockSpec((tm, LANES), lambda i: (i, 0)),
            pl.BlockSpec((tm, LANES), lambda i: (i, 0)),
            pl.BlockSpec((1, LANES), lambda i: (0, 0)),
        ),
        scratch_shapes=[pltpu.VMEM((1, LANES), F32)],
        compiler_params=_params("arbitrary"),
        name="router",
    )(x2, g, whi, wlo, br)


def _row_copy(src_ref, src_row, dst_ref, dst_row, sem):
    return pltpu.make_async_copy(src_ref.at[pl.ds(src_row, 1)], dst_ref.at[pl.ds(dst_row, 1)], sem)


def _dispatch_kernel(pe_ref, nu_ref, dest_ref, h_ref, xs_ref, zero_ref, sem, *, td, tmm, n_blocks):
    @pl.when(pl.program_id(0) == 0)
    def _():
        zero_ref[...] = jnp.zeros_like(zero_ref)

        def fill(start):
            return pltpu.make_async_copy(zero_ref, xs_ref.at[pl.ds(pl.multiple_of(start, tmm), tmm)], sem)

        for e in range(N_EXPERTS):
            prev_end = pe_ref[e - 1] if e > 0 else 0

            @pl.when(pe_ref[e] > prev_end)
            def _():
                fill(pe_ref[e] - tmm).start()
                fill(pe_ref[e] - tmm).wait()

        for j in range(N_EXPERTS):
            @pl.when(n_blocks - 1 - j >= nu_ref[0])
            def _():
                fill((n_blocks - 1 - j) * tmm).start()
                fill((n_blocks - 1 - j) * tmm).wait()

    for r in range(td):
        _row_copy(h_ref, r, xs_ref, dest_ref[2 * r], sem).start(priority=0)
        _row_copy(h_ref, r, xs_ref, dest_ref[2 * r + 1], sem).start(priority=1)
    for _ in range(2):
        pltpu.make_async_copy(h_ref, xs_ref.at[pl.ds(0, td)], sem).wait()


def _dispatch(pad_end, n_used, dest_flat, h, *, n_blocks, tmm, td=512):
    T = h.shape[0]
    return pl.pallas_call(
        functools.partial(_dispatch_kernel, td=td, tmm=tmm, n_blocks=n_blocks),
        out_shape=jax.ShapeDtypeStruct((n_blocks * tmm,) + ROW_TILE, F32),
        grid_spec=pltpu.PrefetchScalarGridSpec(
            num_scalar_prefetch=2,
            grid=(T // td,),
            in_specs=[
                pl.BlockSpec((2 * td,), lambda i, pe, nu: (i,), memory_space=pltpu.SMEM),
                pl.BlockSpec((td,) + ROW_TILE, lambda i, pe, nu: (i, 0, 0)),
            ],
            out_specs=pl.BlockSpec(memory_space=pl.ANY),
            scratch_shapes=[pltpu.VMEM((tmm,) + ROW_TILE, F32), pltpu.SemaphoreType.DMA],
        ),
        compiler_params=_params("arbitrary"),
        name="moe_dispatch",
    )(pad_end, n_used, dest_flat, h)


def _gmm_kernel(be_ref, nu_ref, xs_ref, wg_ref, wu_ref, wd_ref, o_ref, *, f_chunks):
    i = pl.program_id(0)

    @pl.when(i < nu_ref[0])
    def _():
        x = xs_ref[...].reshape(xs_ref.shape[0], D_MODEL).astype(BF16)
        y = None
        for lo, hi in f_chunks:
            g = _dot(x, wg_ref[0, :, lo:hi])
            u = _dot(x, wu_ref[0, :, lo:hi])
            part = _dot((g * _sigmoid(g) * u).astype(BF16), wd_ref[0, lo:hi, :])
            y = part if y is None else y + part
        o_ref[...] = y.reshape(o_ref.shape)

    @pl.when(i >= nu_ref[0])
    def _():
        o_ref[...] = jnp.zeros_like(o_ref)


def _gmm(block_e, n_used, xs, wg, wu, wd, *, tmm):
    n_rows = xs.shape[0]
    n_blocks = n_rows // tmm
    F = wd.shape[1]
    n_pass = F // MXU_PASS_COLS
    assert n_pass * MXU_PASS_COLS == F
    cut = (n_pass // 2) * MXU_PASS_COLS
    f_chunks = ((0, cut), (cut, F))

    def expert_spec(shape):
        return pl.BlockSpec((1,) + shape, lambda i, be, nu: (be[i], 0, 0))

    return pl.pallas_call(
        functools.partial(_gmm_kernel, f_chunks=f_chunks),
        out_shape=jax.ShapeDtypeStruct((n_rows,) + ROW_TILE, F32),
        grid_spec=pltpu.PrefetchScalarGridSpec(
            num_scalar_prefetch=2,
            grid=(n_blocks,),
            in_specs=[
                pl.BlockSpec((tmm,) + ROW_TILE, lambda i, be, nu: (jnp.minimum(i, nu[0] - 1), 0, 0)),
                expert_spec((D_MODEL, F)),
                expert_spec((D_MODEL, F)),
                expert_spec((F, D_MODEL)),
            ],
            out_specs=pl.BlockSpec((tmm,) + ROW_TILE, lambda i, be, nu: (i, 0, 0)),
        ),
        compiler_params=pltpu.CompilerParams(dimension_semantics=("arbitrary",), vmem_limit_bytes=GMM_VMEM_LIMIT),
        name="moe_grouped_swiglu",
    )(block_e, n_used, xs, wg, wu, wd)


def _combine_ple_kernel(dest_ref, dest_next_ref, x_ref, rw_ref, p_ref, gple_ref, wpg_ref, wpp_ref, gfin_ref,
                        yb_ref, o_ref, bufa_ref, bufb_ref, sem, *, tp):
    i = pl.program_id(0)
    th = tp // 2
    bufs = (bufa_ref, bufb_ref)

    def gather(d_ref, first, b):
        for r in range(th):
            _row_copy(yb_ref, d_ref[2 * (first + r)], bufs[b].at[0], r, sem.at[b]).start(priority=0)
            _row_copy(yb_ref, d_ref[2 * (first + r) + 1], bufs[b].at[1], r, sem.at[b]).start(priority=1)

    def wait(b):
        for k in range(2):
            pltpu.make_async_copy(yb_ref.at[pl.ds(0, th)], bufs[b].at[k], sem.at[b]).wait()

    def finish(rows, b):
        rw = rw_ref[rows, :]
        y0 = bufs[b][0].reshape(th, D_MODEL)
        y1 = bufs[b][1].reshape(th, D_MODEL)
        x1 = x_ref[rows, :] + rw[:, 0:1] * y0 + rw[:, 1:2] * y1
        x2 = _ple(x1, p_ref[rows, :], gple_ref[...], wpg_ref[...], wpp_ref[...])
        o_ref[rows, :] = _rms(x2, gfin_ref[...], RMS_EPS)

    @pl.when(i == 0)
    def _():
        gather(dest_ref, 0, 0)

    wait(0)
    gather(dest_ref, th, 1)
    finish(slice(0, th), 0)
    wait(1)
    gather(dest_next_ref, 0, 0)
    finish(slice(th, tp), 1)

    @pl.when(i == pl.num_programs(0) - 1)
    def _():
        wait(0)


def _combine_ple(dest_flat, x2, rw, p2, gple, wpg, wpp, gfin, yb, *, tp=512):
    T = x2.shape[0]
    n = T // tp
    return pl.pallas_call(
        functools.partial(_combine_ple_kernel, tp=tp),
        out_shape=jax.ShapeDtypeStruct((T, D_MODEL), F32),
        grid=(n,),
        in_specs=[
            pl.BlockSpec((2 * tp,), lambda i: (i,), memory_space=pltpu.SMEM),
            pl.BlockSpec((2 * tp,), lambda i: (jnp.minimum(i + 1, n - 1),), memory_space=pltpu.SMEM),
            pl.BlockSpec((tp, D_MODEL), lambda i: (i, 0)),
            pl.BlockSpec((tp, LANES), lambda i: (i, 0)),
            pl.BlockSpec((tp, PLE_DIM), lambda i: (i, 0)),
            _once((1, D_MODEL)),
            _once((D_MODEL, D_MODEL)),
            _once((PLE_DIM, D_MODEL)),
            _once((1, D_MODEL)),
            pl.BlockSpec(memory_space=pl.ANY),
        ],
        out_specs=pl.BlockSpec((tp, D_MODEL), lambda i: (i, 0)),
        scratch_shapes=[pltpu.VMEM((2, tp // 2) + ROW_TILE, F32), pltpu.VMEM((2, tp // 2) + ROW_TILE, F32),
                        pltpu.SemaphoreType.DMA((2,))],
        compiler_params=_params("arbitrary"),
        name="moe_combine_ple",
    )(dest_flat, dest_flat, x2, rw, p2, gple, wpg, wpp, gfin, yb)


def _moe_ple_final(x2, p2, gffn, w_router, b_router, wg, wu, wd, gple, wpg, wpp, gfin, *, tmm=512):
    T = x2.shape[0]
    wr = jnp.zeros((D_MODEL, LANES), F32).at[:, :N_EXPERTS].set(w_router)
    whi = wr.astype(BF16)
    wlo = (wr - whi.astype(F32)).astype(BF16)
    br = jnp.full((1, LANES), NEG_BIG, F32).at[0, :N_EXPERTS].set(b_router)
    h, ri, rw, cnt = _router(x2, gffn, whi, wlo, br)
    counts = cnt[0, :N_EXPERTS].astype(jnp.int32)
    padded = (counts + tmm - 1) // tmm * tmm
    pad_end = jnp.cumsum(padded).astype(jnp.int32)
    pad_start = pad_end - padded
    e_sel = ri[:, 0:2, None] == jnp.arange(N_EXPERTS, dtype=jnp.int32)
    dest = jnp.sum(jnp.where(e_sel, pad_start, 0), axis=-1) + ri[:, 2:4]
    dest_flat = dest.reshape(2 * T).astype(jnp.int32)
    n_blocks = (2 * T) // tmm + N_EXPERTS
    block_start = jnp.arange(n_blocks, dtype=jnp.int32) * tmm
    block_e = jnp.minimum(jnp.sum(block_start[:, None] >= pad_end[None, :], axis=1), N_EXPERTS - 1).astype(jnp.int32)
    n_used = (pad_end[-1] // tmm).astype(jnp.int32).reshape(1)
    xs = _dispatch(pad_end, n_used, dest_flat, h, n_blocks=n_blocks, tmm=tmm)
    yb = _gmm(block_e, n_used, xs, wg.astype(BF16), wu.astype(BF16), wd.astype(BF16), tmm=tmm)
    return _combine_ple(dest_flat, x2, rw, p2, gple, wpg, wpp, gfin, yb)


def kernel(x, p, g_mix, w_in, w_gk_up, b_gk, g_gla_norm, conv_w, conv_b, ln_conv_g, ln_conv_b, w_conv_out, w_pool, pool_scale, w_out, g_ffn, w_ffn_gate, w_ffn_up, w_ffn_down, w_router, b_router, w_moe_gate, w_moe_up, w_moe_down, g_ple, w_ple_gate, w_ple_proj, g_final):
    B, S, _ = x.shape
    T = B * S
    depth = w_in.shape[0]
    assert depth == 2 and x.shape[2] == D_MODEL, "dense layer 0, routed layer 1 with the final norm fused"
    x2 = x.reshape(T, D_MODEL)
    row = lambda a: a.reshape(1, -1)
    for i in range(depth):
        wi = w_in[i]
        w_main = jnp.concatenate([wi[:, :3072], wi[:, 3088:]], axis=1).astype(BF16)
        w_gk = jnp.zeros((D_MODEL, GK_PAD), F32).at[:, :GLA_GATE_RANK].set(wi[:, 3072:3088]).astype(BF16)
        w_gk_up_p = jnp.zeros((GK_PAD, GLA_KEY_DIM), F32).at[:GLA_GATE_RANK].set(w_gk_up[i]).astype(BF16)
        z, zgk = _in_proj(x2, row(g_mix[i]), w_main, w_gk)
        x2 = _mixer(z, zgk, x2, w_gk_up_p, row(b_gk[i]), row(g_gla_norm[i]), conv_w[i], row(conv_b[i]),
                    row(ln_conv_g[i]), row(ln_conv_b[i]), w_conv_out[i].astype(BF16), w_pool[i].astype(BF16),
                    row(pool_scale[i]), w_out[i].astype(BF16), B=B, S=S)
        p2 = p[i].reshape(T, PLE_DIM)
        wpg = w_ple_gate[i].astype(BF16)
        wpp = w_ple_proj[i].astype(BF16)
        if i % 2 == 0:
            j = i // 2
            wgu = jnp.concatenate([w_ffn_gate[j].astype(BF16), w_ffn_up[j].astype(BF16)], axis=1)
            x2 = _ffn_ple(x2, p2, row(g_ffn[i]), wgu, w_ffn_down[j].astype(BF16), row(g_ple[i]), wpg, wpp)
        else:
            j = i // 2
            x2 = _moe_ple_final(x2, p2, row(g_ffn[i]), w_router[j], b_router[j], w_moe_gate[j], w_moe_up[j],
                                w_moe_down[j], row(g_ple[i]), wpg, wpp, row(g_final))
    return x2.reshape(B, S, D_MODEL)
```

```python
import functools

import jax
import jax.numpy as jnp
from jax import lax
from jax.experimental import pallas as pl
from jax.experimental.pallas import tpu as pltpu

F32 = jnp.float32
BF16 = jnp.bfloat16

D_MODEL = 1024
GLA_HEADS = 4
GLA_KEY_DIM = 512
GLA_HEAD_K = 128
GLA_HEAD_V = 256
GLA_GATE_RANK = 16
GLA_GATE_NORMALIZER = 16.0
GLA_CHUNK = 64
GLA_NORM_EPS = 1e-5
CONV_WIDTH = 31
LN_EPS = 1e-5
POOL_WINDOWS = (2, 4, 8, 16)
POOL_GROUP_CH = 256
N_EXPERTS = 8
PLE_DIM = 256
RMS_EPS = 1e-6

Z_Q, Z_K, Z_V, Z_G, Z_CA, Z_CG, Z_POOL, Z_GATE = 0, 512, 1024, 2048, 3072, 4096, 5120, 6144
Z_DIM = 9216
GK_PAD = 128
LANES = 128

CONV_HALO = 32
POOL_HALO = 128
CONV_RC = 128
VMEM_LIMIT = 56 * 1024 * 1024
GMM_VMEM_LIMIT = 60 * 1024 * 1024

NEG_BIG = -1e30
MXU_PASS_COLS = 512
ROW_TILE = (D_MODEL // LANES, LANES)


def _dot(a, b):
    return jnp.dot(a, b, preferred_element_type=F32)


def _dot_nt(a, b):
    return lax.dot_general(a, b, (((1,), (1,)), ((), ())), preferred_element_type=F32)


def _dot_tn(a, b):
    return lax.dot_general(a, b, (((0,), (0,)), ((), ())), preferred_element_type=F32)


def _sigmoid(x):
    return 0.5 * jnp.tanh(0.5 * x) + 0.5


def _rms(x, g, eps):
    return x * lax.rsqrt(jnp.mean(x * x, axis=-1, keepdims=True) + eps) * g


def _split3(x):
    hi = x.astype(BF16)
    r1 = x - hi.astype(F32)
    mid = r1.astype(BF16)
    lo = (r1 - mid.astype(F32)).astype(BF16)
    return hi, mid, lo


def _once(shape):
    n = len(shape)
    return pl.BlockSpec(shape, lambda *_: (0,) * n, pipeline_mode=pl.Buffered(1))


def _params(*semantics):
    return pltpu.CompilerParams(dimension_semantics=semantics, vmem_limit_bytes=VMEM_LIMIT)


def _in_proj_kernel(x_ref, g_ref, w_ref, wgk_ref, z_ref, zgk_ref, *, nc):
    h = _rms(x_ref[...], g_ref[...], RMS_EPS).astype(BF16)
    for c in range(0, Z_DIM, nc):
        z_ref[:, c:c + nc] = _dot(h, w_ref[:, c:c + nc]).astype(BF16)
    zgk_ref[...] = _dot(h, wgk_ref[...])


def _in_proj(x2, g, w_main, w_gk, *, tm=512, nc=1024):
    T = x2.shape[0]
    return pl.pallas_call(
        functools.partial(_in_proj_kernel, nc=nc),
        out_shape=(jax.ShapeDtypeStruct((T, Z_DIM), BF16), jax.ShapeDtypeStruct((T, GK_PAD), F32)),
        grid=(T // tm,),
        in_specs=[
            pl.BlockSpec((tm, D_MODEL), lambda i: (i, 0)),
            _once((1, D_MODEL)),
            _once((D_MODEL, Z_DIM)),
            _once((D_MODEL, GK_PAD)),
        ],
        out_specs=(pl.BlockSpec((tm, Z_DIM), lambda i: (i, 0)), pl.BlockSpec((tm, GK_PAD), lambda i: (i, 0))),
        compiler_params=_params("arbitrary"),
        name="in_proj",
    )(x2, g, w_main, w_gk)


def _mixer_kernel(z_ref, zgk_ref, x_ref, wgk_ref, bgk_ref, gnorm_ref, convw_ref, convb_ref, lng_ref, lnb_ref,
                  wco_ref, wpool_ref, pscale_ref, wout_ref, o_ref,
                  state_ref, ubuf_ref, pbuf_ref, ygla_ref, yconv_ref, *, ts):
    s = pl.program_id(1)
    C = GLA_CHUNK

    @pl.when(s == 0)
    def _():
        state_ref[...] = jnp.zeros_like(state_ref)
        ubuf_ref[0:CONV_HALO, :] = jnp.zeros((CONV_HALO, D_MODEL), F32)
        pbuf_ref[0:POOL_HALO, :] = jnp.zeros((POOL_HALO, D_MODEL), BF16)

    gk_pre = _dot(zgk_ref[...].astype(BF16), wgk_ref[...]) + bgk_ref[...]
    ca = z_ref[:, Z_CA:Z_CA + D_MODEL].astype(F32)
    cg = z_ref[:, Z_CG:Z_CG + D_MODEL].astype(F32)
    ubuf_ref[CONV_HALO:CONV_HALO + ts, :] = ca * _sigmoid(cg)
    gk = (jnp.minimum(gk_pre, 0.0) - jnp.log(1.0 + jnp.exp(-jnp.abs(gk_pre)))) * (1.0 / GLA_GATE_NORMALIZER)
    row = lax.broadcasted_iota(jnp.int32, (ts, ts), 0)
    col = lax.broadcasted_iota(jnp.int32, (ts, ts), 1)
    same_chunk_causal = (jnp.right_shift(row, 6) == jnp.right_shift(col, 6)) & (col <= row)
    tri = same_chunk_causal.astype(BF16)
    g_hi, g_mid, g_lo = _split3(gk)
    b_all = _dot(tri, g_hi) + _dot(tri, g_mid) + _dot(tri, g_lo)
    nchunk = ts // C

    pbuf_ref[POOL_HALO:POOL_HALO + ts, :] = z_ref[:, Z_POOL:Z_POOL + D_MODEL]
    tpos = s * ts + lax.broadcasted_iota(jnp.int32, (ts, 1), 0)
    prow = lax.broadcasted_iota(jnp.int32, (ts, POOL_HALO + ts), 0) + POOL_HALO
    pcol = lax.broadcasted_iota(jnp.int32, (ts, POOL_HALO + ts), 1)
    y_pool = []
    for gi_, w in enumerate(POOL_WINDOWS):
        cs = slice(gi_ * POOL_GROUP_CH, (gi_ + 1) * POOL_GROUP_CH)
        band = ((pcol <= prow) & (pcol > prow - w)).astype(BF16)
        wsum = _dot(band, pbuf_ref[:, cs])
        inv_cnt = 1.0 / jnp.minimum(tpos + 1, w).astype(F32)
        d = wsum * inv_cnt - z_ref[:, Z_POOL + gi_ * POOL_GROUP_CH:Z_POOL + (gi_ + 1) * POOL_GROUP_CH].astype(F32)
        y_pool.append(_dot(d.astype(BF16), wpool_ref[gi_]) * pscale_ref[:, cs])
    pbuf_ref[0:POOL_HALO, :] = pbuf_ref[ts:ts + POOL_HALO, :]

    def per_chunk_row(r):
        return jnp.concatenate(
            [jnp.broadcast_to(b_all[c * C + r:c * C + r + 1, :], (C, GLA_KEY_DIM)) for c in range(nchunk)], axis=0)

    b_mid = per_chunk_row(C // 2 - 1)
    b_last = per_chunk_row(C - 1)
    q = z_ref[:, Z_Q:Z_Q + GLA_KEY_DIM].astype(F32) * (GLA_HEAD_K ** -0.5)
    k = z_ref[:, Z_K:Z_K + GLA_KEY_DIM].astype(F32)
    qa = (q * jnp.exp(b_all - b_mid)).astype(BF16)
    ka = (k * jnp.exp(b_mid - b_all)).astype(BF16)
    qd = (q * jnp.exp(b_all)).astype(BF16)
    kd = (k * jnp.exp(b_last - b_all)).astype(BF16)
    e_dec = [jnp.exp(b_all[c * C + C - 1:c * C + C, :]) for c in range(nchunk)]
    gnorm = gnorm_ref[...]
    for h in range(GLA_HEADS):
        ks = slice(h * GLA_HEAD_K, (h + 1) * GLA_HEAD_K)
        v = z_ref[:, Z_V + h * GLA_HEAD_V:Z_V + (h + 1) * GLA_HEAD_V]
        attn = jnp.where(same_chunk_causal, _dot_nt(qa[:, ks], ka[:, ks]), 0.0).astype(BF16)
        o = _dot(attn, v)
        st = state_ref[h]
        o_inter = []
        for c in range(nchunk):
            rs = slice(c * C, (c + 1) * C)
            o_inter.append(_dot_nt(qd[rs, ks], st.astype(BF16)))
            st = st * e_dec[c][:, ks] + _dot_tn(v[rs, :], kd[rs, ks])
        state_ref[h] = st
        o = o + jnp.concatenate(o_inter, axis=0)
        o = o * lax.rsqrt(jnp.mean(o * o, axis=-1, keepdims=True) + GLA_NORM_EPS) * gnorm
        gi = z_ref[:, Z_G + h * GLA_HEAD_V:Z_G + (h + 1) * GLA_HEAD_V].astype(F32)
        ygla_ref[:, h * GLA_HEAD_V:(h + 1) * GLA_HEAD_V] = o * (gi * _sigmoid(gi))

    RC = CONV_RC

    def conv_rows(i, carry):
        base = pl.multiple_of(i * RC, RC)
        for cc in range(D_MODEL // LANES):
            cs = slice(cc * LANES, (cc + 1) * LANES)
            acc = None
            for r in range(8):
                n_rows = RC if r == 0 else RC + 8
                part = None
                for a in range(5):
                    o = 8 * a + r
                    if o < 2 or o > CONV_HALO:
                        continue
                    kk = o - 2
                    term = convw_ref[kk:kk + 1, cs] * ubuf_ref[pl.ds(base + 8 * a, n_rows), cs]
                    part = term if part is None else part + term
                part = part[r:r + RC, :]
                acc = part if acc is None else acc + part
            yconv_ref[pl.ds(base, RC), cs] = acc + convb_ref[:, cs]
        return carry

    lax.fori_loop(0, ts // RC, conv_rows, 0)
    ubuf_ref[0:CONV_HALO, :] = ubuf_ref[ts:ts + CONV_HALO, :]
    yc = yconv_ref[...]
    mu = jnp.mean(yc, axis=-1, keepdims=True)
    ycc = yc - mu
    var = jnp.mean(ycc * ycc, axis=-1, keepdims=True)
    yn = ycc * lax.rsqrt(var + LN_EPS) * lng_ref[...] + lnb_ref[...]
    y_conv = _dot((yn * _sigmoid(yn)).astype(BF16), wco_ref[...])

    mix_parts = []
    for gi_ in range(4):
        cs = slice(gi_ * POOL_GROUP_CH, (gi_ + 1) * POOL_GROUP_CH)
        g0 = _sigmoid(z_ref[:, Z_GATE + gi_ * 256:Z_GATE + (gi_ + 1) * 256])
        g1 = _sigmoid(z_ref[:, Z_GATE + D_MODEL + gi_ * 256:Z_GATE + D_MODEL + (gi_ + 1) * 256])
        g2 = _sigmoid(z_ref[:, Z_GATE + 2 * D_MODEL + gi_ * 256:Z_GATE + 2 * D_MODEL + (gi_ + 1) * 256])
        m = g0 * ygla_ref[:, cs].astype(BF16) + g1 * y_conv[:, cs].astype(BF16) + g2 * y_pool[gi_].astype(BF16)
        mix_parts.append(m)
    o_ref[...] = x_ref[...] + _dot(jnp.concatenate(mix_parts, axis=1), wout_ref[...])


def _mixer(z, zgk, x2, wgk, bgk, gnorm, convw, convb, lng, lnb, wco, wpool, pscale, wout, *, B, S, ts=256):
    T = B * S
    nt = S // ts
    rowmap = lambda b, s: (b * nt + s, 0)
    return pl.pallas_call(
        functools.partial(_mixer_kernel, ts=ts),
        out_shape=jax.ShapeDtypeStruct((T, D_MODEL), F32),
        grid=(B, nt),
        in_specs=[
            pl.BlockSpec((ts, Z_DIM), rowmap),
            pl.BlockSpec((ts, GK_PAD), rowmap),
            pl.BlockSpec((ts, D_MODEL), rowmap),
            _once((GK_PAD, GLA_KEY_DIM)),
            _once((1, GLA_KEY_DIM)),
            _once((1, GLA_HEAD_V)),
            _once((CONV_WIDTH, D_MODEL)),
            _once((1, D_MODEL)),
            _once((1, D_MODEL)),
            _once((1, D_MODEL)),
            _once((D_MODEL, D_MODEL)),
            _once((4, POOL_GROUP_CH, POOL_GROUP_CH)),
            _once((1, D_MODEL)),
            _once((D_MODEL, D_MODEL)),
        ],
        out_specs=pl.BlockSpec((ts, D_MODEL), rowmap),
        scratch_shapes=[
            pltpu.VMEM((GLA_HEADS, GLA_HEAD_V, GLA_HEAD_K), F32),
            pltpu.VMEM((CONV_HALO + ts, D_MODEL), F32),
            pltpu.VMEM((POOL_HALO + ts, D_MODEL), BF16),
            pltpu.VMEM((ts, D_MODEL), F32),
            pltpu.VMEM((ts, D_MODEL), F32),
        ],
        compiler_params=_params("arbitrary", "arbitrary"),
        name="token_mixer",
    )(z, zgk, x2, wgk, bgk, gnorm, convw, convb, lng, lnb, wco, wpool, pscale, wout)


def _ple(x1, p, gple, wpg, wpp):
    hp = _rms(x1, gple, RMS_EPS).astype(BF16)
    return x1 + _sigmoid(_dot(hp, wpg)) * _dot(p.astype(BF16), wpp)


def _ffn_ple_kernel(x_ref, p_ref, gffn_ref, wgu_ref, wd_ref, gple_ref, wpg_ref, wpp_ref, o_ref):
    x = x_ref[...]
    h = _rms(x, gffn_ref[...], RMS_EPS).astype(BF16)
    F = wd_ref.shape[0]
    gu = _dot(h, wgu_ref[...])
    g = gu[:, :F]
    u = gu[:, F:]
    y = x + _dot((g * _sigmoid(g) * u).astype(BF16), wd_ref[...])
    o_ref[...] = _ple(y, p_ref[...], gple_ref[...], wpg_ref[...], wpp_ref[...])


def _ffn_ple(x2, p2, gffn, wgu, wd, gple, wpg, wpp, *, tm=512):
    T = x2.shape[0]
    F = wd.shape[0]
    return pl.pallas_call(
        _ffn_ple_kernel,
        out_shape=jax.ShapeDtypeStruct((T, D_MODEL), F32),
        grid=(T // tm,),
        in_specs=[
            pl.BlockSpec((tm, D_MODEL), lambda i: (i, 0)),
            pl.BlockSpec((tm, PLE_DIM), lambda i: (i, 0)),
            _once((1, D_MODEL)),
            _once((D_MODEL, 2 * F)),
            _once((F, D_MODEL)),
            _once((1, D_MODEL)),
            _once((D_MODEL, D_MODEL)),
            _once((PLE_DIM, D_MODEL)),
        ],
        out_specs=pl.BlockSpec((tm, D_MODEL), lambda i: (i, 0)),
        compiler_params=_params("arbitrary"),
        name="ffn_ple",
    )(x2, p2, gffn, wgu, wd, gple, wpg, wpp)


def _router_kernel(x_ref, g_ref, whl_ref, br_ref, h_ref, ri_ref, rw_ref, cnt_ref, run_ref, *, tm):
    i = pl.program_id(0)

    @pl.when(i == 0)
    def _():
        run_ref[...] = jnp.zeros_like(run_ref)

    h = _rms(x_ref[...], g_ref[...], RMS_EPS)
    h_ref[...] = h.reshape((tm,) + ROW_TILE)
    hh = h.astype(BF16)
    hl = (h - hh.astype(F32)).astype(BF16)
    hh_w = _dot(hh, whl_ref[...])
    logits = hh_w[:, :LANES] + hh_w[:, LANES:] + _dot(hl, whl_ref[:, :LANES]) + br_ref[...]
    lane = lax.broadcasted_iota(jnp.int32, (tm, LANES), 1)
    m1 = jnp.max(logits, axis=-1, keepdims=True)
    i1 = jnp.min(jnp.where(logits == m1, lane, LANES), axis=-1, keepdims=True)
    l2 = jnp.where(lane == i1, -jnp.inf, logits)
    m2 = jnp.max(l2, axis=-1, keepdims=True)
    i2 = jnp.min(jnp.where(l2 == m2, lane, LANES), axis=-1, keepdims=True)
    e = jnp.exp(m2 - m1)
    w1 = 1.0 / (1.0 + e)
    w2 = e / (1.0 + e)
    sel1 = lane == i1
    sel2 = lane == i2
    onehot = (sel1 | sel2).astype(BF16)
    row = lax.broadcasted_iota(jnp.int32, (tm, tm), 0)
    col = lax.broadcasted_iota(jnp.int32, (tm, tm), 1)
    before = _dot((col < row).astype(BF16), onehot) + run_ref[...]
    pos1 = jnp.sum(jnp.where(sel1, before, 0.0), axis=-1, keepdims=True).astype(jnp.int32)
    pos2 = jnp.sum(jnp.where(sel2, before, 0.0), axis=-1, keepdims=True).astype(jnp.int32)
    run = run_ref[...] + jnp.sum(onehot.astype(F32), axis=0, keepdims=True)
    run_ref[...] = run
    cnt_ref[...] = run
    ri_ref[...] = jnp.where(lane == 0, i1, jnp.where(lane == 1, i2, jnp.where(lane == 2, pos1, jnp.where(lane == 3, pos2, 0))))
    rw_ref[...] = jnp.where(lane == 0, w1, jnp.where(lane == 1, w2, 0.0))


def _router(x2, g, whl, br, *, tm=512):
    T = x2.shape[0]
    return pl.pallas_call(
        functools.partial(_router_kernel, tm=tm),
        out_shape=(
            jax.ShapeDtypeStruct((T,) + ROW_TILE, F32),
            jax.ShapeDtypeStruct((T, LANES), jnp.int32),
            jax.ShapeDtypeStruct((T, LANES), F32),
            jax.ShapeDtypeStruct((1, LANES), F32),
        ),
        grid=(T // tm,),
        in_specs=[
            pl.BlockSpec((tm, D_MODEL), lambda i: (i, 0)),
            _once((1, D_MODEL)),
            _once((D_MODEL, 2 * LANES)),
            _once((1, LANES)),
        ],
        out_specs=(
            pl.BlockSpec((tm,) + ROW_TILE, lambda i: (i, 0, 0)),
            pl.BlockSpec((tm, LANES), lambda i: (i, 0)),
            pl.BlockSpec((tm, LANES), lambda i: (i, 0)),
            pl.BlockSpec((1, LANES), lambda i: (0, 0)),
        ),
        scratch_shapes=[pltpu.VMEM((1, LANES), F32)],
        compiler_params=_params("arbitrary"),
        name="router",
    )(x2, g, whl, br)


def _row_copy(src_ref, src_row, dst_ref, dst_row, sem):
    return pltpu.make_async_copy(src_ref.at[pl.ds(src_row, 1)], dst_ref.at[pl.ds(dst_row, 1)], sem)


def _dispatch_kernel(pe_ref, nu_ref, dest_ref, h_ref, xs_ref, zero_ref, sem, *, td, tmm, n_blocks):
    @pl.when(pl.program_id(0) == 0)
    def _():
        zero_ref[...] = jnp.zeros_like(zero_ref)

        def fill(start):
            return pltpu.make_async_copy(zero_ref, xs_ref.at[pl.ds(pl.multiple_of(start, tmm), tmm)], sem)

        for e in range(N_EXPERTS):
            prev_end = pe_ref[e - 1] if e > 0 else 0

            @pl.when(pe_ref[e] > prev_end)
            def _():
                fill(pe_ref[e] - tmm).start()
                fill(pe_ref[e] - tmm).wait()

        for j in range(N_EXPERTS):
            @pl.when(n_blocks - 1 - j >= nu_ref[0])
            def _():
                fill((n_blocks - 1 - j) * tmm).start()
                fill((n_blocks - 1 - j) * tmm).wait()

    for r in range(td):
        _row_copy(h_ref, r, xs_ref, dest_ref[2 * r], sem).start(priority=0)
        _row_copy(h_ref, r, xs_ref, dest_ref[2 * r + 1], sem).start(priority=1)
    for _ in range(2):
        pltpu.make_async_copy(h_ref, xs_ref.at[pl.ds(0, td)], sem).wait()


def _dispatch(pad_end, n_used, dest_flat, h, *, n_blocks, tmm, td=512):
    T = h.shape[0]
    return pl.pallas_call(
        functools.partial(_dispatch_kernel, td=td, tmm=tmm, n_blocks=n_blocks),
        out_shape=jax.ShapeDtypeStruct((n_blocks * tmm,) + ROW_TILE, F32),
        grid_spec=pltpu.PrefetchScalarGridSpec(
            num_scalar_prefetch=2,
            grid=(T // td,),
            in_specs=[
                pl.BlockSpec((2 * td,), lambda i, pe, nu: (i,), memory_space=pltpu.SMEM),
                pl.BlockSpec((td,) + ROW_TILE, lambda i, pe, nu: (i, 0, 0)),
            ],
            out_specs=pl.BlockSpec(memory_space=pl.ANY),
            scratch_shapes=[pltpu.VMEM((tmm,) + ROW_TILE, F32), pltpu.SemaphoreType.DMA],
        ),
        compiler_params=_params("arbitrary"),
        name="moe_dispatch",
    )(pad_end, n_used, dest_flat, h)


def _gmm_kernel(be_ref, nu_ref, xs_ref, wg_ref, wu_ref, wd_ref, o_ref, *, f_chunks):
    i = pl.program_id(0)

    @pl.when(i < nu_ref[0])
    def _():
        x = xs_ref[...].reshape(xs_ref.shape[0], D_MODEL).astype(BF16)
        y = None
        for lo, hi in f_chunks:
            g = _dot(x, wg_ref[0, :, lo:hi])
            u = _dot(x, wu_ref[0, :, lo:hi])
            part = _dot((g * _sigmoid(g) * u).astype(BF16), wd_ref[0, lo:hi, :])
            y = part if y is None else y + part
        o_ref[...] = y.reshape(o_ref.shape)

    @pl.when(i >= nu_ref[0])
    def _():
        o_ref[...] = jnp.zeros_like(o_ref)


def _gmm(block_e, n_used, xs, wg, wu, wd, *, tmm):
    n_rows = xs.shape[0]
    n_blocks = n_rows // tmm
    F = wd.shape[1]
    n_pass = F // MXU_PASS_COLS
    assert n_pass * MXU_PASS_COLS == F
    cut = (n_pass // 2) * MXU_PASS_COLS
    f_chunks = ((0, cut), (cut, F))

    def expert_spec(shape):
        return pl.BlockSpec((1,) + shape, lambda i, be, nu: (be[i], 0, 0))

    return pl.pallas_call(
        functools.partial(_gmm_kernel, f_chunks=f_chunks),
        out_shape=jax.ShapeDtypeStruct((n_rows,) + ROW_TILE, F32),
        grid_spec=pltpu.PrefetchScalarGridSpec(
            num_scalar_prefetch=2,
            grid=(n_blocks,),
            in_specs=[
                pl.BlockSpec((tmm,) + ROW_TILE, lambda i, be, nu: (jnp.minimum(i, nu[0] - 1), 0, 0)),
                expert_spec((D_MODEL, F)),
                expert_spec((D_MODEL, F)),
                expert_spec((F, D_MODEL)),
            ],
            out_specs=pl.BlockSpec((tmm,) + ROW_TILE, lambda i, be, nu: (i, 0, 0)),
        ),
        compiler_params=pltpu.CompilerParams(dimension_semantics=("arbitrary",), vmem_limit_bytes=GMM_VMEM_LIMIT),
        name="moe_grouped_swiglu",
    )(block_e, n_used, xs, wg, wu, wd)


def _combine_ple_kernel(dest_ref, dest_next_ref, x_ref, rw_ref, p_ref, gple_ref, wpg_ref, wpp_ref, gfin_ref,
                        yb_ref, o_ref, bufa_ref, bufb_ref, sem, *, tp):
    i = pl.program_id(0)
    th = tp // 2
    bufs = (bufa_ref, bufb_ref)

    def gather(d_ref, first, b):
        for r in range(th):
            _row_copy(yb_ref, d_ref[2 * (first + r)], bufs[b].at[0], r, sem.at[b]).start(priority=0)
            _row_copy(yb_ref, d_ref[2 * (first + r) + 1], bufs[b].at[1], r, sem.at[b]).start(priority=1)

    def wait(b):
        for k in range(2):
            pltpu.make_async_copy(yb_ref.at[pl.ds(0, th)], bufs[b].at[k], sem.at[b]).wait()

    def finish(rows, b):
        rw = rw_ref[rows, :]
        y0 = bufs[b][0].reshape(th, D_MODEL)
        y1 = bufs[b][1].reshape(th, D_MODEL)
        x1 = x_ref[rows, :] + rw[:, 0:1] * y0 + rw[:, 1:2] * y1
        x2 = _ple(x1, p_ref[rows, :], gple_ref[...], wpg_ref[...], wpp_ref[...])
        o_ref[rows, :] = _rms(x2, gfin_ref[...], RMS_EPS)

    @pl.when(i == 0)
    def _():
        gather(dest_ref, 0, 0)

    wait(0)
    gather(dest_ref, th, 1)
    finish(slice(0, th), 0)
    wait(1)
    gather(dest_next_ref, 0, 0)
    finish(slice(th, tp), 1)

    @pl.when(i == pl.num_programs(0) - 1)
    def _():
        wait(0)


def _combine_ple(dest_flat, x2, rw, p2, gple, wpg, wpp, gfin, yb, *, tp=512):
    T = x2.shape[0]
    n = T // tp
    return pl.pallas_call(
        functools.partial(_combine_ple_kernel, tp=tp),
        out_shape=jax.ShapeDtypeStruct((T, D_MODEL), F32),
        grid=(n,),
        in_specs=[
            pl.BlockSpec((2 * tp,), lambda i: (i,), memory_space=pltpu.SMEM),
            pl.BlockSpec((2 * tp,), lambda i: (jnp.minimum(i + 1, n - 1),), memory_space=pltpu.SMEM),
            pl.BlockSpec((tp, D_MODEL), lambda i: (i, 0)),
            pl.BlockSpec((tp, LANES), lambda i: (i, 0)),
            pl.BlockSpec((tp, PLE_DIM), lambda i: (i, 0)),
            _once((1, D_MODEL)),
            _once((D_MODEL, D_MODEL)),
            _once((PLE_DIM, D_MODEL)),
            _once((1, D_MODEL)),
            pl.BlockSpec(memory_space=pl.ANY),
        ],
        out_specs=pl.BlockSpec((tp, D_MODEL), lambda i: (i, 0)),
        scratch_shapes=[pltpu.VMEM((2, tp // 2) + ROW_TILE, F32), pltpu.VMEM((2, tp // 2) + ROW_TILE, F32),
                        pltpu.SemaphoreType.DMA((2,))],
        compiler_params=_params("arbitrary"),
        name="moe_combine_ple",
    )(dest_flat, dest_flat, x2, rw, p2, gple, wpg, wpp, gfin, yb)


def _moe_ple_final(x2, p2, gffn, w_router, b_router, wg, wu, wd, gple, wpg, wpp, gfin, *, tmm=512):
    T = x2.shape[0]
    wr = jnp.zeros((D_MODEL, LANES), F32).at[:, :N_EXPERTS].set(w_router)
    whi = wr.astype(BF16)
    wlo = (wr - whi.astype(F32)).astype(BF16)
    br = jnp.full((1, LANES), NEG_BIG, F32).at[0, :N_EXPERTS].set(b_router)
    h, ri, rw, cnt = _router(x2, gffn, jnp.concatenate([whi, wlo], axis=1), br)
    counts = cnt[0, :N_EXPERTS].astype(jnp.int32)
    padded = (counts + tmm - 1) // tmm * tmm
    pad_end = jnp.cumsum(padded).astype(jnp.int32)
    pad_start = pad_end - padded
    e_sel = ri[:, 0:2, None] == jnp.arange(N_EXPERTS, dtype=jnp.int32)
    dest = jnp.sum(jnp.where(e_sel, pad_start, 0), axis=-1) + ri[:, 2:4]
    dest_flat = dest.reshape(2 * T).astype(jnp.int32)
    n_blocks = (2 * T) // tmm + N_EXPERTS
    block_start = jnp.arange(n_blocks, dtype=jnp.int32) * tmm
    block_e = jnp.minimum(jnp.sum(block_start[:, None] >= pad_end[None, :], axis=1), N_EXPERTS - 1).astype(jnp.int32)
    n_used = (pad_end[-1] // tmm).astype(jnp.int32).reshape(1)
    xs = _dispatch(pad_end, n_used, dest_flat, h, n_blocks=n_blocks, tmm=tmm)
    yb = _gmm(block_e, n_used, xs, wg.astype(BF16), wu.astype(BF16), wd.astype(BF16), tmm=tmm)
    return _combine_ple(dest_flat, x2, rw, p2, gple, wpg, wpp, gfin, yb)


def kernel(x, p, g_mix, w_in, w_gk_up, b_gk, g_gla_norm, conv_w, conv_b, ln_conv_g, ln_conv_b, w_conv_out, w_pool, pool_scale, w_out, g_ffn, w_ffn_gate, w_ffn_up, w_ffn_down, w_router, b_router, w_moe_gate, w_moe_up, w_moe_down, g_ple, w_ple_gate, w_ple_proj, g_final):
    B, S, _ = x.shape
    T = B * S
    depth = w_in.shape[0]
    assert depth == 2 and x.shape[2] == D_MODEL, "dense layer 0, routed layer 1 with the final norm fused"
    x2 = x.reshape(T, D_MODEL)
    row = lambda a: a.reshape(1, -1)
    for i in range(depth):
        wi = w_in[i]
        w_main = jnp.concatenate([wi[:, :3072], wi[:, 3088:]], axis=1).astype(BF16)
        w_gk = jnp.zeros((D_MODEL, GK_PAD), F32).at[:, :GLA_GATE_RANK].set(wi[:, 3072:3088]).astype(BF16)
        w_gk_up_p = jnp.zeros((GK_PAD, GLA_KEY_DIM), F32).at[:GLA_GATE_RANK].set(w_gk_up[i]).astype(BF16)
        z, zgk = _in_proj(x2, row(g_mix[i]), w_main, w_gk)
        x2 = _mixer(z, zgk, x2, w_gk_up_p, row(b_gk[i]), row(g_gla_norm[i]), conv_w[i], row(conv_b[i]),
                    row(ln_conv_g[i]), row(ln_conv_b[i]), w_conv_out[i].astype(BF16), w_pool[i].astype(BF16),
                    row(pool_scale[i]), w_out[i].astype(BF16), B=B, S=S)
        p2 = p[i].reshape(T, PLE_DIM)
        wpg = w_ple_gate[i].astype(BF16)
        wpp = w_ple_proj[i].astype(BF16)
        if i % 2 == 0:
            j = i // 2
            wgu = jnp.concatenate([w_ffn_gate[j].astype(BF16), w_ffn_up[j].astype(BF16)], axis=1)
            x2 = _ffn_ple(x2, p2, row(g_ffn[i]), wgu, w_ffn_down[j].astype(BF16), row(g_ple[i]), wpg, wpp)
        else:
            j = i // 2
            x2 = _moe_ple_final(x2, p2, row(g_ffn[i]), w_router[j], b_router[j], w_moe_gate[j], w_moe_up[j],
                                w_moe_down[j], row(g_ple[i]), wpg, wpp, row(g_final))
    return x2.reshape(B, S, D_MODEL)
```

```python
import functools

import jax
import jax.numpy as jnp
from jax import lax
from jax.experimental import pallas as pl
from jax.experimental.pallas import tpu as pltpu

F32 = jnp.float32
BF16 = jnp.bfloat16

D_MODEL = 1024
GLA_HEADS = 4
GLA_KEY_DIM = 512
GLA_HEAD_K = 128
GLA_HEAD_V = 256
GLA_GATE_RANK = 16
GLA_GATE_NORMALIZER = 16.0
GLA_CHUNK = 64
GLA_NORM_EPS = 1e-5
CONV_WIDTH = 31
LN_EPS = 1e-5
POOL_WINDOWS = (2, 4, 8, 16)
POOL_GROUP_CH = 256
N_EXPERTS = 8
PLE_DIM = 256
RMS_EPS = 1e-6

Z_Q, Z_K, Z_V, Z_G, Z_CA, Z_CG, Z_POOL, Z_GATE = 0, 512, 1024, 2048, 3072, 4096, 5120, 6144
Z_DIM = 9216
GK_PAD = 128
LANES = 128

CONV_HALO = 32
POOL_HALO = 128
CONV_RC = 256
VMEM_LIMIT = 56 * 1024 * 1024
GMM_VMEM_LIMIT = 60 * 1024 * 1024

NEG_BIG = -1e30
MXU_PASS_COLS = 512
ROW_TILE = (D_MODEL // LANES, LANES)


def _dot(a, b):
    return jnp.dot(a, b, preferred_element_type=F32)


def _dot_nt(a, b):
    return lax.dot_general(a, b, (((1,), (1,)), ((), ())), preferred_element_type=F32)


def _dot_tn(a, b):
    return lax.dot_general(a, b, (((0,), (0,)), ((), ())), preferred_element_type=F32)


def _sigmoid(x):
    return 0.5 * jnp.tanh(0.5 * x) + 0.5


def _rms(x, g, eps):
    return x * lax.rsqrt(jnp.mean(x * x, axis=-1, keepdims=True) + eps) * g


def _split3(x):
    hi = x.astype(BF16)
    r1 = x - hi.astype(F32)
    mid = r1.astype(BF16)
    lo = (r1 - mid.astype(F32)).astype(BF16)
    return hi, mid, lo


def _once(shape):
    n = len(shape)
    return pl.BlockSpec(shape, lambda *_: (0,) * n, pipeline_mode=pl.Buffered(1))


def _params(*semantics):
    return pltpu.CompilerParams(dimension_semantics=semantics, vmem_limit_bytes=VMEM_LIMIT)


def _in_proj_kernel(x_ref, g_ref, w_ref, wgk_ref, z_ref, zgk_ref, *, nc):
    h = _rms(x_ref[...], g_ref[...], RMS_EPS).astype(BF16)
    for c in range(0, Z_DIM, nc):
        z_ref[:, c:c + nc] = _dot(h, w_ref[:, c:c + nc]).astype(BF16)
    zgk_ref[...] = _dot(h, wgk_ref[...])


def _in_proj(x2, g, w_main, w_gk, *, tm=512, nc=1024):
    T = x2.shape[0]
    return pl.pallas_call(
        functools.partial(_in_proj_kernel, nc=nc),
        out_shape=(jax.ShapeDtypeStruct((T, Z_DIM), BF16), jax.ShapeDtypeStruct((T, GK_PAD), F32)),
        grid=(T // tm,),
        in_specs=[
            pl.BlockSpec((tm, D_MODEL), lambda i: (i, 0)),
            _once((1, D_MODEL)),
            _once((D_MODEL, Z_DIM)),
            _once((D_MODEL, GK_PAD)),
        ],
        out_specs=(pl.BlockSpec((tm, Z_DIM), lambda i: (i, 0)), pl.BlockSpec((tm, GK_PAD), lambda i: (i, 0))),
        compiler_params=_params("arbitrary"),
        name="in_proj",
    )(x2, g, w_main, w_gk)


def _mixer_kernel(z_ref, zgk_ref, x_ref, wgk_ref, bgk_ref, gnorm_ref, convw_ref, convb_ref, lng_ref, lnb_ref,
                  wco_ref, wpool_ref, pscale_ref, wout_ref, o_ref,
                  state_ref, ubuf_ref, pbuf_ref, ygla_ref, yconv_ref, *, ts):
    s = pl.program_id(1)
    C = GLA_CHUNK

    @pl.when(s == 0)
    def _():
        state_ref[...] = jnp.zeros_like(state_ref)
        ubuf_ref[0:CONV_HALO, :] = jnp.zeros((CONV_HALO, D_MODEL), F32)
        pbuf_ref[0:POOL_HALO, :] = jnp.zeros((POOL_HALO, D_MODEL), BF16)

    gk_pre = _dot(zgk_ref[...].astype(BF16), wgk_ref[...]) + bgk_ref[...]
    ca = z_ref[:, Z_CA:Z_CA + D_MODEL].astype(F32)
    cg = z_ref[:, Z_CG:Z_CG + D_MODEL].astype(F32)
    ubuf_ref[CONV_HALO:CONV_HALO + ts, :] = ca * _sigmoid(cg)
    gk = (jnp.minimum(gk_pre, 0.0) - jnp.log(1.0 + jnp.exp(-jnp.abs(gk_pre)))) * (1.0 / GLA_GATE_NORMALIZER)
    row = lax.broadcasted_iota(jnp.int32, (ts, ts), 0)
    col = lax.broadcasted_iota(jnp.int32, (ts, ts), 1)
    same_chunk_causal = (jnp.right_shift(row, 6) == jnp.right_shift(col, 6)) & (col <= row)
    tri = same_chunk_causal.astype(BF16)
    g_hi, g_mid, g_lo = _split3(gk)
    b_all = _dot(tri, g_hi) + _dot(tri, g_mid) + _dot(tri, g_lo)
    nchunk = ts // C

    pbuf_ref[POOL_HALO:POOL_HALO + ts, :] = z_ref[:, Z_POOL:Z_POOL + D_MODEL]
    tpos = s * ts + lax.broadcasted_iota(jnp.int32, (ts, 1), 0)
    prow = lax.broadcasted_iota(jnp.int32, (ts, POOL_HALO + ts), 0) + POOL_HALO
    pcol = lax.broadcasted_iota(jnp.int32, (ts, POOL_HALO + ts), 1)
    y_pool = []
    for gi_, w in enumerate(POOL_WINDOWS):
        cs = slice(gi_ * POOL_GROUP_CH, (gi_ + 1) * POOL_GROUP_CH)
        band = ((pcol <= prow) & (pcol > prow - w)).astype(BF16)
        wsum = _dot(band, pbuf_ref[:, cs])
        inv_cnt = 1.0 / jnp.minimum(tpos + 1, w).astype(F32)
        d = wsum * inv_cnt - z_ref[:, Z_POOL + gi_ * POOL_GROUP_CH:Z_POOL + (gi_ + 1) * POOL_GROUP_CH].astype(F32)
        y_pool.append(_dot(d.astype(BF16), wpool_ref[gi_]) * pscale_ref[:, cs])
    pbuf_ref[0:POOL_HALO, :] = pbuf_ref[ts:ts + POOL_HALO, :]

    def per_chunk_row(r):
        return jnp.concatenate(
            [jnp.broadcast_to(b_all[c * C + r:c * C + r + 1, :], (C, GLA_KEY_DIM)) for c in range(nchunk)], axis=0)

    b_mid = per_chunk_row(C // 2 - 1)
    b_last = per_chunk_row(C - 1)
    q = z_ref[:, Z_Q:Z_Q + GLA_KEY_DIM].astype(F32) * (GLA_HEAD_K ** -0.5)
    k = z_ref[:, Z_K:Z_K + GLA_KEY_DIM].astype(F32)
    qa = (q * jnp.exp(b_all - b_mid)).astype(BF16)
    ka = (k * jnp.exp(b_mid - b_all)).astype(BF16)
    qd = (q * jnp.exp(b_all)).astype(BF16)
    kd = (k * jnp.exp(b_last - b_all)).astype(BF16)
    e_dec = [jnp.exp(b_all[c * C + C - 1:c * C + C, :]) for c in range(nchunk)]
    gnorm = gnorm_ref[...]
    for h in range(GLA_HEADS):
        ks = slice(h * GLA_HEAD_K, (h + 1) * GLA_HEAD_K)
        v = z_ref[:, Z_V + h * GLA_HEAD_V:Z_V + (h + 1) * GLA_HEAD_V]
        attn = jnp.where(same_chunk_causal, _dot_nt(qa[:, ks], ka[:, ks]), 0.0).astype(BF16)
        o = _dot(attn, v)
        st = state_ref[h]
        o_inter = []
        for c in range(nchunk):
            rs = slice(c * C, (c + 1) * C)
            o_inter.append(_dot_nt(qd[rs, ks], st.astype(BF16)))
            st = st * e_dec[c][:, ks] + _dot_tn(v[rs, :], kd[rs, ks])
        state_ref[h] = st
        o = o + jnp.concatenate(o_inter, axis=0)
        o = o * lax.rsqrt(jnp.mean(o * o, axis=-1, keepdims=True) + GLA_NORM_EPS) * gnorm
        gi = z_ref[:, Z_G + h * GLA_HEAD_V:Z_G + (h + 1) * GLA_HEAD_V].astype(F32)
        ygla_ref[:, h * GLA_HEAD_V:(h + 1) * GLA_HEAD_V] = o * (gi * _sigmoid(gi))

    RC = CONV_RC

    def conv_rows(i, carry):
        base = pl.multiple_of(i * RC, RC)
        for cc in range(D_MODEL // LANES):
            cs = slice(cc * LANES, (cc + 1) * LANES)
            acc = None
            for r in range(8):
                n_rows = RC if r == 0 else RC + 8
                part = None
                for a in range(5):
                    o = 8 * a + r
                    if o < 2 or o > CONV_HALO:
                        continue
                    kk = o - 2
                    term = convw_ref[kk:kk + 1, cs] * ubuf_ref[pl.ds(base + 8 * a, n_rows), cs]
                    part = term if part is None else part + term
                part = part[r:r + RC, :]
                acc = part if acc is None else acc + part
            yconv_ref[pl.ds(base, RC), cs] = acc + convb_ref[:, cs]
        return carry

    lax.fori_loop(0, ts // RC, conv_rows, 0)
    ubuf_ref[0:CONV_HALO, :] = ubuf_ref[ts:ts + CONV_HALO, :]
    yc = yconv_ref[...]
    mu = jnp.mean(yc, axis=-1, keepdims=True)
    ycc = yc - mu
    var = jnp.mean(ycc * ycc, axis=-1, keepdims=True)
    yn = ycc * lax.rsqrt(var + LN_EPS) * lng_ref[...] + lnb_ref[...]
    y_conv = _dot((yn * _sigmoid(yn)).astype(BF16), wco_ref[...])

    mix_parts = []
    for gi_ in range(4):
        cs = slice(gi_ * POOL_GROUP_CH, (gi_ + 1) * POOL_GROUP_CH)
        g0 = _sigmoid(z_ref[:, Z_GATE + gi_ * 256:Z_GATE + (gi_ + 1) * 256])
        g1 = _sigmoid(z_ref[:, Z_GATE + D_MODEL + gi_ * 256:Z_GATE + D_MODEL + (gi_ + 1) * 256])
        g2 = _sigmoid(z_ref[:, Z_GATE + 2 * D_MODEL + gi_ * 256:Z_GATE + 2 * D_MODEL + (gi_ + 1) * 256])
        m = g0 * ygla_ref[:, cs].astype(BF16) + g1 * y_conv[:, cs].astype(BF16) + g2 * y_pool[gi_].astype(BF16)
        mix_parts.append(m)
    o_ref[...] = x_ref[...] + _dot(jnp.concatenate(mix_parts, axis=1), wout_ref[...])


def _mixer(z, zgk, x2, wgk, bgk, gnorm, convw, convb, lng, lnb, wco, wpool, pscale, wout, *, B, S, ts=256):
    T = B * S
    nt = S // ts
    rowmap = lambda b, s: (b * nt + s, 0)
    return pl.pallas_call(
        functools.partial(_mixer_kernel, ts=ts),
        out_shape=jax.ShapeDtypeStruct((T, D_MODEL), F32),
        grid=(B, nt),
        in_specs=[
            pl.BlockSpec((ts, Z_DIM), rowmap),
            pl.BlockSpec((ts, GK_PAD), rowmap),
            pl.BlockSpec((ts, D_MODEL), rowmap),
            _once((GK_PAD, GLA_KEY_DIM)),
            _once((1, GLA_KEY_DIM)),
            _once((1, GLA_HEAD_V)),
            _once((CONV_WIDTH, D_MODEL)),
            _once((1, D_MODEL)),
            _once((1, D_MODEL)),
            _once((1, D_MODEL)),
            _once((D_MODEL, D_MODEL)),
            _once((4, POOL_GROUP_CH, POOL_GROUP_CH)),
            _once((1, D_MODEL)),
            _once((D_MODEL, D_MODEL)),
        ],
        out_specs=pl.BlockSpec((ts, D_MODEL), rowmap),
        scratch_shapes=[
            pltpu.VMEM((GLA_HEADS, GLA_HEAD_V, GLA_HEAD_K), F32),
            pltpu.VMEM((CONV_HALO + ts, D_MODEL), F32),
            pltpu.VMEM((POOL_HALO + ts, D_MODEL), BF16),
            pltpu.VMEM((ts, D_MODEL), F32),
            pltpu.VMEM((ts, D_MODEL), F32),
        ],
        compiler_params=_params("arbitrary", "arbitrary"),
        name="token_mixer",
    )(z, zgk, x2, wgk, bgk, gnorm, convw, convb, lng, lnb, wco, wpool, pscale, wout)


def _ple(x1, p, gple, wpg, wpp):
    hp = _rms(x1, gple, RMS_EPS).astype(BF16)
    return x1 + _sigmoid(_dot(hp, wpg)) * _dot(p.astype(BF16), wpp)


def _ffn_ple_kernel(x_ref, p_ref, gffn_ref, wgu_ref, wd_ref, gple_ref, wpg_ref, wpp_ref, o_ref):
    x = x_ref[...]
    h = _rms(x, gffn_ref[...], RMS_EPS).astype(BF16)
    F = wd_ref.shape[0]
    gu = _dot(h, wgu_ref[...])
    g = gu[:, :F]
    u = gu[:, F:]
    y = x + _dot((g * _sigmoid(g) * u).astype(BF16), wd_ref[...])
    o_ref[...] = _ple(y, p_ref[...], gple_ref[...], wpg_ref[...], wpp_ref[...])


def _ffn_ple(x2, p2, gffn, wgu, wd, gple, wpg, wpp, *, tm=512):
    T = x2.shape[0]
    F = wd.shape[0]
    return pl.pallas_call(
        _ffn_ple_kernel,
        out_shape=jax.ShapeDtypeStruct((T, D_MODEL), F32),
        grid=(T // tm,),
        in_specs=[
            pl.BlockSpec((tm, D_MODEL), lambda i: (i, 0)),
            pl.BlockSpec((tm, PLE_DIM), lambda i: (i, 0)),
            _once((1, D_MODEL)),
            _once((D_MODEL, 2 * F)),
            _once((F, D_MODEL)),
            _once((1, D_MODEL)),
            _once((D_MODEL, D_MODEL)),
            _once((PLE_DIM, D_MODEL)),
        ],
        out_specs=pl.BlockSpec((tm, D_MODEL), lambda i: (i, 0)),
        compiler_params=_params("arbitrary"),
        name="ffn_ple",
    )(x2, p2, gffn, wgu, wd, gple, wpg, wpp)


def _router_kernel(x_ref, g_ref, whl_ref, br_ref, h_ref, ri_ref, rw_ref, cnt_ref, run_ref, *, tm):
    i = pl.program_id(0)

    @pl.when(i == 0)
    def _():
        run_ref[...] = jnp.zeros_like(run_ref)

    h = _rms(x_ref[...], g_ref[...], RMS_EPS)
    h_ref[...] = h.reshape((tm,) + ROW_TILE)
    hh = h.astype(BF16)
    hl = (h - hh.astype(F32)).astype(BF16)
    hh_w = _dot(hh, whl_ref[...])
    logits = hh_w[:, :LANES] + hh_w[:, LANES:] + _dot(hl, whl_ref[:, :LANES]) + br_ref[...]
    lane = lax.broadcasted_iota(jnp.int32, (tm, LANES), 1)
    m1 = jnp.max(logits, axis=-1, keepdims=True)
    i1 = jnp.min(jnp.where(logits == m1, lane, LANES), axis=-1, keepdims=True)
    l2 = jnp.where(lane == i1, -jnp.inf, logits)
    m2 = jnp.max(l2, axis=-1, keepdims=True)
    i2 = jnp.min(jnp.where(l2 == m2, lane, LANES), axis=-1, keepdims=True)
    e = jnp.exp(m2 - m1)
    w1 = 1.0 / (1.0 + e)
    w2 = e / (1.0 + e)
    sel1 = lane == i1
    sel2 = lane == i2
    onehot = (sel1 | sel2).astype(BF16)
    row = lax.broadcasted_iota(jnp.int32, (tm, tm), 0)
    col = lax.broadcasted_iota(jnp.int32, (tm, tm), 1)
    before = _dot((col < row).astype(BF16), onehot) + run_ref[...]
    pos1 = jnp.sum(jnp.where(sel1, before, 0.0), axis=-1, keepdims=True).astype(jnp.int32)
    pos2 = jnp.sum(jnp.where(sel2, before, 0.0), axis=-1, keepdims=True).astype(jnp.int32)
    run = run_ref[...] + jnp.sum(onehot.astype(F32), axis=0, keepdims=True)
    run_ref[...] = run
    cnt_ref[...] = run
    ri_ref[...] = jnp.where(lane == 0, i1, jnp.where(lane == 1, i2, jnp.where(lane == 2, pos1, jnp.where(lane == 3, pos2, 0))))
    rw_ref[...] = jnp.where(lane == 0, w1, jnp.where(lane == 1, w2, 0.0))


def _router(x2, g, whl, br, *, tm=512):
    T = x2.shape[0]
    return pl.pallas_call(
        functools.partial(_router_kernel, tm=tm),
        out_shape=(
            jax.ShapeDtypeStruct((T,) + ROW_TILE, F32),
            jax.ShapeDtypeStruct((T, LANES), jnp.int32),
            jax.ShapeDtypeStruct((T, LANES), F32),
            jax.ShapeDtypeStruct((1, LANES), F32),
        ),
        grid=(T // tm,),
        in_specs=[
            pl.BlockSpec((tm, D_MODEL), lambda i: (i, 0)),
            _once((1, D_MODEL)),
            _once((D_MODEL, 2 * LANES)),
            _once((1, LANES)),
        ],
        out_specs=(
            pl.BlockSpec((tm,) + ROW_TILE, lambda i: (i, 0, 0)),
            pl.BlockSpec((tm, LANES), lambda i: (i, 0)),
            pl.BlockSpec((tm, LANES), lambda i: (i, 0)),
            pl.BlockSpec((1, LANES), lambda i: (0, 0)),
        ),
        scratch_shapes=[pltpu.VMEM((1, LANES), F32)],
        compiler_params=_params("arbitrary"),
        name="router",
    )(x2, g, whl, br)


def _row_copy(src_ref, src_row, dst_ref, dst_row, sem):
    return pltpu.make_async_copy(src_ref.at[pl.ds(src_row, 1)], dst_ref.at[pl.ds(dst_row, 1)], sem)


def _dispatch_kernel(pe_ref, nu_ref, dest_ref, h_ref, xs_ref, zero_ref, sem, *, td, tmm, n_blocks):
    @pl.when(pl.program_id(0) == 0)
    def _():
        zero_ref[...] = jnp.zeros_like(zero_ref)

        def fill(start):
            return pltpu.make_async_copy(zero_ref, xs_ref.at[pl.ds(pl.multiple_of(start, tmm), tmm)], sem)

        for e in range(N_EXPERTS):
            prev_end = pe_ref[e - 1] if e > 0 else 0

            @pl.when(pe_ref[e] > prev_end)
            def _():
                fill(pe_ref[e] - tmm).start()
                fill(pe_ref[e] - tmm).wait()

        for j in range(N_EXPERTS):
            @pl.when(n_blocks - 1 - j >= nu_ref[0])
            def _():
                fill((n_blocks - 1 - j) * tmm).start()
                fill((n_blocks - 1 - j) * tmm).wait()

    for r in range(td):
        _row_copy(h_ref, r, xs_ref, dest_ref[2 * r], sem).start(priority=0)
        _row_copy(h_ref, r, xs_ref, dest_ref[2 * r + 1], sem).start(priority=1)
    for _ in range(2):
        pltpu.make_async_copy(h_ref, xs_ref.at[pl.ds(0, td)], sem).wait()


def _dispatch(pad_end, n_used, dest_flat, h, *, n_blocks, tmm, td=512):
    T = h.shape[0]
    return pl.pallas_call(
        functools.partial(_dispatch_kernel, td=td, tmm=tmm, n_blocks=n_blocks),
        out_shape=jax.ShapeDtypeStruct((n_blocks * tmm,) + ROW_TILE, F32),
        grid_spec=pltpu.PrefetchScalarGridSpec(
            num_scalar_prefetch=2,
            grid=(T // td,),
            in_specs=[
                pl.BlockSpec((2 * td,), lambda i, pe, nu: (i,), memory_space=pltpu.SMEM),
                pl.BlockSpec((td,) + ROW_TILE, lambda i, pe, nu: (i, 0, 0)),
            ],
            out_specs=pl.BlockSpec(memory_space=pl.ANY),
            scratch_shapes=[pltpu.VMEM((tmm,) + ROW_TILE, F32), pltpu.SemaphoreType.DMA],
        ),
        compiler_params=_params("arbitrary"),
        name="moe_dispatch",
    )(pad_end, n_used, dest_flat, h)


def _gmm_kernel(be_ref, nu_ref, xs_ref, wg_ref, wu_ref, wd_ref, o_ref, *, f_chunks):
    i = pl.program_id(0)

    @pl.when(i < nu_ref[0])
    def _():
        x = xs_ref[...].reshape(xs_ref.shape[0], D_MODEL).astype(BF16)
        y = None
        for lo, hi in f_chunks:
            g = _dot(x, wg_ref[0, :, lo:hi])
            u = _dot(x, wu_ref[0, :, lo:hi])
            part = _dot((g * _sigmoid(g) * u).astype(BF16), wd_ref[0, lo:hi, :])
            y = part if y is None else y + part
        o_ref[...] = y.reshape(o_ref.shape)

    @pl.when(i >= nu_ref[0])
    def _():
        o_ref[...] = jnp.zeros_like(o_ref)


def _gmm(block_e, n_used, xs, wg, wu, wd, *, tmm):
    n_rows = xs.shape[0]
    n_blocks = n_rows // tmm
    F = wd.shape[1]
    n_pass = F // MXU_PASS_COLS
    assert n_pass * MXU_PASS_COLS == F
    cut = (n_pass // 2) * MXU_PASS_COLS
    f_chunks = ((0, cut), (cut, F))

    def expert_spec(shape):
        return pl.BlockSpec((1,) + shape, lambda i, be, nu: (be[i], 0, 0))

    return pl.pallas_call(
        functools.partial(_gmm_kernel, f_chunks=f_chunks),
        out_shape=jax.ShapeDtypeStruct((n_rows,) + ROW_TILE, F32),
        grid_spec=pltpu.PrefetchScalarGridSpec(
            num_scalar_prefetch=2,
            grid=(n_blocks,),
            in_specs=[
                pl.BlockSpec((tmm,) + ROW_TILE, lambda i, be, nu: (jnp.minimum(i, nu[0] - 1), 0, 0)),
                expert_spec((D_MODEL, F)),
                expert_spec((D_MODEL, F)),
                expert_spec((F, D_MODEL)),
            ],
            out_specs=pl.BlockSpec((tmm,) + ROW_TILE, lambda i, be, nu: (i, 0, 0)),
        ),
        compiler_params=pltpu.CompilerParams(dimension_semantics=("arbitrary",), vmem_limit_bytes=GMM_VMEM_LIMIT),
        name="moe_grouped_swiglu",
    )(block_e, n_used, xs, wg, wu, wd)


def _combine_ple_kernel(dest_ref, dest_next_ref, x_ref, rw_ref, p_ref, gple_ref, wpg_ref, wpp_ref, gfin_ref,
                        yb_ref, o_ref, bufa_ref, bufb_ref, sem, *, tp):
    i = pl.program_id(0)
    th = tp // 2
    bufs = (bufa_ref, bufb_ref)

    def gather(d_ref, first, b):
        for r in range(th):
            _row_copy(yb_ref, d_ref[2 * (first + r)], bufs[b].at[0], r, sem.at[b]).start(priority=0)
            _row_copy(yb_ref, d_ref[2 * (first + r) + 1], bufs[b].at[1], r, sem.at[b]).start(priority=1)

    def wait(b):
        for k in range(2):
            pltpu.make_async_copy(yb_ref.at[pl.ds(0, th)], bufs[b].at[k], sem.at[b]).wait()

    def finish(rows, b):
        rw = rw_ref[rows, :]
        y0 = bufs[b][0].reshape(th, D_MODEL)
        y1 = bufs[b][1].reshape(th, D_MODEL)
        x1 = x_ref[rows, :] + rw[:, 0:1] * y0 + rw[:, 1:2] * y1
        x2 = _ple(x1, p_ref[rows, :], gple_ref[...], wpg_ref[...], wpp_ref[...])
        o_ref[rows, :] = _rms(x2, gfin_ref[...], RMS_EPS)

    @pl.when(i == 0)
    def _():
        gather(dest_ref, 0, 0)

    wait(0)
    gather(dest_ref, th, 1)
    finish(slice(0, th), 0)
    wait(1)
    gather(dest_next_ref, 0, 0)
    finish(slice(th, tp), 1)

    @pl.when(i == pl.num_programs(0) - 1)
    def _():
        wait(0)


def _combine_ple(dest_flat, x2, rw, p2, gple, wpg, wpp, gfin, yb, *, tp=512):
    T = x2.shape[0]
    n = T // tp
    return pl.pallas_call(
        functools.partial(_combine_ple_kernel, tp=tp),
        out_shape=jax.ShapeDtypeStruct((T, D_MODEL), F32),
        grid=(n,),
        in_specs=[
            pl.BlockSpec((2 * tp,), lambda i: (i,), memory_space=pltpu.SMEM),
            pl.BlockSpec((2 * tp,), lambda i: (jnp.minimum(i + 1, n - 1),), memory_space=pltpu.SMEM),
            pl.BlockSpec((tp, D_MODEL), lambda i: (i, 0)),
            pl.BlockSpec((tp, LANES), lambda i: (i, 0)),
            pl.BlockSpec((tp, PLE_DIM), lambda i: (i, 0)),
            _once((1, D_MODEL)),
            _once((D_MODEL, D_MODEL)),
            _once((PLE_DIM, D_MODEL)),
            _once((1, D_MODEL)),
            pl.BlockSpec(memory_space=pl.ANY),
        ],
        out_specs=pl.BlockSpec((tp, D_MODEL), lambda i: (i, 0)),
        scratch_shapes=[pltpu.VMEM((2, tp // 2) + ROW_TILE, F32), pltpu.VMEM((2, tp // 2) + ROW_TILE, F32),
                        pltpu.SemaphoreType.DMA((2,))],
        compiler_params=_params("arbitrary"),
        name="moe_combine_ple",
    )(dest_flat, dest_flat, x2, rw, p2, gple, wpg, wpp, gfin, yb)


def _moe_ple_final(x2, p2, gffn, w_router, b_router, wg, wu, wd, gple, wpg, wpp, gfin, *, tmm=512):
    T = x2.shape[0]
    wr = jnp.zeros((D_MODEL, LANES), F32).at[:, :N_EXPERTS].set(w_router)
    whi = wr.astype(BF16)
    wlo = (wr - whi.astype(F32)).astype(BF16)
    br = jnp.full((1, LANES), NEG_BIG, F32).at[0, :N_EXPERTS].set(b_router)
    h, ri, rw, cnt = _router(x2, gffn, jnp.concatenate([whi, wlo], axis=1), br)
    counts = cnt[0, :N_EXPERTS].astype(jnp.int32)
    padded = (counts + tmm - 1) // tmm * tmm
    pad_end = jnp.cumsum(padded).astype(jnp.int32)
    pad_start = pad_end - padded
    e_sel = ri[:, 0:2, None] == jnp.arange(N_EXPERTS, dtype=jnp.int32)
    dest = jnp.sum(jnp.where(e_sel, pad_start, 0), axis=-1) + ri[:, 2:4]
    dest_flat = dest.reshape(2 * T).astype(jnp.int32)
    n_blocks = (2 * T) // tmm + N_EXPERTS
    block_start = jnp.arange(n_blocks, dtype=jnp.int32) * tmm
    block_e = jnp.minimum(jnp.sum(block_start[:, None] >= pad_end[None, :], axis=1), N_EXPERTS - 1).astype(jnp.int32)
    n_used = (pad_end[-1] // tmm).astype(jnp.int32).reshape(1)
    xs = _dispatch(pad_end, n_used, dest_flat, h, n_blocks=n_blocks, tmm=tmm)
    yb = _gmm(block_e, n_used, xs, wg.astype(BF16), wu.astype(BF16), wd.astype(BF16), tmm=tmm)
    return _combine_ple(dest_flat, x2, rw, p2, gple, wpg, wpp, gfin, yb)


def kernel(x, p, g_mix, w_in, w_gk_up, b_gk, g_gla_norm, conv_w, conv_b, ln_conv_g, ln_conv_b, w_conv_out, w_pool, pool_scale, w_out, g_ffn, w_ffn_gate, w_ffn_up, w_ffn_down, w_router, b_router, w_moe_gate, w_moe_up, w_moe_down, g_ple, w_ple_gate, w_ple_proj, g_final):
    B, S, _ = x.shape
    T = B * S
    depth = w_in.shape[0]
    assert depth == 2 and x.shape[2] == D_MODEL, "dense layer 0, routed layer 1 with the final norm fused"
    x2 = x.reshape(T, D_MODEL)
    row = lambda a: a.reshape(1, -1)
    for i in range(depth):
        wi = w_in[i]
        w_main = jnp.concatenate([wi[:, :3072], wi[:, 3088:]], axis=1).astype(BF16)
        w_gk = jnp.zeros((D_MODEL, GK_PAD), F32).at[:, :GLA_GATE_RANK].set(wi[:, 3072:3088]).astype(BF16)
        w_gk_up_p = jnp.zeros((GK_PAD, GLA_KEY_DIM), F32).at[:GLA_GATE_RANK].set(w_gk_up[i]).astype(BF16)
        z, zgk = _in_proj(x2, row(g_mix[i]), w_main, w_gk)
        x2 = _mixer(z, zgk, x2, w_gk_up_p, row(b_gk[i]), row(g_gla_norm[i]), conv_w[i], row(conv_b[i]),
                    row(ln_conv_g[i]), row(ln_conv_b[i]), w_conv_out[i].astype(BF16), w_pool[i].astype(BF16),
                    row(pool_scale[i]), w_out[i].astype(BF16), B=B, S=S)
        p2 = p[i].reshape(T, PLE_DIM)
        wpg = w_ple_gate[i].astype(BF16)
        wpp = w_ple_proj[i].astype(BF16)
        if i % 2 == 0:
            j = i // 2
            wgu = jnp.concatenate([w_ffn_gate[j].astype(BF16), w_ffn_up[j].astype(BF16)], axis=1)
            x2 = _ffn_ple(x2, p2, row(g_ffn[i]), wgu, w_ffn_down[j].astype(BF16), row(g_ple[i]), wpg, wpp)
        else:
            j = i // 2
            x2 = _moe_ple_final(x2, p2, row(g_ffn[i]), w_router[j], b_router[j], w_moe_gate[j], w_moe_up[j],
                                w_moe_down[j], row(g_ple[i]), wpg, wpp, row(g_final))
    return x2.reshape(B, S, D_MODEL)
```

```python
import functools

import jax
import jax.numpy as jnp
from jax import lax
from jax.experimental import pallas as pl
from jax.experimental.pallas import tpu as pltpu

F32 = jnp.float32
BF16 = jnp.bfloat16

D_MODEL = 1024
GLA_HEADS = 4
GLA_KEY_DIM = 512
GLA_HEAD_K = 128
GLA_HEAD_V = 256
GLA_GATE_RANK = 16
GLA_GATE_NORMALIZER = 16.0
GLA_CHUNK = 64
GLA_NORM_EPS = 1e-5
CONV_WIDTH = 31
LN_EPS = 1e-5
POOL_WINDOWS = (2, 4, 8, 16)
POOL_GROUP_CH = 256
N_EXPERTS = 8
PLE_DIM = 256
RMS_EPS = 1e-6

Z_Q, Z_K, Z_V, Z_G, Z_CA, Z_CG, Z_POOL, Z_GATE = 0, 512, 1024, 2048, 3072, 4096, 5120, 6144
Z_DIM = 9216
GK_PAD = 128
LANES = 128

CONV_HALO = 32
POOL_HALO = 128
CONV_RC = 256
VMEM_LIMIT = 56 * 1024 * 1024
GMM_VMEM_LIMIT = 60 * 1024 * 1024

NEG_BIG = -1e30
MXU_PASS_COLS = 512
ROW_TILE = (D_MODEL // LANES, LANES)


def _dot(a, b):
    return jnp.dot(a, b, preferred_element_type=F32)


def _dot_nt(a, b):
    return lax.dot_general(a, b, (((1,), (1,)), ((), ())), preferred_element_type=F32)


def _dot_tn(a, b):
    return lax.dot_general(a, b, (((0,), (0,)), ((), ())), preferred_element_type=F32)


def _sigmoid(x):
    return 0.5 * jnp.tanh(0.5 * x) + 0.5


def _rms(x, g, eps):
    return x * lax.rsqrt(jnp.mean(x * x, axis=-1, keepdims=True) + eps) * g


def _split3(x):
    hi = x.astype(BF16)
    r1 = x - hi.astype(F32)
    mid = r1.astype(BF16)
    lo = (r1 - mid.astype(F32)).astype(BF16)
    return hi, mid, lo


def _once(shape):
    n = len(shape)
    return pl.BlockSpec(shape, lambda *_: (0,) * n, pipeline_mode=pl.Buffered(1))


def _params(*semantics):
    return pltpu.CompilerParams(dimension_semantics=semantics, vmem_limit_bytes=VMEM_LIMIT)


def _in_proj_kernel(x_ref, g_ref, w_ref, wgk_ref, z_ref, zgk_ref, *, nc):
    h = _rms(x_ref[...], g_ref[...], RMS_EPS).astype(BF16)
    for c in range(0, Z_DIM, nc):
        z_ref[:, c:c + nc] = _dot(h, w_ref[:, c:c + nc]).astype(BF16)
    zgk_ref[...] = _dot(h, wgk_ref[...])


def _in_proj(x2, g, w_main, w_gk, *, tm=512, nc=1024):
    T = x2.shape[0]
    return pl.pallas_call(
        functools.partial(_in_proj_kernel, nc=nc),
        out_shape=(jax.ShapeDtypeStruct((T, Z_DIM), BF16), jax.ShapeDtypeStruct((T, GK_PAD), F32)),
        grid=(T // tm,),
        in_specs=[
            pl.BlockSpec((tm, D_MODEL), lambda i: (i, 0)),
            _once((1, D_MODEL)),
            _once((D_MODEL, Z_DIM)),
            _once((D_MODEL, GK_PAD)),
        ],
        out_specs=(pl.BlockSpec((tm, Z_DIM), lambda i: (i, 0)), pl.BlockSpec((tm, GK_PAD), lambda i: (i, 0))),
        compiler_params=_params("arbitrary"),
        name="in_proj",
    )(x2, g, w_main, w_gk)


def _mixer_kernel(z_ref, zgk_ref, x_ref, wgk_ref, bgk_ref, gnorm_ref, convw_ref, convb_ref, lng_ref, lnb_ref,
                  wco_ref, wpool_ref, pscale_ref, wout_ref, o_ref,
                  state_ref, ubuf_ref, pbuf_ref, ygla_ref, yconv_ref, *, ts):
    s = pl.program_id(1)
    C = GLA_CHUNK

    @pl.when(s == 0)
    def _():
        state_ref[...] = jnp.zeros_like(state_ref)
        ubuf_ref[0:CONV_HALO, :] = jnp.zeros((CONV_HALO, D_MODEL), F32)
        pbuf_ref[0:POOL_HALO, :] = jnp.zeros((POOL_HALO, D_MODEL), BF16)

    gk_pre = _dot(zgk_ref[...].astype(BF16), wgk_ref[...]) + bgk_ref[...]
    ca = z_ref[:, Z_CA:Z_CA + D_MODEL].astype(F32)
    cg = z_ref[:, Z_CG:Z_CG + D_MODEL].astype(F32)
    ubuf_ref[CONV_HALO:CONV_HALO + ts, :] = ca * _sigmoid(cg)
    gk = (jnp.minimum(gk_pre, 0.0) - jnp.log(1.0 + jnp.exp(-jnp.abs(gk_pre)))) * (1.0 / GLA_GATE_NORMALIZER)
    row = lax.broadcasted_iota(jnp.int32, (ts, ts), 0)
    col = lax.broadcasted_iota(jnp.int32, (ts, ts), 1)
    same_chunk_causal = (jnp.right_shift(row, 6) == jnp.right_shift(col, 6)) & (col <= row)
    tri = same_chunk_causal.astype(BF16)
    g_hi, g_mid, g_lo = _split3(gk)
    b_all = _dot(tri, g_hi) + _dot(tri, g_mid) + _dot(tri, g_lo)
    nchunk = ts // C

    pbuf_ref[POOL_HALO:POOL_HALO + ts, :] = z_ref[:, Z_POOL:Z_POOL + D_MODEL]
    tpos = s * ts + lax.broadcasted_iota(jnp.int32, (ts, 1), 0)
    prow = lax.broadcasted_iota(jnp.int32, (ts, POOL_HALO + ts), 0) + POOL_HALO
    pcol = lax.broadcasted_iota(jnp.int32, (ts, POOL_HALO + ts), 1)
    y_pool = []
    for gi_, w in enumerate(POOL_WINDOWS):
        cs = slice(gi_ * POOL_GROUP_CH, (gi_ + 1) * POOL_GROUP_CH)
        band = ((pcol <= prow) & (pcol > prow - w)).astype(BF16)
        wsum = _dot(band, pbuf_ref[:, cs])
        inv_cnt = 1.0 / jnp.minimum(tpos + 1, w).astype(F32)
        d = wsum * inv_cnt - z_ref[:, Z_POOL + gi_ * POOL_GROUP_CH:Z_POOL + (gi_ + 1) * POOL_GROUP_CH].astype(F32)
        y_pool.append(_dot(d.astype(BF16), wpool_ref[gi_]) * pscale_ref[:, cs])
    pbuf_ref[0:POOL_HALO, :] = pbuf_ref[ts:ts + POOL_HALO, :]

    def per_chunk_row(r):
        return jnp.concatenate(
            [jnp.broadcast_to(b_all[c * C + r:c * C + r + 1, :], (C, GLA_KEY_DIM)) for c in range(nchunk)], axis=0)

    b_mid = per_chunk_row(C // 2 - 1)
    b_last = per_chunk_row(C - 1)
    q = z_ref[:, Z_Q:Z_Q + GLA_KEY_DIM].astype(F32) * (GLA_HEAD_K ** -0.5)
    k = z_ref[:, Z_K:Z_K + GLA_KEY_DIM].astype(F32)
    qa = (q * jnp.exp(b_all - b_mid)).astype(BF16)
    ka = (k * jnp.exp(b_mid - b_all)).astype(BF16)
    qd = (q * jnp.exp(b_all)).astype(BF16)
    kd = (k * jnp.exp(b_last - b_all)).astype(BF16)
    e_dec = [jnp.exp(b_all[c * C + C - 1:c * C + C, :]) for c in range(nchunk)]
    gnorm = gnorm_ref[...]
    for h in range(GLA_HEADS):
        ks = slice(h * GLA_HEAD_K, (h + 1) * GLA_HEAD_K)
        v = z_ref[:, Z_V + h * GLA_HEAD_V:Z_V + (h + 1) * GLA_HEAD_V]
        attn = jnp.where(same_chunk_causal, _dot_nt(qa[:, ks], ka[:, ks]), 0.0).astype(BF16)
        o = _dot(attn, v)
        st = state_ref[h]
        o_inter = []
        for c in range(nchunk):
            rs = slice(c * C, (c + 1) * C)
            o_inter.append(_dot_nt(qd[rs, ks], st.astype(BF16)))
            st = st * e_dec[c][:, ks] + _dot_tn(v[rs, :], kd[rs, ks])
        state_ref[h] = st
        o = o + jnp.concatenate(o_inter, axis=0)
        o = o * lax.rsqrt(jnp.mean(o * o, axis=-1, keepdims=True) + GLA_NORM_EPS) * gnorm
        gi = z_ref[:, Z_G + h * GLA_HEAD_V:Z_G + (h + 1) * GLA_HEAD_V].astype(F32)
        ygla_ref[:, h * GLA_HEAD_V:(h + 1) * GLA_HEAD_V] = o * (gi * _sigmoid(gi))

    RC = CONV_RC

    def conv_rows(i, carry):
        base = pl.multiple_of(i * RC, RC)
        for cc in range(D_MODEL // LANES):
            cs = slice(cc * LANES, (cc + 1) * LANES)
            acc = None
            for r in range(8):
                n_rows = RC if r == 0 else RC + 8
                part = None
                for a in range(5):
                    o = 8 * a + r
                    if o < 2 or o > CONV_HALO:
                        continue
                    kk = o - 2
                    term = convw_ref[kk:kk + 1, cs] * ubuf_ref[pl.ds(base + 8 * a, n_rows), cs]
                    part = term if part is None else part + term
                part = part[r:r + RC, :]
                acc = part if acc is None else acc + part
            yconv_ref[pl.ds(base, RC), cs] = acc + convb_ref[:, cs]
        return carry

    lax.fori_loop(0, ts // RC, conv_rows, 0)
    ubuf_ref[0:CONV_HALO, :] = ubuf_ref[ts:ts + CONV_HALO, :]
    yc = yconv_ref[...]
    mu = jnp.mean(yc, axis=-1, keepdims=True)
    ycc = yc - mu
    var = jnp.mean(ycc * ycc, axis=-1, keepdims=True)
    yn = ycc * lax.rsqrt(var + LN_EPS) * lng_ref[...] + lnb_ref[...]
    y_conv = _dot((yn * _sigmoid(yn)).astype(BF16), wco_ref[...])

    mix_parts = []
    for gi_ in range(4):
        cs = slice(gi_ * POOL_GROUP_CH, (gi_ + 1) * POOL_GROUP_CH)
        g0 = _sigmoid(z_ref[:, Z_GATE + gi_ * 256:Z_GATE + (gi_ + 1) * 256])
        g1 = _sigmoid(z_ref[:, Z_GATE + D_MODEL + gi_ * 256:Z_GATE + D_MODEL + (gi_ + 1) * 256])
        g2 = _sigmoid(z_ref[:, Z_GATE + 2 * D_MODEL + gi_ * 256:Z_GATE + 2 * D_MODEL + (gi_ + 1) * 256])
        m = g0 * ygla_ref[:, cs].astype(BF16) + g1 * y_conv[:, cs].astype(BF16) + g2 * y_pool[gi_].astype(BF16)
        mix_parts.append(m)
    o_ref[...] = x_ref[...] + _dot(jnp.concatenate(mix_parts, axis=1), wout_ref[...])


def _mixer(z, zgk, x2, wgk, bgk, gnorm, convw, convb, lng, lnb, wco, wpool, pscale, wout, *, B, S, ts=256):
    T = B * S
    nt = S // ts
    rowmap = lambda b, s: (b * nt + s, 0)
    return pl.pallas_call(
        functools.partial(_mixer_kernel, ts=ts),
        out_shape=jax.ShapeDtypeStruct((T, D_MODEL), F32),
        grid=(B, nt),
        in_specs=[
            pl.BlockSpec((ts, Z_DIM), rowmap),
            pl.BlockSpec((ts, GK_PAD), rowmap),
            pl.BlockSpec((ts, D_MODEL), rowmap),
            _once((GK_PAD, GLA_KEY_DIM)),
            _once((1, GLA_KEY_DIM)),
            _once((1, GLA_HEAD_V)),
            _once((CONV_WIDTH, D_MODEL)),
            _once((1, D_MODEL)),
            _once((1, D_MODEL)),
            _once((1, D_MODEL)),
            _once((D_MODEL, D_MODEL)),
            _once((4, POOL_GROUP_CH, POOL_GROUP_CH)),
            _once((1, D_MODEL)),
            _once((D_MODEL, D_MODEL)),
        ],
        out_specs=pl.BlockSpec((ts, D_MODEL), rowmap),
        scratch_shapes=[
            pltpu.VMEM((GLA_HEADS, GLA_HEAD_V, GLA_HEAD_K), F32),
            pltpu.VMEM((CONV_HALO + ts, D_MODEL), F32),
            pltpu.VMEM((POOL_HALO + ts, D_MODEL), BF16),
            pltpu.VMEM((ts, D_MODEL), F32),
            pltpu.VMEM((ts, D_MODEL), F32),
        ],
        compiler_params=_params("arbitrary", "arbitrary"),
        name="token_mixer",
    )(z, zgk, x2, wgk, bgk, gnorm, convw, convb, lng, lnb, wco, wpool, pscale, wout)


def _ple(x1, p, gple, wpg, wpp):
    hp = _rms(x1, gple, RMS_EPS).astype(BF16)
    return x1 + _sigmoid(_dot(hp, wpg)) * _dot(p.astype(BF16), wpp)


def _ffn_ple_kernel(x_ref, p_ref, gffn_ref, wgu_ref, wd_ref, gple_ref, wpg_ref, wpp_ref, o_ref):
    x = x_ref[...]
    h = _rms(x, gffn_ref[...], RMS_EPS).astype(BF16)
    F = wd_ref.shape[0]
    gu = _dot(h, wgu_ref[...])
    g = gu[:, :F]
    u = gu[:, F:]
    y = x + _dot((g * _sigmoid(g) * u).astype(BF16), wd_ref[...])
    o_ref[...] = _ple(y, p_ref[...], gple_ref[...], wpg_ref[...], wpp_ref[...])


def _ffn_ple(x2, p2, gffn, wgu, wd, gple, wpg, wpp, *, tm=512):
    T = x2.shape[0]
    F = wd.shape[0]
    return pl.pallas_call(
        _ffn_ple_kernel,
        out_shape=jax.ShapeDtypeStruct((T, D_MODEL), F32),
        grid=(T // tm,),
        in_specs=[
            pl.BlockSpec((tm, D_MODEL), lambda i: (i, 0)),
            pl.BlockSpec((tm, PLE_DIM), lambda i: (i, 0)),
            _once((1, D_MODEL)),
            _once((D_MODEL, 2 * F)),
            _once((F, D_MODEL)),
            _once((1, D_MODEL)),
            _once((D_MODEL, D_MODEL)),
            _once((PLE_DIM, D_MODEL)),
        ],
        out_specs=pl.BlockSpec((tm, D_MODEL), lambda i: (i, 0)),
        compiler_params=_params("arbitrary"),
        name="ffn_ple",
    )(x2, p2, gffn, wgu, wd, gple, wpg, wpp)


def _router_kernel(x_ref, g_ref, whl_ref, br_ref, h_ref, ri_ref, rw_ref, cnt_ref, run_ref, *, tm):
    i = pl.program_id(0)

    @pl.when(i == 0)
    def _():
        run_ref[...] = jnp.zeros_like(run_ref)

    h = _rms(x_ref[...], g_ref[...], RMS_EPS)
    h_ref[...] = h.reshape((tm,) + ROW_TILE)
    hh = h.astype(BF16)
    hl = (h - hh.astype(F32)).astype(BF16)
    hh_w = _dot(hh, whl_ref[...])
    logits = hh_w[:, :LANES] + hh_w[:, LANES:] + _dot(hl, whl_ref[:, :LANES]) + br_ref[...]
    lane = lax.broadcasted_iota(jnp.int32, (tm, LANES), 1)
    m1 = jnp.max(logits, axis=-1, keepdims=True)
    i1 = jnp.min(jnp.where(logits == m1, lane, LANES), axis=-1, keepdims=True)
    l2 = jnp.where(lane == i1, -jnp.inf, logits)
    m2 = jnp.max(l2, axis=-1, keepdims=True)
    i2 = jnp.min(jnp.where(l2 == m2, lane, LANES), axis=-1, keepdims=True)
    e = jnp.exp(m2 - m1)
    w1 = 1.0 / (1.0 + e)
    w2 = e / (1.0 + e)
    sel1 = lane == i1
    sel2 = lane == i2
    onehot = (sel1 | sel2).astype(BF16)
    row = lax.broadcasted_iota(jnp.int32, (tm, tm), 0)
    col = lax.broadcasted_iota(jnp.int32, (tm, tm), 1)
    before = _dot((col < row).astype(BF16), onehot) + run_ref[...]
    pos1 = jnp.sum(jnp.where(sel1, before, 0.0), axis=-1, keepdims=True).astype(jnp.int32)
    pos2 = jnp.sum(jnp.where(sel2, before, 0.0), axis=-1, keepdims=True).astype(jnp.int32)
    run = run_ref[...] + jnp.sum(onehot.astype(F32), axis=0, keepdims=True)
    run_ref[...] = run
    cnt_ref[...] = run
    ri_ref[...] = jnp.where(lane == 0, i1, jnp.where(lane == 1, i2, jnp.where(lane == 2, pos1, jnp.where(lane == 3, pos2, 0))))
    rw_ref[...] = jnp.where(lane == 0, w1, jnp.where(lane == 1, w2, 0.0))


def _router(x2, g, whl, br, *, tm=512):
    T = x2.shape[0]
    return pl.pallas_call(
        functools.partial(_router_kernel, tm=tm),
        out_shape=(
            jax.ShapeDtypeStruct((T,) + ROW_TILE, F32),
            jax.ShapeDtypeStruct((T, LANES), jnp.int32),
            jax.ShapeDtypeStruct((T, LANES), F32),
            jax.ShapeDtypeStruct((1, LANES), F32),
        ),
        grid=(T // tm,),
        in_specs=[
            pl.BlockSpec((tm, D_MODEL), lambda i: (i, 0)),
            _once((1, D_MODEL)),
            _once((D_MODEL, 2 * LANES)),
            _once((1, LANES)),
        ],
        out_specs=(
            pl.BlockSpec((tm,) + ROW_TILE, lambda i: (i, 0, 0)),
            pl.BlockSpec((tm, LANES), lambda i: (i, 0)),
            pl.BlockSpec((tm, LANES), lambda i: (i, 0)),
            pl.BlockSpec((1, LANES), lambda i: (0, 0)),
        ),
        scratch_shapes=[pltpu.VMEM((1, LANES), F32)],
        compiler_params=_params("arbitrary"),
        name="router",
    )(x2, g, whl, br)


def _row_copy(src_ref, src_row, dst_ref, dst_row, sem):
    return pltpu.make_async_copy(src_ref.at[pl.ds(src_row, 1)], dst_ref.at[pl.ds(dst_row, 1)], sem)


def _dispatch_kernel(pe_ref, nu_ref, dest_ref, h_ref, xs_ref, zero_ref, sem, *, td, tmm, n_blocks):
    @pl.when(pl.program_id(0) == 0)
    def _():
        zero_ref[...] = jnp.zeros_like(zero_ref)

        def fill(start):
            return pltpu.make_async_copy(zero_ref, xs_ref.at[pl.ds(pl.multiple_of(start, tmm), tmm)], sem)

        for e in range(N_EXPERTS):
            prev_end = pe_ref[e - 1] if e > 0 else 0

            @pl.when(pe_ref[e] > prev_end)
            def _():
                fill(pe_ref[e] - tmm).start()
                fill(pe_ref[e] - tmm).wait()

        for j in range(N_EXPERTS):
            @pl.when(n_blocks - 1 - j >= nu_ref[0])
            def _():
                fill((n_blocks - 1 - j) * tmm).start()
                fill((n_blocks - 1 - j) * tmm).wait()

    for r in range(td):
        _row_copy(h_ref, r, xs_ref, dest_ref[2 * r], sem).start(priority=0)
        _row_copy(h_ref, r, xs_ref, dest_ref[2 * r + 1], sem).start(priority=1)
    for _ in range(2):
        pltpu.make_async_copy(h_ref, xs_ref.at[pl.ds(0, td)], sem).wait()


def _dispatch(pad_end, n_used, dest_flat, h, *, n_blocks, tmm, td=512):
    T = h.shape[0]
    return pl.pallas_call(
        functools.partial(_dispatch_kernel, td=td, tmm=tmm, n_blocks=n_blocks),
        out_shape=jax.ShapeDtypeStruct((n_blocks * tmm,) + ROW_TILE, F32),
        grid_spec=pltpu.PrefetchScalarGridSpec(
            num_scalar_prefetch=2,
            grid=(T // td,),
            in_specs=[
                pl.BlockSpec((2 * td,), lambda i, pe, nu: (i,), memory_space=pltpu.SMEM),
                pl.BlockSpec((td,) + ROW_TILE, lambda i, pe, nu: (i, 0, 0)),
            ],
            out_specs=pl.BlockSpec(memory_space=pl.ANY),
            scratch_shapes=[pltpu.VMEM((tmm,) + ROW_TILE, F32), pltpu.SemaphoreType.DMA],
        ),
        compiler_params=_params("arbitrary"),
        name="moe_dispatch",
    )(pad_end, n_used, dest_flat, h)


def _gmm_kernel(be_ref, nu_ref, xs_ref, wg_ref, wu_ref, wd_ref, o_ref, *, f_chunks):
    i = pl.program_id(0)

    @pl.when(i < nu_ref[0])
    def _():
        x = xs_ref[...].reshape(xs_ref.shape[0], D_MODEL).astype(BF16)
        y = None
        for lo, hi in f_chunks:
            g = _dot(x, wg_ref[0, :, lo:hi])
            u = _dot(x, wu_ref[0, :, lo:hi])
            part = _dot((g * _sigmoid(g) * u).astype(BF16), wd_ref[0, lo:hi, :])
            y = part if y is None else y + part
        o_ref[...] = y.reshape(o_ref.shape)

    @pl.when(i >= nu_ref[0])
    def _():
        o_ref[...] = jnp.zeros_like(o_ref)


def _gmm(block_e, n_used, xs, wg, wu, wd, *, tmm):
    n_rows = xs.shape[0]
    n_blocks = n_rows // tmm
    F = wd.shape[1]
    n_pass = F // MXU_PASS_COLS
    assert n_pass * MXU_PASS_COLS == F
    cut = (n_pass // 2) * MXU_PASS_COLS
    f_chunks = ((0, cut), (cut, F))

    def expert_spec(shape):
        return pl.BlockSpec((1,) + shape, lambda i, be, nu: (be[i], 0, 0))

    return pl.pallas_call(
        functools.partial(_gmm_kernel, f_chunks=f_chunks),
        out_shape=jax.ShapeDtypeStruct((n_rows,) + ROW_TILE, F32),
        grid_spec=pltpu.PrefetchScalarGridSpec(
            num_scalar_prefetch=2,
            grid=(n_blocks,),
            in_specs=[
                pl.BlockSpec((tmm,) + ROW_TILE, lambda i, be, nu: (jnp.minimum(i, nu[0] - 1), 0, 0)),
                expert_spec((D_MODEL, F)),
                expert_spec((D_MODEL, F)),
                expert_spec((F, D_MODEL)),
            ],
            out_specs=pl.BlockSpec((tmm,) + ROW_TILE, lambda i, be, nu: (i, 0, 0)),
        ),
        compiler_params=pltpu.CompilerParams(dimension_semantics=("arbitrary",), vmem_limit_bytes=GMM_VMEM_LIMIT),
        name="moe_grouped_swiglu",
    )(block_e, n_used, xs, wg, wu, wd)


def _combine_ple_kernel(dest_ref, dest_next_ref, x_ref, rw_ref, p_ref, gple_ref, wpg_ref, wpp_ref, gfin_ref,
                        yb_ref, o_ref, bufa_ref, bufb_ref, sem, *, tp):
    i = pl.program_id(0)
    th = tp // 2
    bufs = (bufa_ref, bufb_ref)

    def gather(d_ref, first, b):
        for r in range(th):
            _row_copy(yb_ref, d_ref[2 * (first + r)], bufs[b].at[0], r, sem.at[b]).start(priority=0)
            _row_copy(yb_ref, d_ref[2 * (first + r) + 1], bufs[b].at[1], r, sem.at[b]).start(priority=1)

    def wait(b):
        for k in range(2):
            pltpu.make_async_copy(yb_ref.at[pl.ds(0, th)], bufs[b].at[k], sem.at[b]).wait()

    def finish(rows, b):
        rw = rw_ref[rows, :]
        y0 = bufs[b][0].reshape(th, D_MODEL)
        y1 = bufs[b][1].reshape(th, D_MODEL)
        x1 = x_ref[rows, :] + rw[:, 0:1] * y0 + rw[:, 1:2] * y1
        x2 = _ple(x1, p_ref[rows, :], gple_ref[...], wpg_ref[...], wpp_ref[...])
        o_ref[rows, :] = _rms(x2, gfin_ref[...], RMS_EPS)

    @pl.when(i == 0)
    def _():
        gather(dest_ref, 0, 0)

    wait(0)
    gather(dest_ref, th, 1)
    finish(slice(0, th), 0)
    wait(1)
    gather(dest_next_ref, 0, 0)
    finish(slice(th, tp), 1)

    @pl.when(i == pl.num_programs(0) - 1)
    def _():
        wait(0)


def _combine_ple(dest_flat, x2, rw, p2, gple, wpg, wpp, gfin, yb, *, tp=512):
    T = x2.shape[0]
    n = T // tp
    return pl.pallas_call(
        functools.partial(_combine_ple_kernel, tp=tp),
        out_shape=jax.ShapeDtypeStruct((T, D_MODEL), F32),
        grid=(n,),
        in_specs=[
            pl.BlockSpec((2 * tp,), lambda i: (i,), memory_space=pltpu.SMEM),
            pl.BlockSpec((2 * tp,), lambda i: (jnp.minimum(i + 1, n - 1),), memory_space=pltpu.SMEM),
            pl.BlockSpec((tp, D_MODEL), lambda i: (i, 0)),
            pl.BlockSpec((tp, LANES), lambda i: (i, 0)),
            pl.BlockSpec((tp, PLE_DIM), lambda i: (i, 0)),
            _once((1, D_MODEL)),
            _once((D_MODEL, D_MODEL)),
            _once((PLE_DIM, D_MODEL)),
            _once((1, D_MODEL)),
            pl.BlockSpec(memory_space=pl.ANY),
        ],
        out_specs=pl.BlockSpec((tp, D_MODEL), lambda i: (i, 0)),
        scratch_shapes=[pltpu.VMEM((2, tp // 2) + ROW_TILE, F32), pltpu.VMEM((2, tp // 2) + ROW_TILE, F32),
                        pltpu.SemaphoreType.DMA((2,))],
        compiler_params=_params("arbitrary"),
        name="moe_combine_ple",
    )(dest_flat, dest_flat, x2, rw, p2, gple, wpg, wpp, gfin, yb)


def _moe_ple_final(x2, p2, gffn, w_router, b_router, wg, wu, wd, gple, wpg, wpp, gfin, *, tmm=512):
    T = x2.shape[0]
    wr = jnp.zeros((D_MODEL, LANES), F32).at[:, :N_EXPERTS].set(w_router)
    whi = wr.astype(BF16)
    wlo = (wr - whi.astype(F32)).astype(BF16)
    br = jnp.full((1, LANES), NEG_BIG, F32).at[0, :N_EXPERTS].set(b_router)
    h, ri, rw, cnt = _router(x2, gffn, jnp.concatenate([whi, wlo], axis=1), br)
    counts = cnt[0, :N_EXPERTS].astype(jnp.int32)
    padded = (counts + tmm - 1) // tmm * tmm
    pad_end = jnp.cumsum(padded).astype(jnp.int32)
    pad_start = pad_end - padded
    e_sel = ri[:, 0:2, None] == jnp.arange(N_EXPERTS, dtype=jnp.int32)
    dest = jnp.sum(jnp.where(e_sel, pad_start, 0), axis=-1) + ri[:, 2:4]
    dest_flat = dest.reshape(2 * T).astype(jnp.int32)
    n_blocks = (2 * T) // tmm + N_EXPERTS
    block_start = jnp.arange(n_blocks, dtype=jnp.int32) * tmm
    block_e = jnp.minimum(jnp.sum(block_start[:, None] >= pad_end[None, :], axis=1), N_EXPERTS - 1).astype(jnp.int32)
    n_used = (pad_end[-1] // tmm).astype(jnp.int32).reshape(1)
    xs = _dispatch(pad_end, n_used, dest_flat, h, n_blocks=n_blocks, tmm=tmm)
    yb = _gmm(block_e, n_used, xs, wg.astype(BF16), wu.astype(BF16), wd.astype(BF16), tmm=tmm)
    return _combine_ple(dest_flat, x2, rw, p2, gple, wpg, wpp, gfin, yb)


def kernel(x, p, g_mix, w_in, w_gk_up, b_gk, g_gla_norm, conv_w, conv_b, ln_conv_g, ln_conv_b, w_conv_out, w_pool, pool_scale, w_out, g_ffn, w_ffn_gate, w_ffn_up, w_ffn_down, w_router, b_router, w_moe_gate, w_moe_up, w_moe_down, g_ple, w_ple_gate, w_ple_proj, g_final):
    B, S, _ = x.shape
    T = B * S
    depth = w_in.shape[0]
    assert depth == 2 and x.shape[2] == D_MODEL, "dense layer 0, routed layer 1 with the final norm fused"
    x2 = x.reshape(T, D_MODEL)
    row = lambda a: a.reshape(1, -1)
    for i in range(depth):
        wi = w_in[i]
        wb = wi.astype(BF16)
        w_main = jnp.concatenate([wb[:, :3072], wb[:, 3088:]], axis=1)
        w_gk = jnp.zeros((D_MODEL, GK_PAD), F32).at[:, :GLA_GATE_RANK].set(wi[:, 3072:3088]).astype(BF16)
        w_gk_up_p = jnp.zeros((GK_PAD, GLA_KEY_DIM), F32).at[:GLA_GATE_RANK].set(w_gk_up[i]).astype(BF16)
        z, zgk = _in_proj(x2, row(g_mix[i]), w_main, w_gk)
        x2 = _mixer(z, zgk, x2, w_gk_up_p, row(b_gk[i]), row(g_gla_norm[i]), conv_w[i], row(conv_b[i]),
                    row(ln_conv_g[i]), row(ln_conv_b[i]), w_conv_out[i].astype(BF16), w_pool[i].astype(BF16),
                    row(pool_scale[i]), w_out[i].astype(BF16), B=B, S=S)
        p2 = p[i].reshape(T, PLE_DIM)
        wpg = w_ple_gate[i].astype(BF16)
        wpp = w_ple_proj[i].astype(BF16)
        if i % 2 == 0:
            j = i // 2
            wgu = jnp.concatenate([w_ffn_gate[j].astype(BF16), w_ffn_up[j].astype(BF16)], axis=1)
            x2 = _ffn_ple(x2, p2, row(g_ffn[i]), wgu, w_ffn_down[j].astype(BF16), row(g_ple[i]), wpg, wpp)
        else:
            j = i // 2
            x2 = _moe_ple_final(x2, p2, row(g_ffn[i]), w_router[j], b_router[j], w_moe_gate[j], w_moe_up[j],
                                w_moe_down[j], row(g_ple[i]), wpg, wpp, row(g_final))
    return x2.reshape(B, S, D_MODEL)
```

```python
import functools

import jax
import jax.numpy as jnp
from jax import lax
from jax.experimental import pallas as pl
from jax.experimental.pallas import tpu as pltpu

F32 = jnp.float32
BF16 = jnp.bfloat16

D_MODEL = 1024
GLA_HEADS = 4
GLA_KEY_DIM = 512
GLA_HEAD_K = 128
GLA_HEAD_V = 256
GLA_GATE_RANK = 16
GLA_GATE_NORMALIZER = 16.0
GLA_CHUNK = 64
GLA_NORM_EPS = 1e-5
CONV_WIDTH = 31
LN_EPS = 1e-5
POOL_WINDOWS = (2, 4, 8, 16)
POOL_GROUP_CH = 256
N_EXPERTS = 8
PLE_DIM = 256
RMS_EPS = 1e-6

Z_Q, Z_K, Z_V, Z_G, Z_CA, Z_CG, Z_POOL, Z_GATE = 0, 512, 1024, 2048, 3072, 4096, 5120, 6144
Z_DIM = 9216
GK_PAD = 128
LANES = 128

CONV_HALO = 32
POOL_HALO = 128
CONV_RC = 256
VMEM_LIMIT = 56 * 1024 * 1024
GMM_VMEM_LIMIT = 60 * 1024 * 1024

NEG_BIG = -1e30
MXU_PASS_COLS = 512
ROW_TILE = (D_MODEL // LANES, LANES)


def _dot(a, b):
    return jnp.dot(a, b, preferred_element_type=F32)


def _dot_nt(a, b):
    return lax.dot_general(a, b, (((1,), (1,)), ((), ())), preferred_element_type=F32)


def _dot_tn(a, b):
    return lax.dot_general(a, b, (((0,), (0,)), ((), ())), preferred_element_type=F32)


def _sigmoid(x):
    return 0.5 * jnp.tanh(0.5 * x) + 0.5


def _rms(x, g, eps):
    return x * lax.rsqrt(jnp.mean(x * x, axis=-1, keepdims=True) + eps) * g


def _split3(x):
    hi = x.astype(BF16)
    r1 = x - hi.astype(F32)
    mid = r1.astype(BF16)
    lo = (r1 - mid.astype(F32)).astype(BF16)
    return hi, mid, lo


def _once(shape):
    n = len(shape)
    return pl.BlockSpec(shape, lambda *_: (0,) * n, pipeline_mode=pl.Buffered(1))


def _params(*semantics):
    return pltpu.CompilerParams(dimension_semantics=semantics, vmem_limit_bytes=VMEM_LIMIT)


def _in_proj_kernel(x_ref, g_ref, w_ref, wgk_ref, z_ref, zgk_ref, *, nc):
    h = _rms(x_ref[...], g_ref[...], RMS_EPS).astype(BF16)
    for c in range(0, Z_DIM, nc):
        z_ref[:, c:c + nc] = _dot(h, w_ref[:, c:c + nc]).astype(BF16)
    zgk_ref[...] = _dot(h, wgk_ref[...])


def _in_proj(x2, g, w_main, w_gk, *, tm=512, nc=1024):
    T = x2.shape[0]
    return pl.pallas_call(
        functools.partial(_in_proj_kernel, nc=nc),
        out_shape=(jax.ShapeDtypeStruct((T, Z_DIM), BF16), jax.ShapeDtypeStruct((T, GK_PAD), F32)),
        grid=(T // tm,),
        in_specs=[
            pl.BlockSpec((tm, D_MODEL), lambda i: (i, 0)),
            _once((1, D_MODEL)),
            _once((D_MODEL, Z_DIM)),
            _once((D_MODEL, GK_PAD)),
        ],
        out_specs=(pl.BlockSpec((tm, Z_DIM), lambda i: (i, 0)), pl.BlockSpec((tm, GK_PAD), lambda i: (i, 0))),
        compiler_params=_params("arbitrary"),
        name="in_proj",
    )(x2, g, w_main, w_gk)


def _mixer_kernel(z_ref, zgk_ref, x_ref, wgk_ref, bgk_ref, gnorm_ref, convw_ref, convb_ref, lng_ref, lnb_ref,
                  wco_ref, wpool_ref, pscale_ref, wout_ref, o_ref,
                  state_ref, ubuf_ref, pbuf_ref, ygla_ref, yconv_ref, *, ts):
    s = pl.program_id(1)
    C = GLA_CHUNK

    @pl.when(s == 0)
    def _():
        state_ref[...] = jnp.zeros_like(state_ref)
        ubuf_ref[0:CONV_HALO, :] = jnp.zeros((CONV_HALO, D_MODEL), F32)
        pbuf_ref[0:POOL_HALO, :] = jnp.zeros((POOL_HALO, D_MODEL), BF16)

    gk_pre = _dot(zgk_ref[...].astype(BF16), wgk_ref[...]) + bgk_ref[...]
    ca = z_ref[:, Z_CA:Z_CA + D_MODEL].astype(F32)
    cg = z_ref[:, Z_CG:Z_CG + D_MODEL].astype(F32)
    ubuf_ref[CONV_HALO:CONV_HALO + ts, :] = ca * _sigmoid(cg)
    gk = (jnp.minimum(gk_pre, 0.0) - jnp.log(1.0 + jnp.exp(-jnp.abs(gk_pre)))) * (1.0 / GLA_GATE_NORMALIZER)
    row = lax.broadcasted_iota(jnp.int32, (ts, ts), 0)
    col = lax.broadcasted_iota(jnp.int32, (ts, ts), 1)
    same_chunk_causal = (jnp.right_shift(row, 6) == jnp.right_shift(col, 6)) & (col <= row)
    tri = same_chunk_causal.astype(BF16)
    g_hi, g_mid, g_lo = _split3(gk)
    b_all = _dot(tri, g_hi) + _dot(tri, g_mid) + _dot(tri, g_lo)
    nchunk = ts // C

    pbuf_ref[POOL_HALO:POOL_HALO + ts, :] = z_ref[:, Z_POOL:Z_POOL + D_MODEL]
    tpos = s * ts + lax.broadcasted_iota(jnp.int32, (ts, 1), 0)
    prow = lax.broadcasted_iota(jnp.int32, (ts, POOL_HALO + ts), 0) + POOL_HALO
    pcol = lax.broadcasted_iota(jnp.int32, (ts, POOL_HALO + ts), 1)
    y_pool = []
    for gi_, w in enumerate(POOL_WINDOWS):
        cs = slice(gi_ * POOL_GROUP_CH, (gi_ + 1) * POOL_GROUP_CH)
        band = ((pcol <= prow) & (pcol > prow - w)).astype(BF16)
        wsum = _dot(band, pbuf_ref[:, cs])
        inv_cnt = 1.0 / jnp.minimum(tpos + 1, w).astype(F32)
        d = wsum * inv_cnt - z_ref[:, Z_POOL + gi_ * POOL_GROUP_CH:Z_POOL + (gi_ + 1) * POOL_GROUP_CH].astype(F32)
        y_pool.append(_dot(d.astype(BF16), wpool_ref[gi_]) * pscale_ref[:, cs])
    pbuf_ref[0:POOL_HALO, :] = pbuf_ref[ts:ts + POOL_HALO, :]

    def per_chunk_row(r):
        return jnp.concatenate(
            [jnp.broadcast_to(b_all[c * C + r:c * C + r + 1, :], (C, GLA_KEY_DIM)) for c in range(nchunk)], axis=0)

    b_mid = per_chunk_row(C // 2 - 1)
    b_last = per_chunk_row(C - 1)
    q = z_ref[:, Z_Q:Z_Q + GLA_KEY_DIM].astype(F32) * (GLA_HEAD_K ** -0.5)
    k = z_ref[:, Z_K:Z_K + GLA_KEY_DIM].astype(F32)
    qa = (q * jnp.exp(b_all - b_mid)).astype(BF16)
    ka = (k * jnp.exp(b_mid - b_all)).astype(BF16)
    qd = (q * jnp.exp(b_all)).astype(BF16)
    kd = (k * jnp.exp(b_last - b_all)).astype(BF16)
    e_dec = [jnp.exp(b_all[c * C + C - 1:c * C + C, :]) for c in range(nchunk)]
    gnorm = gnorm_ref[...]
    for h in range(GLA_HEADS):
        ks = slice(h * GLA_HEAD_K, (h + 1) * GLA_HEAD_K)
        v = z_ref[:, Z_V + h * GLA_HEAD_V:Z_V + (h + 1) * GLA_HEAD_V]
        attn = jnp.where(same_chunk_causal, _dot_nt(qa[:, ks], ka[:, ks]), 0.0).astype(BF16)
        o = _dot(attn, v)
        st = state_ref[h]
        o_inter = []
        for c in range(nchunk):
            rs = slice(c * C, (c + 1) * C)
            o_inter.append(_dot_nt(qd[rs, ks], st.astype(BF16)))
            st = st * e_dec[c][:, ks] + _dot_tn(v[rs, :], kd[rs, ks])
        state_ref[h] = st
        o = o + jnp.concatenate(o_inter, axis=0)
        o = o * lax.rsqrt(jnp.mean(o * o, axis=-1, keepdims=True) + GLA_NORM_EPS) * gnorm
        gi = z_ref[:, Z_G + h * GLA_HEAD_V:Z_G + (h + 1) * GLA_HEAD_V].astype(F32)
        ygla_ref[:, h * GLA_HEAD_V:(h + 1) * GLA_HEAD_V] = o * (gi * _sigmoid(gi))

    RC = CONV_RC

    def conv_rows(i, carry):
        base = pl.multiple_of(i * RC, RC)
        for cc in range(D_MODEL // LANES):
            cs = slice(cc * LANES, (cc + 1) * LANES)
            acc = None
            for r in range(8):
                n_rows = RC if r == 0 else RC + 8
                part = None
                for a in range(5):
                    o = 8 * a + r
                    if o < 2 or o > CONV_HALO:
                        continue
                    kk = o - 2
                    term = convw_ref[kk:kk + 1, cs] * ubuf_ref[pl.ds(base + 8 * a, n_rows), cs]
                    part = term if part is None else part + term
                part = part[r:r + RC, :]
                acc = part if acc is None else acc + part
            yconv_ref[pl.ds(base, RC), cs] = acc + convb_ref[:, cs]
        return carry

    lax.fori_loop(0, ts // RC, conv_rows, 0)
    ubuf_ref[0:CONV_HALO, :] = ubuf_ref[ts:ts + CONV_HALO, :]
    yc = yconv_ref[...]
    mu = jnp.mean(yc, axis=-1, keepdims=True)
    ycc = yc - mu
    var = jnp.mean(ycc * ycc, axis=-1, keepdims=True)
    yn = ycc * lax.rsqrt(var + LN_EPS) * lng_ref[...] + lnb_ref[...]
    y_conv = _dot((yn * _sigmoid(yn)).astype(BF16), wco_ref[...])

    mix_parts = []
    for gi_ in range(4):
        cs = slice(gi_ * POOL_GROUP_CH, (gi_ + 1) * POOL_GROUP_CH)
        g0 = _sigmoid(z_ref[:, Z_GATE + gi_ * 256:Z_GATE + (gi_ + 1) * 256])
        g1 = _sigmoid(z_ref[:, Z_GATE + D_MODEL + gi_ * 256:Z_GATE + D_MODEL + (gi_ + 1) * 256])
        g2 = _sigmoid(z_ref[:, Z_GATE + 2 * D_MODEL + gi_ * 256:Z_GATE + 2 * D_MODEL + (gi_ + 1) * 256])
        m = g0 * ygla_ref[:, cs].astype(BF16) + g1 * y_conv[:, cs].astype(BF16) + g2 * y_pool[gi_].astype(BF16)
        mix_parts.append(m)
    o_ref[...] = x_ref[...] + _dot(jnp.concatenate(mix_parts, axis=1), wout_ref[...])


def _mixer(z, zgk, x2, wgk, bgk, gnorm, convw, convb, lng, lnb, wco, wpool, pscale, wout, *, B, S, ts=256):
    T = B * S
    nt = S // ts
    rowmap = lambda b, s: (b * nt + s, 0)
    return pl.pallas_call(
        functools.partial(_mixer_kernel, ts=ts),
        out_shape=jax.ShapeDtypeStruct((T, D_MODEL), F32),
        grid=(B, nt),
        in_specs=[
            pl.BlockSpec((ts, Z_DIM), rowmap),
            pl.BlockSpec((ts, GK_PAD), rowmap),
            pl.BlockSpec((ts, D_MODEL), rowmap),
            _once((GK_PAD, GLA_KEY_DIM)),
            _once((1, GLA_KEY_DIM)),
            _once((1, GLA_HEAD_V)),
            _once((CONV_WIDTH, D_MODEL)),
            _once((1, D_MODEL)),
            _once((1, D_MODEL)),
            _once((1, D_MODEL)),
            _once((D_MODEL, D_MODEL)),
            _once((4, POOL_GROUP_CH, POOL_GROUP_CH)),
            _once((1, D_MODEL)),
            _once((D_MODEL, D_MODEL)),
        ],
        out_specs=pl.BlockSpec((ts, D_MODEL), rowmap),
        scratch_shapes=[
            pltpu.VMEM((GLA_HEADS, GLA_HEAD_V, GLA_HEAD_K), F32),
            pltpu.VMEM((CONV_HALO + ts, D_MODEL), F32),
            pltpu.VMEM((POOL_HALO + ts, D_MODEL), BF16),
            pltpu.VMEM((ts, D_MODEL), F32),
            pltpu.VMEM((ts, D_MODEL), F32),
        ],
        compiler_params=_params("arbitrary", "arbitrary"),
        name="token_mixer",
    )(z, zgk, x2, wgk, bgk, gnorm, convw, convb, lng, lnb, wco, wpool, pscale, wout)


def _ple(x1, p, gple, wpg, wpp):
    hp = _rms(x1, gple, RMS_EPS).astype(BF16)
    return x1 + _sigmoid(_dot(hp, wpg)) * _dot(p.astype(BF16), wpp)


def _ffn_ple_kernel(x_ref, p_ref, gffn_ref, wgu_ref, wd_ref, gple_ref, wpg_ref, wpp_ref, o_ref):
    x = x_ref[...]
    h = _rms(x, gffn_ref[...], RMS_EPS).astype(BF16)
    F = wd_ref.shape[0]
    gu = _dot(h, wgu_ref[...])
    g = gu[:, :F]
    u = gu[:, F:]
    y = x + _dot((g * _sigmoid(g) * u).astype(BF16), wd_ref[...])
    o_ref[...] = _ple(y, p_ref[...], gple_ref[...], wpg_ref[...], wpp_ref[...])


def _ffn_ple(x2, p2, gffn, wgu, wd, gple, wpg, wpp, *, tm=512):
    T = x2.shape[0]
    F = wd.shape[0]
    return pl.pallas_call(
        _ffn_ple_kernel,
        out_shape=jax.ShapeDtypeStruct((T, D_MODEL), F32),
        grid=(T // tm,),
        in_specs=[
            pl.BlockSpec((tm, D_MODEL), lambda i: (i, 0)),
            pl.BlockSpec((tm, PLE_DIM), lambda i: (i, 0)),
            _once((1, D_MODEL)),
            _once((D_MODEL, 2 * F)),
            _once((F, D_MODEL)),
            _once((1, D_MODEL)),
            _once((D_MODEL, D_MODEL)),
            _once((PLE_DIM, D_MODEL)),
        ],
        out_specs=pl.BlockSpec((tm, D_MODEL), lambda i: (i, 0)),
        compiler_params=_params("arbitrary"),
        name="ffn_ple",
    )(x2, p2, gffn, wgu, wd, gple, wpg, wpp)


def _router_kernel(x_ref, g_ref, whl_ref, br_ref, h_ref, ri_ref, rw_ref, cnt_ref, run_ref, *, tm):
    i = pl.program_id(0)

    @pl.when(i == 0)
    def _():
        run_ref[...] = jnp.zeros_like(run_ref)

    h = _rms(x_ref[...], g_ref[...], RMS_EPS)
    h_ref[...] = h.reshape((tm,) + ROW_TILE)
    hh = h.astype(BF16)
    hl = (h - hh.astype(F32)).astype(BF16)
    hh_w = _dot(hh, whl_ref[...])
    logits = hh_w[:, :LANES] + hh_w[:, LANES:] + _dot(hl, whl_ref[:, :LANES]) + br_ref[...]
    lane = lax.broadcasted_iota(jnp.int32, (tm, LANES), 1)
    m1 = jnp.max(logits, axis=-1, keepdims=True)
    i1 = jnp.min(jnp.where(logits == m1, lane, LANES), axis=-1, keepdims=True)
    l2 = jnp.where(lane == i1, -jnp.inf, logits)
    m2 = jnp.max(l2, axis=-1, keepdims=True)
    i2 = jnp.min(jnp.where(l2 == m2, lane, LANES), axis=-1, keepdims=True)
    e = jnp.exp(m2 - m1)
    w1 = 1.0 / (1.0 + e)
    w2 = e / (1.0 + e)
    sel1 = lane == i1
    sel2 = lane == i2
    onehot = (sel1 | sel2).astype(BF16)
    row = lax.broadcasted_iota(jnp.int32, (tm, tm), 0)
    col = lax.broadcasted_iota(jnp.int32, (tm, tm), 1)
    before = _dot((col < row).astype(BF16), onehot) + run_ref[...]
    pos1 = jnp.sum(jnp.where(sel1, before, 0.0), axis=-1, keepdims=True).astype(jnp.int32)
    pos2 = jnp.sum(jnp.where(sel2, before, 0.0), axis=-1, keepdims=True).astype(jnp.int32)
    run = run_ref[...] + jnp.sum(onehot.astype(F32), axis=0, keepdims=True)
    run_ref[...] = run
    cnt_ref[...] = run
    ri_ref[...] = jnp.where(lane == 0, i1, jnp.where(lane == 1, i2, jnp.where(lane == 2, pos1, jnp.where(lane == 3, pos2, 0))))
    rw_ref[...] = jnp.where(lane == 0, w1, jnp.where(lane == 1, w2, 0.0))


def _router(x2, g, whl, br, *, tm=512):
    T = x2.shape[0]
    return pl.pallas_call(
        functools.partial(_router_kernel, tm=tm),
        out_shape=(
            jax.ShapeDtypeStruct((T,) + ROW_TILE, F32),
            jax.ShapeDtypeStruct((T, LANES), jnp.int32),
            jax.ShapeDtypeStruct((T, LANES), F32),
            jax.ShapeDtypeStruct((1, LANES), F32),
        ),
        grid=(T // tm,),
        in_specs=[
            pl.BlockSpec((tm, D_MODEL), lambda i: (i, 0)),
            _once((1, D_MODEL)),
            _once((D_MODEL, 2 * LANES)),
            _once((1, LANES)),
        ],
        out_specs=(
            pl.BlockSpec((tm,) + ROW_TILE, lambda i: (i, 0, 0)),
            pl.BlockSpec((tm, LANES), lambda i: (i, 0)),
            pl.BlockSpec((tm, LANES), lambda i: (i, 0)),
            pl.BlockSpec((1, LANES), lambda i: (0, 0)),
        ),
        scratch_shapes=[pltpu.VMEM((1, LANES), F32)],
        compiler_params=_params("arbitrary"),
        name="router",
    )(x2, g, whl, br)


def _row_copy(src_ref, src_row, dst_ref, dst_row, sem):
    return pltpu.make_async_copy(src_ref.at[pl.ds(src_row, 1)], dst_ref.at[pl.ds(dst_row, 1)], sem)


def _dispatch_kernel(pe_ref, nu_ref, dest_ref, h_ref, xs_ref, zero_ref, sem, *, td, tmm, n_blocks):
    @pl.when(pl.program_id(0) == 0)
    def _():
        zero_ref[...] = jnp.zeros_like(zero_ref)

        def fill(start):
            return pltpu.make_async_copy(zero_ref, xs_ref.at[pl.ds(pl.multiple_of(start, tmm), tmm)], sem)

        for e in range(N_EXPERTS):
            prev_end = pe_ref[e - 1] if e > 0 else 0

            @pl.when(pe_ref[e] > prev_end)
            def _():
                fill(pe_ref[e] - tmm).start()
                fill(pe_ref[e] - tmm).wait()

        for j in range(N_EXPERTS):
            @pl.when(n_blocks - 1 - j >= nu_ref[0])
            def _():
                fill((n_blocks - 1 - j) * tmm).start()
                fill((n_blocks - 1 - j) * tmm).wait()

    for r in range(td):
        _row_copy(h_ref, r, xs_ref, dest_ref[2 * r], sem).start(priority=0)
        _row_copy(h_ref, r, xs_ref, dest_ref[2 * r + 1], sem).start(priority=1)
    for _ in range(2):
        pltpu.make_async_copy(h_ref, xs_ref.at[pl.ds(0, td)], sem).wait()


def _dispatch(pad_end, n_used, dest_flat, h, *, n_blocks, tmm, td=512):
    T = h.shape[0]
    return pl.pallas_call(
        functools.partial(_dispatch_kernel, td=td, tmm=tmm, n_blocks=n_blocks),
        out_shape=jax.ShapeDtypeStruct((n_blocks * tmm,) + ROW_TILE, F32),
        grid_spec=pltpu.PrefetchScalarGridSpec(
            num_scalar_prefetch=2,
            grid=(T // td,),
            in_specs=[
                pl.BlockSpec((2 * td,), lambda i, pe, nu: (i,), memory_space=pltpu.SMEM),
                pl.BlockSpec((td,) + ROW_TILE, lambda i, pe, nu: (i, 0, 0)),
            ],
            out_specs=pl.BlockSpec(memory_space=pl.ANY),
            scratch_shapes=[pltpu.VMEM((tmm,) + ROW_TILE, F32), pltpu.SemaphoreType.DMA],
        ),
        compiler_params=_params("arbitrary"),
        name="moe_dispatch",
    )(pad_end, n_used, dest_flat, h)


def _gmm_kernel(be_ref, nu_ref, xs_ref, wg_ref, wu_ref, wd_ref, o_ref, *, f_chunks):
    i = pl.program_id(0)

    @pl.when(i < nu_ref[0])
    def _():
        x = xs_ref[...].reshape(xs_ref.shape[0], D_MODEL).astype(BF16)
        y = None
        for lo, hi in f_chunks:
            g = _dot(x, wg_ref[0, :, lo:hi])
            u = _dot(x, wu_ref[0, :, lo:hi])
            part = _dot((g * _sigmoid(g) * u).astype(BF16), wd_ref[0, lo:hi, :])
            y = part if y is None else y + part
        o_ref[...] = y.reshape(o_ref.shape)

    @pl.when(i >= nu_ref[0])
    def _():
        o_ref[...] = jnp.zeros_like(o_ref)


def _gmm(block_e, n_used, xs, wg, wu, wd, *, tmm):
    n_rows = xs.shape[0]
    n_blocks = n_rows // tmm
    F = wd.shape[1]
    n_pass = F // MXU_PASS_COLS
    assert n_pass * MXU_PASS_COLS == F
    cut = (n_pass // 2) * MXU_PASS_COLS
    f_chunks = ((0, cut), (cut, F))

    def expert_spec(shape):
        return pl.BlockSpec((1,) + shape, lambda i, be, nu: (be[i], 0, 0))

    return pl.pallas_call(
        functools.partial(_gmm_kernel, f_chunks=f_chunks),
        out_shape=jax.ShapeDtypeStruct((n_rows,) + ROW_TILE, F32),
        grid_spec=pltpu.PrefetchScalarGridSpec(
            num_scalar_prefetch=2,
            grid=(n_blocks,),
            in_specs=[
                pl.BlockSpec((tmm,) + ROW_TILE, lambda i, be, nu: (jnp.minimum(i, nu[0] - 1), 0, 0)),
                expert_spec((D_MODEL, F)),
                expert_spec((D_MODEL, F)),
                expert_spec((F, D_MODEL)),
            ],
            out_specs=pl.BlockSpec((tmm,) + ROW_TILE, lambda i, be, nu: (i, 0, 0)),
        ),
        compiler_params=pltpu.CompilerParams(dimension_semantics=("arbitrary",), vmem_limit_bytes=GMM_VMEM_LIMIT),
        name="moe_grouped_swiglu",
    )(block_e, n_used, xs, wg, wu, wd)


def _combine_ple_kernel(dest_ref, dest_next_ref, x_ref, rw_ref, p_ref, gple_ref, wpg_ref, wpp_ref, gfin_ref,
                        yb_ref, o_ref, bufa_ref, bufb_ref, sem, *, tp):
    i = pl.program_id(0)
    th = tp // 2
    bufs = (bufa_ref, bufb_ref)

    def gather(d_ref, first, b):
        for r in range(th):
            _row_copy(yb_ref, d_ref[2 * (first + r)], bufs[b].at[0], r, sem.at[b]).start(priority=1)
            _row_copy(yb_ref, d_ref[2 * (first + r) + 1], bufs[b].at[1], r, sem.at[b]).start(priority=1)

    def wait(b):
        for k in range(2):
            pltpu.make_async_copy(yb_ref.at[pl.ds(0, th)], bufs[b].at[k], sem.at[b]).wait()

    def finish(rows, b):
        rw = rw_ref[rows, :]
        y0 = bufs[b][0].reshape(th, D_MODEL)
        y1 = bufs[b][1].reshape(th, D_MODEL)
        x1 = x_ref[rows, :] + rw[:, 0:1] * y0 + rw[:, 1:2] * y1
        x2 = _ple(x1, p_ref[rows, :], gple_ref[...], wpg_ref[...], wpp_ref[...])
        o_ref[rows, :] = _rms(x2, gfin_ref[...], RMS_EPS)

    @pl.when(i == 0)
    def _():
        gather(dest_ref, 0, 0)

    wait(0)
    gather(dest_ref, th, 1)
    finish(slice(0, th), 0)
    wait(1)
    gather(dest_next_ref, 0, 0)
    finish(slice(th, tp), 1)

    @pl.when(i == pl.num_programs(0) - 1)
    def _():
        wait(0)


def _combine_ple(dest_flat, x2, rw, p2, gple, wpg, wpp, gfin, yb, *, tp=512):
    T = x2.shape[0]
    n = T // tp
    return pl.pallas_call(
        functools.partial(_combine_ple_kernel, tp=tp),
        out_shape=jax.ShapeDtypeStruct((T, D_MODEL), F32),
        grid=(n,),
        in_specs=[
            pl.BlockSpec((2 * tp,), lambda i: (i,), memory_space=pltpu.SMEM),
            pl.BlockSpec((2 * tp,), lambda i: (jnp.minimum(i + 1, n - 1),), memory_space=pltpu.SMEM),
            pl.BlockSpec((tp, D_MODEL), lambda i: (i, 0)),
            pl.BlockSpec((tp, LANES), lambda i: (i, 0)),
            pl.BlockSpec((tp, PLE_DIM), lambda i: (i, 0)),
            _once((1, D_MODEL)),
            _once((D_MODEL, D_MODEL)),
            _once((PLE_DIM, D_MODEL)),
            _once((1, D_MODEL)),
            pl.BlockSpec(memory_space=pl.ANY),
        ],
        out_specs=pl.BlockSpec((tp, D_MODEL), lambda i: (i, 0)),
        scratch_shapes=[pltpu.VMEM((2, tp // 2) + ROW_TILE, F32), pltpu.VMEM((2, tp // 2) + ROW_TILE, F32),
                        pltpu.SemaphoreType.DMA((2,))],
        compiler_params=_params("arbitrary"),
        name="moe_combine_ple",
    )(dest_flat, dest_flat, x2, rw, p2, gple, wpg, wpp, gfin, yb)


def _moe_ple_final(x2, p2, gffn, w_router, b_router, wg, wu, wd, gple, wpg, wpp, gfin, *, tmm=512):
    T = x2.shape[0]
    wr = jnp.zeros((D_MODEL, LANES), F32).at[:, :N_EXPERTS].set(w_router)
    whi = wr.astype(BF16)
    wlo = (wr - whi.astype(F32)).astype(BF16)
    br = jnp.full((1, LANES), NEG_BIG, F32).at[0, :N_EXPERTS].set(b_router)
    h, ri, rw, cnt = _router(x2, gffn, jnp.concatenate([whi, wlo], axis=1), br)
    counts = cnt[0, :N_EXPERTS].astype(jnp.int32)
    padded = (counts + tmm - 1) // tmm * tmm
    pad_end = jnp.cumsum(padded).astype(jnp.int32)
    pad_start = pad_end - padded
    e_sel = ri[:, 0:2, None] == jnp.arange(N_EXPERTS, dtype=jnp.int32)
    dest = jnp.sum(jnp.where(e_sel, pad_start, 0), axis=-1) + ri[:, 2:4]
    dest_flat = dest.reshape(2 * T).astype(jnp.int32)
    n_blocks = (2 * T) // tmm + N_EXPERTS
    block_start = jnp.arange(n_blocks, dtype=jnp.int32) * tmm
    block_e = jnp.minimum(jnp.sum(block_start[:, None] >= pad_end[None, :], axis=1), N_EXPERTS - 1).astype(jnp.int32)
    n_used = (pad_end[-1] // tmm).astype(jnp.int32).reshape(1)
    xs = _dispatch(pad_end, n_used, dest_flat, h, n_blocks=n_blocks, tmm=tmm)
    yb = _gmm(block_e, n_used, xs, wg.astype(BF16), wu.astype(BF16), wd.astype(BF16), tmm=tmm)
    return _combine_ple(dest_flat, x2, rw, p2, gple, wpg, wpp, gfin, yb)


def kernel(x, p, g_mix, w_in, w_gk_up, b_gk, g_gla_norm, conv_w, conv_b, ln_conv_g, ln_conv_b, w_conv_out, w_pool, pool_scale, w_out, g_ffn, w_ffn_gate, w_ffn_up, w_ffn_down, w_router, b_router, w_moe_gate, w_moe_up, w_moe_down, g_ple, w_ple_gate, w_ple_proj, g_final):
    B, S, _ = x.shape
    T = B * S
    depth = w_in.shape[0]
    assert depth == 2 and x.shape[2] == D_MODEL, "dense layer 0, routed layer 1 with the final norm fused"
    x2 = x.reshape(T, D_MODEL)
    row = lambda a: a.reshape(1, -1)
    for i in range(depth):
        wi = w_in[i]
        w_main = jnp.concatenate([wi[:, :3072], wi[:, 3088:]], axis=1).astype(BF16)
        w_gk = jnp.zeros((D_MODEL, GK_PAD), F32).at[:, :GLA_GATE_RANK].set(wi[:, 3072:3088]).astype(BF16)
        w_gk_up_p = jnp.zeros((GK_PAD, GLA_KEY_DIM), F32).at[:GLA_GATE_RANK].set(w_gk_up[i]).astype(BF16)
        z, zgk = _in_proj(x2, row(g_mix[i]), w_main, w_gk)
        x2 = _mixer(z, zgk, x2, w_gk_up_p, row(b_gk[i]), row(g_gla_norm[i]), conv_w[i], row(conv_b[i]),
                    row(ln_conv_g[i]), row(ln_conv_b[i]), w_conv_out[i].astype(BF16), w_pool[i].astype(BF16),
                    row(pool_scale[i]), w_out[i].astype(BF16), B=B, S=S)
        p2 = p[i].reshape(T, PLE_DIM)
        wpg = w_ple_gate[i].astype(BF16)
        wpp = w_ple_proj[i].astype(BF16)
        if i % 2 == 0:
            j = i // 2
            wgu = jnp.concatenate([w_ffn_gate[j].astype(BF16), w_ffn_up[j].astype(BF16)], axis=1)
            x2 = _ffn_ple(x2, p2, row(g_ffn[i]), wgu, w_ffn_down[j].astype(BF16), row(g_ple[i]), wpg, wpp)
        else:
            j = i // 2
            x2 = _moe_ple_final(x2, p2, row(g_ffn[i]), w_router[j], b_router[j], w_moe_gate[j], w_moe_up[j],
                                w_moe_down[j], row(g_ple[i]), wpg, wpp, row(g_final))
    return x2.reshape(B, S, D_MODEL)
```
